```python
import jax, jax.numpy as jnp
from jax import lax
import numpy as np

D_MODEL = 2048
BATCH = 2
SEQ = 4096
DEPTH = 1
DEC_BATCH = 32
DEC_SEQ = 4
PAST_LEN = 16384
PAGE_SIZE = 128

NSA_HEADS = 16
NSA_KV_HEADS = 4
NSA_GROUP = NSA_HEADS // NSA_KV_HEADS
HEAD_DIM = 64
CMP_BLOCK = 32
CMP_STRIDE = 16
CMP_HIDDEN = 2 * HEAD_DIM
SEL_BLOCK = 64
N_SELECT = 16
WINDOW = 512
Q_BLOCK = 128
GLA_HEADS = 4
GLA_DK = 128
GLA_DV = 256
GLA_RANK = 16
GLA_TAU = 16.0
GLA_CHUNK = 64
D_FF = 5632
CONV_W = 3
PLE_DIM = 256
EPS = 1e-6
NEG_INF = -1e30
FORCE_SCORE = 1e9
SCALE = HEAD_DIM ** -0.5

NSA_WIDTH = NSA_HEADS * HEAD_DIM
GLA_WIDTH = GLA_HEADS * GLA_DV
MIX_WIDTH = NSA_WIDTH + GLA_WIDTH
KV_WIDTH = 2 * NSA_KV_HEADS * HEAD_DIM
IN_SPLITS = (NSA_WIDTH, KV_WIDTH, KV_WIDTH, KV_WIDTH, 3 * NSA_HEADS,
             GLA_HEADS * GLA_DK, GLA_HEADS * GLA_DK, GLA_WIDTH, GLA_WIDTH, GLA_RANK)
D_IN = sum(IN_SPLITS)
IN_OFFSETS = tuple(int(o) for o in np.cumsum(IN_SPLITS)[:-1])

kernel_name = 'hymba_nsa_gla_convffn_step'


def _rmsnorm(x, g):
    xf = x.astype(jnp.float32)
    y = xf * lax.rsqrt(jnp.mean(xf * xf, axis=-1, keepdims=True) + EPS)
    return (y * g.astype(jnp.float32)).astype(x.dtype)


def _alibi_slopes():
    h = np.arange(1, NSA_HEADS + 1, dtype=np.float32)
    return jnp.asarray(2.0 ** (-8.0 * h / NSA_HEADS), dtype=jnp.float32).reshape(NSA_KV_HEADS, NSA_GROUP)


def _compress(rows, w1, w2, pe):
    b, l = rows.shape[0], rows.shape[1]
    span = CMP_BLOCK // CMP_STRIDE
    n16 = l // CMP_STRIDE
    n_cmp = n16 - span + 1
    r = rows[:, :n16 * CMP_STRIDE].reshape(b, n16, CMP_STRIDE, NSA_KV_HEADS, HEAD_DIM)
    w1s = w1.reshape(span, CMP_STRIDE, HEAD_DIM, CMP_HIDDEN)
    hid = jnp.einsum('ld,ldf->f', pe, w1)
    for j in range(span):
        hid = hid + jnp.einsum('bcsnd,sdf->bcnf', r[:, j:j + n_cmp], w1s[j])
    return jnp.einsum('bcnf,fd->bcnd', jax.nn.gelu(hid), w2)


def _compress_kv(rows, cmp_w):
    w_k1, w_k2, pe_k, w_v1, w_v2, pe_v = cmp_w
    return (_compress(rows[..., 0, :], w_k1, w_k2, pe_k), _compress(rows[..., 1, :], w_v1, w_v2, pe_v))


def _cmp_attend(q, kc, vc, q_pos, slopes):
    n_cmp = kc.shape[1]
    s = jnp.einsum('btngd,bcnd->bngtc', q, kc).astype(jnp.float32) * SCALE
    end = jnp.arange(n_cmp, dtype=jnp.int32) * CMP_STRIDE + (CMP_BLOCK - 1)
    dist = (q_pos[:, None] - end[None, :]).astype(jnp.float32)
    valid = dist >= 0
    s = jnp.where(valid, s - slopes[None, :, :, None, None] * dist, NEG_INF)
    p = jax.nn.softmax(s, axis=-1) * valid
    o = jnp.einsum('bngtc,bcnd->btngd', p.astype(vc.dtype), vc)
    return o, jnp.sum(p, axis=2)


def _select_blocks(p_grp, q_pos, n_sel):
    n_cmp = p_grp.shape[-1]
    c = jnp.arange(n_cmp)[:, None]
    j = jnp.arange(n_sel)[None, :]
    per_sel = SEL_BLOCK // CMP_STRIDE
    overlap = sum(((c + n >= per_sel * j) & (c + n < per_sel * (j + 1))).astype(jnp.float32)
                  for n in range(CMP_BLOCK // CMP_STRIDE))
    imp = jnp.einsum('bntc,cj->bntj', p_grp, overlap)
    cur = (q_pos // SEL_BLOCK)[:, None]
    forced = (j == 0) | (j == cur) | (j == cur - 1)
    imp = jnp.where(forced, FORCE_SCORE, jnp.where(j > cur, -FORCE_SCORE, imp))
    _, idx = lax.top_k(imp, min(N_SELECT, n_sel))
    return idx.astype(jnp.int32)


def _sel_attend(q, kv, idx, q_pos, slopes):
    k_pos = idx[..., None] * SEL_BLOCK + jnp.arange(SEL_BLOCK, dtype=jnp.int32)
    dist = (q_pos[None, None, :, None, None] - k_pos).astype(jnp.float32)
    s = jnp.einsum('btngd,bntkld->bngtkl', q, kv[..., 0, :]).astype(jnp.float32) * SCALE
    s = jnp.where((dist >= 0)[:, :, None], s - slopes[None, :, :, None, None, None] * dist[:, :, None], NEG_INF)
    sh = s.shape
    p = jax.nn.softmax(s.reshape(sh[:4] + (-1,)), axis=-1).reshape(sh)
    return jnp.einsum('bngtkl,bntkld->btngd', p.astype(kv.dtype), kv[..., 1, :])


def _win_attend(q, kv, q_pos, k_pos, slopes):
    dist = (q_pos[:, None] - k_pos[None, :]).astype(jnp.float32)
    valid = (dist >= 0) & (dist < WINDOW) & (k_pos[None, :] >= 0)
    s = jnp.einsum('btngd,bsnd->bngts', q, kv[..., 0, :]).astype(jnp.float32) * SCALE
    s = jnp.where(valid, s - slopes[None, :, :, None, None] * dist, NEG_INF)
    p = jax.nn.softmax(s, axis=-1)
    return jnp.einsum('bngts,bsnd->btngd', p.astype(kv.dtype), kv[..., 1, :])


def _nsa_prompt(q, kvc, kvs, kvw, q_pos, cmp_w, slopes):
    b, t = q.shape[0], q.shape[1]
    kc, vc = _compress_kv(kvc, cmp_w)
    o_c, p_grp = _cmp_attend(q, kc, vc, q_pos, slopes)
    n_sel = t // SEL_BLOCK
    idx = _select_blocks(p_grp, q_pos, n_sel)
    blocks = kvs.reshape(b, n_sel, SEL_BLOCK, NSA_KV_HEADS, 2, HEAD_DIM)
    nqb = t // Q_BLOCK
    bi = jnp.arange(b)[:, None, None, None]
    ni = jnp.arange(NSA_KV_HEADS)[None, :, None, None]
    q_b = q.reshape(b, nqb, Q_BLOCK, NSA_KV_HEADS, NSA_GROUP, HEAD_DIM)
    idx_b = jnp.moveaxis(idx.reshape(b, NSA_KV_HEADS, nqb, Q_BLOCK, -1), 2, 0)
    pos_b = q_pos.reshape(nqb, Q_BLOCK)

    def sel_block(args):
        qb, ib, pb = args
        return _sel_attend(qb, blocks[bi, ib, :, ni], ib, pb, slopes)

    o_s = jnp.moveaxis(lax.map(sel_block, (jnp.moveaxis(q_b, 1, 0), idx_b, pos_b)), 0, 1).reshape(q.shape)
    kvw_pad = jnp.pad(kvw, ((0, 0), (WINDOW, 0), (0, 0), (0, 0), (0, 0)))
    rows = jnp.arange(nqb, dtype=jnp.int32)[:, None] * Q_BLOCK + jnp.arange(Q_BLOCK + WINDOW, dtype=jnp.int32)[None, :]
    o_w = jax.vmap(_win_attend, in_axes=(1, 1, 0, 0, None), out_axes=1)(
        q_b, kvw_pad[:, rows], pos_b, rows - WINDOW, slopes).reshape(q.shape)
    return o_c, o_s, o_w


def _nsa_sample(q, kvc, kvs, kvw, q_pos, cache_c, cache_s, cache_w, page_table, cmp_w, slopes):
    b, t = q.shape[0], q.shape[1]
    past_len = page_table.shape[1] * PAGE_SIZE
    past_c = cache_c[page_table].reshape(b, past_len, NSA_KV_HEADS, 2, HEAD_DIM)
    kc, vc = _compress_kv(jnp.concatenate([past_c, kvc], axis=1), cmp_w)
    o_c, p_grp = _cmp_attend(q, kc, vc, q_pos, slopes)
    nb_past = past_len // SEL_BLOCK
    n_tail = -(-t // SEL_BLOCK)
    idx = _select_blocks(p_grp, q_pos, nb_past + n_tail)
    bi = jnp.arange(b)[:, None, None, None]
    ni = jnp.arange(NSA_KV_HEADS)[None, :, None, None]
    per_page = PAGE_SIZE // SEL_BLOCK
    jp = jnp.minimum(idx, nb_past - 1)
    phys = page_table[bi, jp // per_page]
    rows = (jp % per_page)[..., None] * SEL_BLOCK + jnp.arange(SEL_BLOCK, dtype=jnp.int32)
    past_kv = cache_s[phys[..., None], rows, ni[..., None]]
    tail = jnp.pad(kvs, ((0, 0), (0, n_tail * SEL_BLOCK - t), (0, 0), (0, 0), (0, 0)))
    tail = tail.reshape(b, n_tail, SEL_BLOCK, NSA_KV_HEADS, 2, HEAD_DIM)
    tail_kv = tail[bi, jnp.clip(idx - nb_past, 0, n_tail - 1), :, ni]
    kv_sel = jnp.where((idx >= nb_past)[..., None, None, None], tail_kv, past_kv)
    o_s = _sel_attend(q, kv_sel, idx, q_pos, slopes)
    buf_len = cache_w.shape[1]
    k_pos = jnp.concatenate([past_len - buf_len + jnp.arange(buf_len, dtype=jnp.int32),
                             past_len + jnp.arange(t, dtype=jnp.int32)])
    o_w = _win_attend(q, jnp.concatenate([cache_w, kvw], axis=1), q_pos, k_pos, slopes)
    return o_c, o_s, o_w


def _gla(q, k, v, lg, s0):
    b, h, t, _ = q.shape
    c = min(GLA_CHUNK, t)
    nc = -(-t // c)
    pad = nc * c - t

    def chunks(a):
        a = jnp.pad(a, ((0, 0), (0, 0), (0, pad), (0, 0)))
        return jnp.moveaxis(a.reshape(b, h, nc, c, a.shape[-1]), 2, 0)

    causal = jnp.tril(jnp.ones((c, c), dtype=bool))[:, :, None]

    def step(s, xs):
        qc, kc, vc, gc = xs
        cum = jnp.cumsum(gc, axis=2)
        inter = jnp.einsum('bhtk,bhkv->bhtv', qc * jnp.exp(cum), s)
        diff = cum[:, :, :, None, :] - cum[:, :, None, :, :]
        decay = jnp.where(causal, jnp.exp(jnp.minimum(diff, 0.0)), 0.0)
        att = jnp.einsum('bhtk,bhsk,bhtsk->bhts', qc, kc, decay)
        out = inter + jnp.einsum('bhts,bhsv->bhtv', att, vc)
        last = cum[:, :, -1:, :]
        s_new = jnp.exp(last[:, :, 0, :])[..., None] * s + jnp.einsum('bhsk,bhsv->bhkv', kc * jnp.exp(last - cum), vc)
        return s_new, out

    s_fin, o = lax.scan(step, s0, (chunks(q), chunks(k), chunks(v), chunks(lg)))
    o = jnp.moveaxis(o, 0, 2).reshape(b, h, nc * c, v.shape[-1])[:, :, :t]
    return o, s_fin


def _conv_ffn(h, buf, w_up, conv_w, conv_b, w_down):
    t = h.shape[1]
    a, gate = jnp.split(h @ w_up, 2, axis=-1)
    ext = jnp.concatenate([buf.astype(a.dtype), a], axis=1)
    conv = conv_b + sum(ext[:, j:j + t] * conv_w[j] for j in range(CONV_W))
    y = (jax.nn.gelu(conv) * gate) @ w_down
    return y, ext[:, t:]


def _layer(x, p_emb, q_pos, past, slopes, g_attn, w_in, w_cmp_k1, w_cmp_k2, pe_cmp_k, w_cmp_v1, w_cmp_v2, pe_cmp_v,
           w_gla_a2, b_gla_a, g_nsa_out, g_gla_out, w_out, g_ffn, w_up, conv_w, conv_b, w_down,
           w_ple_proj, g_ple, w_ple_gate):
    b, t, _ = x.shape
    n1 = _rmsnorm(x, g_attn)
    q, kvc, kvs, kvw, gl, gq, gk, gv, gr, ga = jnp.split(n1 @ w_in, IN_OFFSETS, axis=-1)
    q = q.reshape(b, t, NSA_KV_HEADS, NSA_GROUP, HEAD_DIM)
    kv_shape = (b, t, NSA_KV_HEADS, 2, HEAD_DIM)
    kvc, kvs, kvw = kvc.reshape(kv_shape), kvs.reshape(kv_shape), kvw.reshape(kv_shape)
    gates = jax.nn.sigmoid(gl.astype(jnp.float32)).reshape(b, t, NSA_KV_HEADS, NSA_GROUP, 3).astype(x.dtype)
    cmp_w = (w_cmp_k1, w_cmp_k2, pe_cmp_k, w_cmp_v1, w_cmp_v2, pe_cmp_v)
    if past is None:
        o_c, o_s, o_w = _nsa_prompt(q, kvc, kvs, kvw, q_pos, cmp_w, slopes)
        win_rows = kvw[:, t - min(WINDOW, t):]
        s0 = jnp.zeros((b, GLA_HEADS, GLA_DK, GLA_DV), jnp.float32)
        conv_buf = jnp.zeros((b, CONV_W - 1, D_FF), x.dtype)
    else:
        cache_c, cache_s, cache_w, page_table, s0, conv_buf = past
        o_c, o_s, o_w = _nsa_sample(q, kvc, kvs, kvw, q_pos, cache_c, cache_s, cache_w, page_table, cmp_w, slopes)
        win_rows = kvw
        s0 = s0.astype(jnp.float32)
    o_nsa = gates[..., 0:1] * o_c + gates[..., 1:2] * o_s + gates[..., 2:3] * o_w
    o_nsa = _rmsnorm(o_nsa.reshape(b, t, NSA_WIDTH), g_nsa_out)

    def heads(a, d):
        return a.reshape(b, t, GLA_HEADS, d).transpose(0, 2, 1, 3).astype(jnp.float32)

    log_a = jax.nn.log_sigmoid((ga @ w_gla_a2 + b_gla_a).astype(jnp.float32)) / GLA_TAU
    o_g, s_new = _gla(heads(gq, GLA_DK) * GLA_DK ** -0.5, heads(gk, GLA_DK), heads(gv, GLA_DV),
                      heads(log_a, GLA_DK), s0)
    o_g = _rmsnorm(o_g.transpose(0, 2, 1, 3).astype(x.dtype), g_gla_out).reshape(b, t, GLA_WIDTH) * jax.nn.silu(gr)
    h = x + jnp.concatenate([o_nsa, o_g], axis=-1) @ w_out
    f, conv_new = _conv_ffn(_rmsnorm(h, g_ffn), conv_buf, w_up, conv_w, conv_b, w_down)
    h = h + f
    h = h + jax.nn.sigmoid(h @ w_ple_gate) * _rmsnorm(p_emb @ w_ple_proj, g_ple)
    return h, (kvc, kvs, win_rows, s_new.astype(x.dtype), conv_new)


def setup_inputs(seed: int = 0) -> dict:
    key = jax.random.key(seed)
    ks = iter(jax.random.split(key, 48))

    def nrm(shape, scale):
        return jax.random.normal(next(ks), shape, jnp.float32) * scale

    def gain(shape):
        return 1.0 + nrm(shape, 0.01)

    n_pages = PAST_LEN // PAGE_SIZE
    n_used = DEC_BATCH * n_pages
    n_pool = n_used + max(1, n_used // 4)
    win_buf = min(WINDOW, PAST_LEN)
    page_table = jax.random.permutation(next(ks), n_pool)[:n_used].reshape(DEC_BATCH, n_pages).astype(jnp.int32)
    pool_shape = (DEPTH, n_pool, PAGE_SIZE, NSA_KV_HEADS, 2, HEAD_DIM)
    return {
        'x_prompt': nrm((BATCH, SEQ, D_MODEL), 1.0),
        'x_sample': nrm((DEC_BATCH, DEC_SEQ, D_MODEL), 1.0),
        'p_prompt': nrm((DEPTH, BATCH, SEQ, PLE_DIM), 1.0),
        'p_sample': nrm((DEPTH, DEC_BATCH, DEC_SEQ, PLE_DIM), 1.0),
        'cache_cmp_kv': nrm(pool_shape, 1.0),
        'cache_sel_kv': nrm(pool_shape, 1.0),
        'cache_win_kv': nrm((DEPTH, DEC_BATCH, win_buf, NSA_KV_HEADS, 2, HEAD_DIM), 1.0),
        'state_gla': nrm((DEPTH, DEC_BATCH, GLA_HEADS, GLA_DK, GLA_DV), 1.0),
        'state_ffn_conv': nrm((DEPTH, DEC_BATCH, CONV_W - 1, D_FF), 1.0),
        'page_table': page_table,
        'g_attn': gain((DEPTH, D_MODEL)),
        'w_in': nrm((DEPTH, D_MODEL, D_IN), D_MODEL ** -0.5),
        'w_cmp_k1': nrm((DEPTH, CMP_BLOCK, HEAD_DIM, CMP_HIDDEN), (CMP_BLOCK * HEAD_DIM) ** -0.5),
        'w_cmp_k2': nrm((DEPTH, CMP_HIDDEN, HEAD_DIM), CMP_HIDDEN ** -0.5),
        'pe_cmp_k': nrm((DEPTH, CMP_BLOCK, HEAD_DIM), 0.1),
        'w_cmp_v1': nrm((DEPTH, CMP_BLOCK, HEAD_DIM, CMP_HIDDEN), (CMP_BLOCK * HEAD_DIM) ** -0.5),
        'w_cmp_v2': nrm((DEPTH, CMP_HIDDEN, HEAD_DIM), CMP_HIDDEN ** -0.5),
        'pe_cmp_v': nrm((DEPTH, CMP_BLOCK, HEAD_DIM), 0.1),
        'w_gla_a2': nrm((DEPTH, GLA_RANK, GLA_HEADS * GLA_DK), GLA_RANK ** -0.5),
        'b_gla_a': nrm((DEPTH, GLA_HEADS * GLA_DK), 0.1),
        'g_nsa_out': gain((DEPTH, NSA_WIDTH)),
        'g_gla_out': gain((DEPTH, GLA_DV)),
        'w_out': nrm((DEPTH, MIX_WIDTH, D_MODEL), MIX_WIDTH ** -0.5),
        'g_ffn': gain((DEPTH, D_MODEL)),
        'w_up': nrm((DEPTH, D_MODEL, 2 * D_FF), D_MODEL ** -0.5),
        'conv_w': nrm((DEPTH, CONV_W, D_FF), CONV_W ** -0.5),
        'conv_b': nrm((DEPTH, D_FF), 0.01),
        'w_down': nrm((DEPTH, D_FF, D_MODEL), D_FF ** -0.5),
        'w_ple_proj': nrm((DEPTH, PLE_DIM, D_MODEL), PLE_DIM ** -0.5),
        'g_ple': gain((DEPTH, D_MODEL)),
        'w_ple_gate': nrm((DEPTH, D_MODEL, D_MODEL), D_MODEL ** -0.5),
        'g_final': gain((D_MODEL,)),
    }


def reference(x_prompt, x_sample, p_prompt, p_sample, cache_cmp_kv, cache_sel_kv, cache_win_kv, state_gla,
              state_ffn_conv, page_table, g_attn, w_in, w_cmp_k1, w_cmp_k2, pe_cmp_k, w_cmp_v1, w_cmp_v2, pe_cmp_v,
              w_gla_a2, b_gla_a, g_nsa_out, g_gla_out, w_out, g_ffn, w_up, conv_w, conv_b, w_down,
              w_ple_proj, g_ple, w_ple_gate, g_final):
    slopes = _alibi_slopes()
    pos_p = jnp.arange(x_prompt.shape[1], dtype=jnp.int32)
    past_len = page_table.shape[1] * PAGE_SIZE
    pos_s = past_len + jnp.arange(x_sample.shape[1], dtype=jnp.int32)
    hp, hs = x_prompt, x_sample
    new_p = [[] for _ in range(5)]
    new_s = [[] for _ in range(5)]
    for i in range(DEPTH):
        w = (g_attn[i], w_in[i], w_cmp_k1[i], w_cmp_k2[i], pe_cmp_k[i], w_cmp_v1[i], w_cmp_v2[i], pe_cmp_v[i],
             w_gla_a2[i], b_gla_a[i], g_nsa_out[i], g_gla_out[i], w_out[i], g_ffn[i], w_up[i], conv_w[i],
             conv_b[i], w_down[i], w_ple_proj[i], g_ple[i], w_ple_gate[i])
        hp, st_p = _layer(hp, p_prompt[i], pos_p, None, slopes, *w)
        past = (cache_cmp_kv[i], cache_sel_kv[i], cache_win_kv[i], page_table, state_gla[i], state_ffn_conv[i])
        hs, st_s = _layer(hs, p_sample[i], pos_s, past, slopes, *w)
        for lst, a in zip(new_p, st_p):
            lst.append(a)
        for lst, a in zip(new_s, st_s):
            lst.append(a)
    y_prompt = _rmsnorm(hp, g_final)
    y_sample = _rmsnorm(hs, g_final)
    cmp_p, sel_p, win_p, gla_p, conv_p = [jnp.stack(l) for l in new_p]
    cmp_s, sel_s, win_s, gla_s, conv_s = [jnp.stack(l) for l in new_s]
    return (y_prompt, y_sample, cmp_p, sel_p, win_p, gla_p, conv_p, cmp_s, sel_s, win_s, gla_s, conv_s)
```

```python
import functools

import numpy as np
import jax
import jax.numpy as jnp
from jax import lax
from jax.experimental import pallas as pl
from jax.experimental.pallas import tpu as pltpu

F32 = jnp.float32
BF16 = jnp.bfloat16
HIGHEST = lax.Precision.HIGHEST

D_MODEL = 2048
PAGE_SIZE = 128
NSA_HEADS = 16
NSA_KV_HEADS = 4
NSA_GROUP = NSA_HEADS // NSA_KV_HEADS
HEAD_DIM = 64
CMP_BLOCK = 32
CMP_STRIDE = 16
CMP_HIDDEN = 2 * HEAD_DIM
SEL_BLOCK = 64
N_SELECT = 16
WINDOW = 512
Q_BLOCK = 128
GLA_HEADS = 4
GLA_DK = 128
GLA_DV = 256
GLA_RANK = 16
GLA_TAU = 16.0
D_FF = 5632
CONV_W = 3
PLE_DIM = 256
EPS = 1e-6
NEG_INF = -1e30
FORCE_SCORE = 1e9
REMOVED_SCORE = -3e38
SCALE = HEAD_DIM ** -0.5

NSA_WIDTH = NSA_HEADS * HEAD_DIM
GLA_WIDTH = GLA_HEADS * GLA_DV
KV_WIDTH = 2 * NSA_KV_HEADS * HEAD_DIM
KV_SLAB = 2 * HEAD_DIM
Q_SLAB = NSA_GROUP * HEAD_DIM
GATE_W = 3 * NSA_HEADS

LANES = 128
SUBLANES = 8
VMEM_LIMIT = 56 * 1024 * 1024

COL_Q = 0
COL_KVC = COL_Q + NSA_WIDTH
COL_KVS = COL_KVC + KV_WIDTH
COL_KVW = COL_KVS + KV_WIDTH
COL_GQ = COL_KVW + KV_WIDTH
COL_GK = COL_GQ + GLA_HEADS * GLA_DK
COL_GV = COL_GK + GLA_HEADS * GLA_DK
COL_GR = COL_GV + GLA_WIDTH
COL_MISC = COL_GR + GLA_WIDTH
D_IN_PAD = COL_MISC + LANES
MISC_GA = GATE_W


def _cparams(sem):
    return pltpu.CompilerParams(dimension_semantics=sem, vmem_limit_bytes=VMEM_LIMIT)


def _rms(x, g):
    return x * lax.rsqrt(jnp.mean(x * x, axis=-1, keepdims=True) + EPS) * g


def _gelu_tanh(x):
    return 0.5 * x * (1.0 + jnp.tanh(np.float32(np.sqrt(2.0 / np.pi)) * (x + 0.044715 * (x * x * x))))


def _sigmoid(x):
    return 1.0 / (1.0 + jnp.exp(-x))


def _dot(a, b):
    return jnp.dot(a, b, preferred_element_type=F32)


def _dot_nt(a, b):
    return lax.dot_general(a, b, (((1,), (1,)), ((), ())), preferred_element_type=F32)


def _dot_tn(a, b):
    return lax.dot_general(a, b, (((0,), (0,)), ((), ())), preferred_element_type=F32)


def _norm_matmul_kernel(x_ref, g_ref, w_ref, o_ref, xn_ref):
    @pl.when(pl.program_id(1) == 0)
    def _():
        xn_ref[...] = _rms(x_ref[...], g_ref[...]).astype(BF16)

    o_ref[...] = _dot(xn_ref[...], w_ref[...])


def _norm_matmul(x, g, w, tm, tn):
    m, d = x.shape
    n = w.shape[1]
    return pl.pallas_call(
        _norm_matmul_kernel,
        grid=(m // tm, n // tn),
        in_specs=[pl.BlockSpec((tm, d), lambda i, j: (i, 0)),
                  pl.BlockSpec((1, d), lambda i, j: (0, 0)),
                  pl.BlockSpec((d, tn), lambda i, j: (0, j))],
        out_specs=pl.BlockSpec((tm, tn), lambda i, j: (i, j)),
        out_shape=jax.ShapeDtypeStruct((m, n), F32),
        scratch_shapes=[pltpu.VMEM((tm, d), BF16)],
        compiler_params=_cparams(("parallel", "arbitrary")),
        name="in_proj",
    )(x, g, w)


def _cmp_pe_hidden(pe_ref, w1_ref):
    span = CMP_BLOCK // CMP_STRIDE
    acc = jnp.zeros((SUBLANES, 2 * CMP_HIDDEN), F32)
    for j in range(span):
        for s in range(CMP_STRIDE):
            row = jnp.broadcast_to(pe_ref[pl.ds(j * CMP_STRIDE + s, 1), :], (SUBLANES, KV_SLAB)).astype(BF16)
            acc = acc + _dot(row, w1_ref[s][:, j * 2 * CMP_HIDDEN:(j + 1) * 2 * CMP_HIDDEN])
    return acc[0:1, :]


def _cmp_prompt_kernel(x_ref, w1_ref, w2_ref, pe_ref, o_ref, *, n16):
    acc = jnp.zeros((n16, 4 * CMP_HIDDEN), F32)
    for s in range(CMP_STRIDE):
        xs = x_ref[pl.ds(s, n16, stride=CMP_STRIDE), :].astype(BF16)
        acc = acc + _dot(xs, w1_ref[s])
    a = acc[:, :2 * CMP_HIDDEN]
    b_next = pltpu.roll(acc[:, 2 * CMP_HIDDEN:], n16 - 1, axis=0)
    hid = a + b_next + _cmp_pe_hidden(pe_ref, w1_ref)
    o_ref[...] = _dot(_gelu_tanh(hid).astype(BF16), w2_ref[...])


def _cmp_weights(w_k1, w_k2, pe_k, w_v1, w_v2, pe_v):
    z = jnp.zeros((CMP_BLOCK, HEAD_DIM, CMP_HIDDEN), F32)
    wl = jnp.concatenate([jnp.concatenate([w_k1, z], axis=2), jnp.concatenate([z, w_v1], axis=2)], axis=1)
    w1 = jnp.concatenate([wl[:CMP_STRIDE], wl[CMP_STRIDE:]], axis=2).astype(BF16)
    z2 = jnp.zeros((CMP_HIDDEN, HEAD_DIM), F32)
    w2 = jnp.concatenate([jnp.concatenate([w_k2, z2], axis=1), jnp.concatenate([z2, w_v2], axis=1)], axis=0)
    pe = jnp.concatenate([pe_k, pe_v], axis=1)
    return w1, w2.astype(BF16), pe


def _cmp_prompt(proj, b, t, w1, w2, pe):
    n16 = t // CMP_STRIDE
    col0 = COL_KVC // KV_SLAB
    return pl.pallas_call(
        functools.partial(_cmp_prompt_kernel, n16=n16),
        grid=(b, NSA_KV_HEADS),
        in_specs=[pl.BlockSpec((t, KV_SLAB), lambda i, n: (i, col0 + n)),
                  pl.BlockSpec(w1.shape, lambda i, n: (0, 0, 0)),
                  pl.BlockSpec(w2.shape, lambda i, n: (0, 0)),
                  pl.BlockSpec(pe.shape, lambda i, n: (0, 0))],
        out_specs=pl.BlockSpec((None, None, n16, KV_SLAB), lambda i, n: (i, n, 0, 0)),
        out_shape=jax.ShapeDtypeStruct((b, NSA_KV_HEADS, n16, KV_SLAB), F32),
        compiler_params=_cparams(("parallel", "parallel")),
        name="cmp_prompt",
    )(proj, w1, w2, pe)


def _overlap_matrix(n_cmp_rows, n_sel_cols, shift):
    c = lax.broadcasted_iota(jnp.int32, (n_cmp_rows, n_sel_cols), 0) - shift
    j = lax.broadcasted_iota(jnp.int32, (n_cmp_rows, n_sel_cols), 1)
    per_sel = SEL_BLOCK // CMP_STRIDE
    ov = jnp.zeros((n_cmp_rows, n_sel_cols), F32)
    for n in range(CMP_BLOCK // CMP_STRIDE):
        ov = ov + ((c + n >= per_sel * j) & (c + n < per_sel * (j + 1))).astype(F32)
    return jnp.where(c >= 0, ov, 0.0)


def _top_blocks(imp, n_pick):
    rows, nsel = imp.shape
    j = lax.broadcasted_iota(jnp.int32, (rows, nsel), 1).astype(F32)
    mask = jnp.zeros((rows, nsel), F32)
    picks = []
    for _ in range(n_pick):
        m = jnp.max(imp, axis=-1, keepdims=True)
        jmin = jnp.min(jnp.where(imp == m, j, float(nsel)), axis=-1, keepdims=True)
        hit = j == jmin
        mask = jnp.where(hit, 1.0, mask)
        imp = jnp.where(hit, REMOVED_SCORE, imp)
        picks.append(jmin.astype(jnp.int32))
    return mask, picks


def _slope_col(slopes_ref, n, rows_per_head, rows):
    g = lax.broadcasted_iota(jnp.int32, (rows, 1), 0) // rows_per_head
    col = jnp.zeros((rows, 1), F32)
    for gg in range(NSA_GROUP):
        col = jnp.where(g == gg, slopes_ref[n * NSA_GROUP + gg], col)
    return col


def _pick_head_gates(gl, n):
    out = jnp.zeros((gl.shape[0], 3 * NSA_GROUP), F32)
    for nn in range(NSA_KV_HEADS):
        out = jnp.where(n == nn, gl[:, nn * 3 * NSA_GROUP:(nn + 1) * 3 * NSA_GROUP], out)
    return _sigmoid(out)


def _nsa_prompt_kernel(slopes_ref, q_ref, kcv_ref, kvs_ref, kvw_ref, misc_ref, expand_ref, o_ref, selk_ref,
                       *, n16, n_sel):
    n = pl.program_id(1)
    qb = pl.program_id(2)
    rows = NSA_GROUP * Q_BLOCK
    qf = q_ref[...] * SCALE
    q = jnp.concatenate([qf[:, g * HEAD_DIM:(g + 1) * HEAD_DIM] for g in range(NSA_GROUP)], axis=0).astype(BF16)
    tok = lax.broadcasted_iota(jnp.int32, (rows, 1), 0) % Q_BLOCK
    qpos = qb * Q_BLOCK + tok
    slope = _slope_col(slopes_ref, n, Q_BLOCK, rows)

    kcv = kcv_ref[...]
    kc = kcv[:, :HEAD_DIM].astype(BF16)
    vc = kcv[:, HEAD_DIM:].astype(BF16)
    s = _dot_nt(q, kc)
    end = lax.broadcasted_iota(jnp.int32, (1, n16), 1) * CMP_STRIDE + (CMP_BLOCK - 1)
    dist = (qpos - end).astype(F32)
    valid = (dist >= 0) & (end < n16 * CMP_STRIDE)
    s = jnp.where(valid, s - slope * dist, NEG_INF)
    e = jnp.exp(s - jnp.max(s, axis=-1, keepdims=True))
    p = jnp.where(valid, e / jnp.sum(e, axis=-1, keepdims=True), 0.0)
    o_c = _dot(p.astype(BF16), vc)
    p_grp = p[0:Q_BLOCK]
    for g in range(1, NSA_GROUP):
        p_grp = p_grp + p[g * Q_BLOCK:(g + 1) * Q_BLOCK]
    imp = jnp.dot(p_grp, _overlap_matrix(n16, n_sel, 0), preferred_element_type=F32, precision=HIGHEST)

    j = lax.broadcasted_iota(jnp.int32, (Q_BLOCK, n_sel), 1)
    cur = qpos[0:Q_BLOCK] // SEL_BLOCK
    forced = (j == 0) | (j == cur) | (j == cur - 1)
    imp = jnp.where(forced, FORCE_SCORE, jnp.where(j > cur, -FORCE_SCORE, imp))
    mask, _ = _top_blocks(imp, min(N_SELECT, n_sel))
    selk_ref[...] = _dot(mask.astype(BF16), expand_ref[...])

    def attend(slab_ref, first_tile, n_tiles, ok_fn):
        def body(i, carry):
            m, l, acc = carry
            kt = first_tile + i
            kv = slab_ref[pl.ds(pl.multiple_of(kt * Q_BLOCK, Q_BLOCK), Q_BLOCK), :]
            k = kv[:, :HEAD_DIM].astype(BF16)
            v = kv[:, HEAD_DIM:].astype(BF16)
            kpos = kt * Q_BLOCK + lax.broadcasted_iota(jnp.int32, (1, Q_BLOCK), 1)
            dist = (qpos - kpos).astype(F32)
            ok = ok_fn(kt, dist)
            sc = jnp.where(ok, _dot_nt(q, k) - slope * dist, NEG_INF)
            m_new = jnp.maximum(m, jnp.max(sc, axis=-1, keepdims=True))
            alpha = jnp.exp(m - m_new)
            pr = jnp.where(ok, jnp.exp(sc - m_new), 0.0)
            l = alpha * l + jnp.sum(pr, axis=-1, keepdims=True)
            acc = alpha * acc + _dot(pr.astype(BF16), v)
            return m_new, l, acc

        init = (jnp.full((rows, 1), NEG_INF, F32), jnp.zeros((rows, 1), F32), jnp.zeros((rows, HEAD_DIM), F32))
        _, l, acc = lax.fori_loop(0, n_tiles, body, init)
        return acc / l

    def sel_ok(kt, dist):
        picked = selk_ref[:, pl.ds(pl.multiple_of(kt * Q_BLOCK, Q_BLOCK), Q_BLOCK)] > 0.5
        return jnp.concatenate([picked] * NSA_GROUP, axis=0) & (dist >= 0)

    def win_ok(kt, dist):
        return (dist >= 0) & (dist < WINDOW)

    o_s = attend(kvs_ref, 0, qb + 1, sel_ok)
    first = jnp.maximum(qb - WINDOW // Q_BLOCK, 0)
    o_w = attend(kvw_ref, first, qb + 1 - first, win_ok)

    gates = _pick_head_gates(misc_ref[:, 0:GATE_W], n)
    outs = []
    for g in range(NSA_GROUP):
        r = slice(g * Q_BLOCK, (g + 1) * Q_BLOCK)
        outs.append(gates[:, 3 * g:3 * g + 1] * o_c[r] + gates[:, 3 * g + 1:3 * g + 2] * o_s[r]
                    + gates[:, 3 * g + 2:3 * g + 3] * o_w[r])
    o_ref[...] = jnp.concatenate(outs, axis=1)


def _nsa_prompt(proj, kcv, slopes, b, t):
    n16 = t // CMP_STRIDE
    n_sel = t // SEL_BLOCK
    nqb = t // Q_BLOCK
    blk = np.arange(t) // SEL_BLOCK
    expand = jnp.asarray((np.arange(n_sel)[:, None] == blk[None, :]).astype(np.float32), dtype=BF16)
    grid_spec = pltpu.PrefetchScalarGridSpec(
        num_scalar_prefetch=1,
        grid=(b, NSA_KV_HEADS, nqb),
        in_specs=[pl.BlockSpec((Q_BLOCK, Q_SLAB), lambda i, n, qb, sl: (i * nqb + qb, COL_Q // Q_SLAB + n)),
                  pl.BlockSpec((None, None, n16, KV_SLAB), lambda i, n, qb, sl: (i, n, 0, 0)),
                  pl.BlockSpec((t, KV_SLAB), lambda i, n, qb, sl: (i, COL_KVS // KV_SLAB + n)),
                  pl.BlockSpec((t, KV_SLAB), lambda i, n, qb, sl: (i, COL_KVW // KV_SLAB + n)),
                  pl.BlockSpec((Q_BLOCK, LANES), lambda i, n, qb, sl: (i * nqb + qb, COL_MISC // LANES)),
                  pl.BlockSpec((n_sel, t), lambda i, n, qb, sl: (0, 0))],
        out_specs=pl.BlockSpec((Q_BLOCK, Q_SLAB), lambda i, n, qb, sl: (i * nqb + qb, n)),
        scratch_shapes=[pltpu.VMEM((Q_BLOCK, t), F32)],
    )
    return pl.pallas_call(
        functools.partial(_nsa_prompt_kernel, n16=n16, n_sel=n_sel),
        grid_spec=grid_spec,
        out_shape=jax.ShapeDtypeStruct((b * t, NSA_WIDTH), F32),
        compiler_params=_cparams(("parallel", "parallel", "arbitrary")),
        name="nsa_prompt",
    )(slopes, proj, kcv, proj, proj, proj, expand)


def _gla_kernel(q_ref, k_ref, v_ref, r_ref, misc_ref, wa_ref, ba_ref, gout_ref, s0_ref, o_ref, sfin_ref, state_ref,
                *, chunk, sub, valid_rows):
    c = pl.program_id(2)

    @pl.when(c == 0)
    def _():
        state_ref[...] = s0_ref[...]

    ga = misc_ref[:, MISC_GA:MISC_GA + GLA_RANK]
    x = jnp.dot(ga, wa_ref[...], preferred_element_type=F32, precision=HIGHEST) + ba_ref[...]
    lg = (jnp.minimum(x, 0.0) - jnp.log1p(jnp.exp(-jnp.abs(x)))) / GLA_TAU
    row = lax.broadcasted_iota(jnp.int32, (chunk, 1), 0)
    if valid_rows < chunk:
        lg = jnp.where(row < valid_rows, lg, 0.0)
    tri = (lax.broadcasted_iota(jnp.int32, (chunk, chunk), 0)
           >= lax.broadcasted_iota(jnp.int32, (chunk, chunk), 1)).astype(F32)
    cum = jnp.dot(tri, lg, preferred_element_type=F32, precision=HIGHEST)
    q = q_ref[...] * (GLA_DK ** -0.5)
    k = k_ref[...]
    v = v_ref[...]
    vb = v.astype(BF16)
    state = state_ref[...]
    inter = _dot((q * jnp.exp(cum)).astype(BF16), state.astype(BF16))

    outs = []
    for i in range(chunk // sub):
        r0 = i * sub
        qi, ki, ci, vi = q[r0:r0 + sub], k[r0:r0 + sub], cum[r0:r0 + sub], v[r0:r0 + sub]
        o_i = inter[r0:r0 + sub]
        if i > 0:
            anchor = cum[r0:r0 + 1]
            qd = (qi * jnp.exp(ci - anchor)).astype(BF16)
            kd = (k[0:r0] * jnp.exp(anchor - cum[0:r0])).astype(BF16)
            o_i = o_i + _dot(_dot_nt(qd, kd).astype(BF16), vb[0:r0])
        trow = lax.broadcasted_iota(jnp.int32, (sub, 1), 0)
        for s_ in range(sub):
            w = jnp.sum(qi * (ki[s_:s_ + 1] * jnp.exp(jnp.minimum(ci - ci[s_:s_ + 1], 0.0))), axis=-1, keepdims=True)
            o_i = o_i + jnp.where(trow >= s_, w, 0.0) * vi[s_:s_ + 1]
        outs.append(o_i)
    o = jnp.concatenate(outs, axis=0) if len(outs) > 1 else outs[0]

    last = cum[chunk - 1:chunk]
    kdec = (k * jnp.exp(last - cum)).astype(BF16)
    decay_col = jnp.transpose(jnp.broadcast_to(jnp.exp(last), (SUBLANES, GLA_DK)))[:, 0:1]
    new_state = decay_col * state + _dot_tn(kdec, vb)
    state_ref[...] = new_state

    @pl.when(c == pl.num_programs(2) - 1)
    def _():
        sfin_ref[...] = new_state

    o_ref[...] = _rms(o, gout_ref[...]) * (r_ref[...] * _sigmoid(r_ref[...]))


def _gla(proj, w_a2, b_a, g_out, s0, b, t, chunk, sub, valid_rows):
    nck = t // chunk
    hk = GLA_DK
    return pl.pallas_call(
        functools.partial(_gla_kernel, chunk=chunk, sub=sub, valid_rows=valid_rows),
        grid=(b, GLA_HEADS, nck),
        in_specs=[pl.BlockSpec((chunk, GLA_DK), lambda i, h, c: (i * nck + c, COL_GQ // GLA_DK + h)),
                  pl.BlockSpec((chunk, GLA_DK), lambda i, h, c: (i * nck + c, COL_GK // GLA_DK + h)),
                  pl.BlockSpec((chunk, GLA_DV), lambda i, h, c: (i * nck + c, COL_GV // GLA_DV + h)),
                  pl.BlockSpec((chunk, GLA_DV), lambda i, h, c: (i * nck + c, COL_GR // GLA_DV + h)),
                  pl.BlockSpec((chunk, LANES), lambda i, h, c: (i * nck + c, COL_MISC // LANES)),
                  pl.BlockSpec((GLA_RANK, hk), lambda i, h, c: (0, h)),
                  pl.BlockSpec((1, hk), lambda i, h, c: (0, h)),
                  pl.BlockSpec((1, GLA_DV), lambda i, h, c: (0, 0)),
                  pl.BlockSpec((None, None, GLA_DK, GLA_DV), lambda i, h, c: (i, h, 0, 0))],
        out_specs=[pl.BlockSpec((chunk, GLA_DV), lambda i, h, c: (i * nck + c, h)),
                   pl.BlockSpec((None, None, GLA_DK, GLA_DV), lambda i, h, c: (i, h, 0, 0))],
        out_shape=[jax.ShapeDtypeStruct((b * t, GLA_WIDTH), F32),
                   jax.ShapeDtypeStruct((b, GLA_HEADS, GLA_DK, GLA_DV), F32)],
        scratch_shapes=[pltpu.VMEM((GLA_DK, GLA_DV), F32)],
        compiler_params=_cparams(("parallel", "parallel", "arbitrary")),
        name="gla",
    )(proj, proj, proj, proj, proj, w_a2, b_a, g_out, s0)


def _out_proj_kernel(on_ref, og_ref, x_ref, g_ref, w_ref, o_ref, a_ref):
    @pl.when(pl.program_id(1) == 0)
    def _():
        a_ref[:, :NSA_WIDTH] = _rms(on_ref[...], g_ref[...]).astype(BF16)
        a_ref[:, NSA_WIDTH:] = og_ref[...].astype(BF16)

    o_ref[...] = x_ref[...] + _dot(a_ref[...], w_ref[...])


def _out_proj(o_nsa, o_gla, x, g_nsa, w_out, tm, tn):
    m = x.shape[0]
    return pl.pallas_call(
        _out_proj_kernel,
        grid=(m // tm, D_MODEL // tn),
        in_specs=[pl.BlockSpec((tm, NSA_WIDTH), lambda i, j: (i, 0)),
                  pl.BlockSpec((tm, GLA_WIDTH), lambda i, j: (i, 0)),
                  pl.BlockSpec((tm, tn), lambda i, j: (i, j)),
                  pl.BlockSpec((1, NSA_WIDTH), lambda i, j: (0, 0)),
                  pl.BlockSpec((NSA_WIDTH + GLA_WIDTH, tn), lambda i, j: (0, j))],
        out_specs=pl.BlockSpec((tm, tn), lambda i, j: (i, j)),
        out_shape=jax.ShapeDtypeStruct((m, D_MODEL), F32),
        scratch_shapes=[pltpu.VMEM((tm, NSA_WIDTH + GLA_WIDTH), BF16)],
        compiler_params=_cparams(("parallel", "arbitrary")),
        name="out_proj",
    )(o_nsa, o_gla, x, g_nsa, w_out)


def _ffn_kernel(h_ref, g_ref, wa_ref, wg_ref, cw_ref, cb_ref, wd_ref, p1_ref, p2_ref, o_ref, tail_ref,
                n2_ref, acc_ref, carry_ref, *, tm, tf, seq_rows):
    i = pl.program_id(0)
    j = pl.program_id(1)

    @pl.when(j == 0)
    def _():
        n2_ref[...] = _rms(h_ref[...], g_ref[...]).astype(BF16)
        acc_ref[...] = jnp.zeros_like(acc_ref)

    n2 = n2_ref[...]
    a = _dot(n2, wa_ref[...])
    gate = _dot(n2, wg_ref[...])
    row = lax.broadcasted_iota(jnp.int32, (tm, 1), 0)
    r1 = pltpu.roll(a, 1, axis=0)
    r2 = pltpu.roll(a, 2, axis=0)
    if seq_rows >= tm:
        cols = pl.ds(pl.multiple_of(j * tf, tf), tf)
        first = (i % (seq_rows // tm)) == 0
        prev = jnp.where(first, p2_ref[...], carry_ref[:, cols])
        a1 = jnp.where(row == 0, prev[1:2], r1)
        a2 = jnp.where(row == 0, prev[0:1], jnp.where(row == 1, prev[1:2], r2))
        carry_ref[:, cols] = a[tm - 2:tm]
    else:
        t = row % seq_rows
        a1 = jnp.where(t == 0, p1_ref[...], r1)
        a2 = jnp.where(t < 2, p2_ref[...], r2)
    cw = cw_ref[...]
    conv = cb_ref[...] + a2 * cw[0:1] + a1 * cw[1:2] + a * cw[2:3]
    y = (_gelu_tanh(conv) * gate).astype(BF16)
    acc_ref[...] += _dot(y, wd_ref[...])
    tail_ref[...] = a[tm - tail_ref.shape[0]:tm]

    @pl.when(j == pl.num_programs(1) - 1)
    def _():
        o_ref[...] = h_ref[...] + acc_ref[...]


def _ffn(h, g_ffn, w_up, conv_w, conv_b, w_down, p1, p2, tm, tf, seq_rows):
    m = h.shape[0]
    nj = D_FF // tf
    if seq_rows >= tm:
        tiles_per_seq = seq_rows // tm
        p1_spec = pl.BlockSpec((None, CONV_W - 1, tf), lambda i, j: (i // tiles_per_seq, 0, j))
        p2_spec = pl.BlockSpec((None, CONV_W - 1, tf), lambda i, j: (i // tiles_per_seq, 0, j))
    else:
        p1_spec = pl.BlockSpec((tm, tf), lambda i, j: (i, j))
        p2_spec = pl.BlockSpec((tm, tf), lambda i, j: (i, j))
    if seq_rows >= tm:
        tail_spec = pl.BlockSpec((None, SUBLANES, tf), lambda i, j: (i, 0, j))
        tail_shape = jax.ShapeDtypeStruct((m // tm, SUBLANES, D_FF), F32)
    else:
        tail_spec = pl.BlockSpec((tm, tf), lambda i, j: (i, j))
        tail_shape = jax.ShapeDtypeStruct((m, D_FF), F32)
    return pl.pallas_call(
        functools.partial(_ffn_kernel, tm=tm, tf=tf, seq_rows=seq_rows),
        grid=(m // tm, nj),
        in_specs=[pl.BlockSpec((tm, D_MODEL), lambda i, j: (i, 0)),
                  pl.BlockSpec((1, D_MODEL), lambda i, j: (0, 0)),
                  pl.BlockSpec((D_MODEL, tf), lambda i, j: (0, j)),
                  pl.BlockSpec((D_MODEL, tf), lambda i, j: (0, nj + j)),
                  pl.BlockSpec((CONV_W, tf), lambda i, j: (0, j)),
                  pl.BlockSpec((1, tf), lambda i, j: (0, j)),
                  pl.BlockSpec((tf, D_MODEL), lambda i, j: (j, 0)),
                  p1_spec, p2_spec],
        out_specs=[pl.BlockSpec((tm, D_MODEL), lambda i, j: (i, 0)),
                   tail_spec],
        out_shape=[jax.ShapeDtypeStruct((m, D_MODEL), F32), tail_shape],
        scratch_shapes=[pltpu.VMEM((tm, D_MODEL), BF16), pltpu.VMEM((tm, D_MODEL), F32),
                        pltpu.VMEM((CONV_W - 1, D_FF), F32)],
        compiler_params=_cparams(("arbitrary", "arbitrary")),
        name="conv_ffn",
    )(h, g_ffn, w_up, w_up, conv_w, conv_b, w_down, p1, p2)


def _ple_kernel(h_ref, p_ref, wg_ref, wp_ref, gp_ref, gf_ref, o_ref, *, final_norm):
    h = h_ref[...]
    gate = _sigmoid(_dot(h.astype(BF16), wg_ref[...]))
    pe = _rms(_dot(p_ref[...].astype(BF16), wp_ref[...]), gp_ref[...])
    h = h + gate * pe
    o_ref[...] = _rms(h, gf_ref[...]) if final_norm else h


def _ple(h, p, w_gate, w_proj, g_ple, g_final, tm, final_norm):
    m = h.shape[0]
    return pl.pallas_call(
        functools.partial(_ple_kernel, final_norm=final_norm),
        grid=(m // tm,),
        in_specs=[pl.BlockSpec((tm, D_MODEL), lambda i: (i, 0)),
                  pl.BlockSpec((tm, PLE_DIM), lambda i: (i, 0)),
                  pl.BlockSpec((D_MODEL, D_MODEL), lambda i: (0, 0)),
                  pl.BlockSpec((PLE_DIM, D_MODEL), lambda i: (0, 0)),
                  pl.BlockSpec((1, D_MODEL), lambda i: (0, 0)),
                  pl.BlockSpec((1, D_MODEL), lambda i: (0, 0))],
        out_specs=pl.BlockSpec((tm, D_MODEL), lambda i: (i, 0)),
        out_shape=jax.ShapeDtypeStruct((m, D_MODEL), F32),
        compiler_params=_cparams(("parallel",)),
        name="ple_norm",
    )(h, p, w_gate, w_proj, g_ple, g_final)


CMP_PAGES = 16
CHUNKS_PER_PAGE = PAGE_SIZE // CMP_STRIDE


def _cmp_sample_kernel(pt_ref, *refs):
    page_refs = refs[:CMP_PAGES]
    w1_ref, w2_ref, pe_ref, o_ref, carry_ref, x_ref = refs[CMP_PAGES:]
    grp = pl.program_id(1)
    nck = CMP_PAGES * CHUNKS_PER_PAGE

    @pl.when(grp == 0)
    def _():
        carry_ref[...] = jnp.zeros_like(carry_ref)

    for k, r in enumerate(page_refs):
        for n in range(NSA_KV_HEADS):
            x_ref[n, k * PAGE_SIZE:(k + 1) * PAGE_SIZE, :] = r[:, n * KV_SLAB:(n + 1) * KV_SLAB]
    acc = jnp.zeros((NSA_KV_HEADS * nck, 4 * CMP_HIDDEN), F32)
    for s in range(CMP_STRIDE):
        xs = jnp.concatenate([x_ref[n, pl.ds(s, nck, stride=CMP_STRIDE), :] for n in range(NSA_KV_HEADS)], axis=0)
        acc = acc + _dot(xs.astype(BF16), w1_ref[s])
    a = acc[:, :2 * CMP_HIDDEN]
    b = acc[:, 2 * CMP_HIDDEN:]
    a_prev = pltpu.roll(a, 1, axis=0)
    row = lax.broadcasted_iota(jnp.int32, (NSA_KV_HEADS * nck, 1), 0)
    for n in range(NSA_KV_HEADS):
        a_prev = jnp.where(row == n * nck, carry_ref[n:n + 1, :], a_prev)
        carry_ref[n:n + 1, :] = a[(n + 1) * nck - 1:(n + 1) * nck]
    hid = a_prev + b + _cmp_pe_hidden(pe_ref, w1_ref)
    res = _dot(_gelu_tanh(hid).astype(BF16), w2_ref[...])
    for n in range(NSA_KV_HEADS):
        o_ref[n] = res[n * nck:(n + 1) * nck]


def _cmp_sample(cache_c, page_table_flat, w1, w2, pe, b, n_pages):
    n_grp = n_pages // CMP_PAGES
    nck = CMP_PAGES * CHUNKS_PER_PAGE

    def page_spec(k):
        return pl.BlockSpec((None, PAGE_SIZE, KV_WIDTH),
                            lambda i, g, pt: (pt[i * n_pages + g * CMP_PAGES + k], 0, 0))

    grid_spec = pltpu.PrefetchScalarGridSpec(
        num_scalar_prefetch=1,
        grid=(b, n_grp),
        in_specs=[page_spec(k) for k in range(CMP_PAGES)]
        + [pl.BlockSpec(w1.shape, lambda i, g, pt: (0, 0, 0)),
           pl.BlockSpec(w2.shape, lambda i, g, pt: (0, 0)),
           pl.BlockSpec(pe.shape, lambda i, g, pt: (0, 0))],
        out_specs=pl.BlockSpec((None, NSA_KV_HEADS, nck, KV_SLAB), lambda i, g, pt: (i, 0, g, 0)),
        scratch_shapes=[pltpu.VMEM((SUBLANES, 2 * CMP_HIDDEN), F32),
                        pltpu.VMEM((NSA_KV_HEADS, CMP_PAGES * PAGE_SIZE, KV_SLAB), F32)],
    )
    return pl.pallas_call(
        _cmp_sample_kernel,
        grid_spec=grid_spec,
        out_shape=jax.ShapeDtypeStruct((b, NSA_KV_HEADS, n_pages * CHUNKS_PER_PAGE, KV_SLAB), F32),
        compiler_params=_cparams(("parallel", "arbitrary")),
        name="cmp_sample",
    )(page_table_flat, *([cache_c] * CMP_PAGES), w1, w2, pe)


def _sel_sample_kernel(slopes_ref, q_ref, kcv_ref, oc_ref, idx_ref, *, past_len, n_sel_pad, t_valid):
    n = pl.program_id(1)
    tp = SUBLANES
    rows = NSA_GROUP * tp
    n_rows_c = past_len // CMP_STRIDE
    nb_past = past_len // SEL_BLOCK
    qf = q_ref[...] * SCALE
    q = jnp.concatenate([qf[:, g * HEAD_DIM:(g + 1) * HEAD_DIM] for g in range(NSA_GROUP)], axis=0).astype(BF16)
    tok = lax.broadcasted_iota(jnp.int32, (rows, 1), 0) % tp
    qpos = past_len + tok
    slope = _slope_col(slopes_ref, n, tp, rows)
    kcv = kcv_ref[...]
    kc = kcv[:, :HEAD_DIM].astype(BF16)
    vc = kcv[:, HEAD_DIM:].astype(BF16)
    s = _dot_nt(q, kc)
    cp = lax.broadcasted_iota(jnp.int32, (1, n_rows_c), 1)
    end = (cp - 1) * CMP_STRIDE + (CMP_BLOCK - 1)
    dist = (qpos - end).astype(F32)
    valid = (cp >= 1) & (dist >= 0)
    s = jnp.where(valid, s - slope * dist, NEG_INF)
    e = jnp.exp(s - jnp.max(s, axis=-1, keepdims=True))
    p = jnp.where(valid, e / jnp.sum(e, axis=-1, keepdims=True), 0.0)
    o_c = _dot(p.astype(BF16), vc)
    oc_ref[...] = jnp.concatenate([o_c[g * tp:(g + 1) * tp] for g in range(NSA_GROUP)], axis=1)
    p_grp = p[0:tp]
    for g in range(1, NSA_GROUP):
        p_grp = p_grp + p[g * tp:(g + 1) * tp]
    imp = jnp.dot(p_grp, _overlap_matrix(n_rows_c, n_sel_pad, 1), preferred_element_type=F32, precision=HIGHEST)
    n_tail = -(-t_valid // SEL_BLOCK)
    n_sel = nb_past + n_tail
    j = lax.broadcasted_iota(jnp.int32, (tp, n_sel_pad), 1)
    cur = qpos[0:tp] // SEL_BLOCK
    forced = (j == 0) | (j == cur) | (j == cur - 1)
    imp = jnp.where(forced, FORCE_SCORE, jnp.where(j > cur, -FORCE_SCORE, imp))
    imp = jnp.where(j < n_sel, imp, REMOVED_SCORE)
    _, picks = _top_blocks(imp, min(N_SELECT, n_sel))
    kcol = lax.broadcasted_iota(jnp.int32, (tp, N_SELECT), 1)
    idx = jnp.zeros((tp, N_SELECT), jnp.int32)
    for kk, pk in enumerate(picks):
        idx = jnp.where(kcol == kk, pk, idx)
    idx_ref[...] = idx


def _sel_sample(proj8, kcv, slopes, b, past_len, t_valid):
    n_rows_c = past_len // CMP_STRIDE
    nb_past = past_len // SEL_BLOCK
    n_sel_pad = -(-(nb_past + 1) // LANES) * LANES
    grid_spec = pltpu.PrefetchScalarGridSpec(
        num_scalar_prefetch=1,
        grid=(b, NSA_KV_HEADS),
        in_specs=[pl.BlockSpec((SUBLANES, Q_SLAB), lambda i, n, sl: (i, COL_Q // Q_SLAB + n)),
                  pl.BlockSpec((None, None, n_rows_c, KV_SLAB), lambda i, n, sl: (i, n, 0, 0))],
        out_specs=[pl.BlockSpec((None, None, SUBLANES, Q_SLAB), lambda i, n, sl: (i, n, 0, 0)),
                   pl.BlockSpec((None, None, SUBLANES, N_SELECT), lambda i, n, sl: (i, n, 0, 0))],
    )
    return pl.pallas_call(
        functools.partial(_sel_sample_kernel, past_len=past_len, n_sel_pad=n_sel_pad, t_valid=t_valid),
        grid_spec=grid_spec,
        out_shape=[jax.ShapeDtypeStruct((b, NSA_KV_HEADS, SUBLANES, Q_SLAB), F32),
                   jax.ShapeDtypeStruct((b, NSA_KV_HEADS, SUBLANES, N_SELECT), jnp.int32)],
        compiler_params=_cparams(("parallel", "parallel")),
        name="sel_sample",
    )(slopes, proj8, kcv)


def _nsa_sample_kernel(idx_ref, pt_ref, slopes_ref, *refs, past_len, n_pages):
    blk_refs = refs[:N_SELECT]
    q_ref, tail_ref, wnew_ref, wcache_ref, oc_ref, misc_ref, o_ref = refs[N_SELECT:]
    i = pl.program_id(0)
    n = pl.program_id(1)
    t = pl.program_id(2)
    nb_past = past_len // SEL_BLOCK
    g8 = SUBLANES
    qrow = q_ref[pl.ds(t, 1), :] * SCALE
    q = jnp.concatenate([qrow[:, g * HEAD_DIM:(g + 1) * HEAD_DIM] for g in range(NSA_GROUP)]
                        + [jnp.zeros((g8 - NSA_GROUP, HEAD_DIM), F32)], axis=0).astype(BF16)
    slope = _slope_col(slopes_ref, n, 1, g8)
    qpos = past_len + t
    base = ((i * NSA_KV_HEADS + n) * SUBLANES + t) * N_SELECT

    scores, oks, vals = [], [], []
    tail_count = jnp.zeros((), jnp.int32)
    lane = lax.broadcasted_iota(jnp.int32, (1, SEL_BLOCK), 1)
    for kk in range(N_SELECT):
        jb = idx_ref[base + kk]
        tail_count = tail_count + (jb >= nb_past).astype(jnp.int32)
        kv = blk_refs[kk][...]
        kpos = jb * SEL_BLOCK + lane
        dist = (qpos - kpos).astype(F32)
        ok = (dist >= 0) & (jnp.full((1, SEL_BLOCK), jb, jnp.int32) < nb_past)
        sc = _dot_nt(q, kv[:, :HEAD_DIM].astype(BF16)) - slope * dist
        scores.append(jnp.where(ok, sc, NEG_INF))
        oks.append(ok)
        vals.append(kv[:, HEAD_DIM:].astype(BF16))
    tl = tail_ref[...]
    lane8 = lax.broadcasted_iota(jnp.int32, (1, SUBLANES), 1)
    dist = (t - lane8).astype(F32)
    ok = (dist >= 0) & (jnp.full((1, SUBLANES), tail_count, jnp.int32) > 0)
    sc = _dot_nt(q, tl[:, :HEAD_DIM].astype(BF16)) - slope * dist
    scores.append(jnp.where(ok, sc, NEG_INF))
    oks.append(ok)
    vals.append(tl[:, HEAD_DIM:].astype(BF16))

    def softmax_av(scores, oks, vals):
        m = scores[0].max(axis=-1, keepdims=True)
        for sc in scores[1:]:
            m = jnp.maximum(m, sc.max(axis=-1, keepdims=True))
        l = jnp.zeros((g8, 1), F32)
        acc = jnp.zeros((g8, HEAD_DIM), F32)
        for sc, ok, v in zip(scores, oks, vals):
            pr = jnp.where(ok, jnp.exp(sc - m), 0.0)
            l = l + pr.sum(axis=-1, keepdims=True)
            acc = acc + _dot(pr.astype(BF16), v)
        return acc / l

    o_s = softmax_av(scores, oks, vals)

    wc = wcache_ref[...]
    buf_len = wc.shape[0]
    lane_w = lax.broadcasted_iota(jnp.int32, (1, buf_len), 1)
    dist_c = (qpos - (past_len - buf_len + lane_w)).astype(F32)
    ok_c = (dist_c >= 0) & (dist_c < WINDOW)
    sc_c = jnp.where(ok_c, _dot_nt(q, wc[:, :HEAD_DIM].astype(BF16)) - slope * dist_c, NEG_INF)
    wn = wnew_ref[...]
    dist_n = (t - lane8).astype(F32)
    ok_n = (dist_n >= 0) & (dist_n < WINDOW)
    sc_n = jnp.where(ok_n, _dot_nt(q, wn[:, :HEAD_DIM].astype(BF16)) - slope * dist_n, NEG_INF)
    o_w = softmax_av([sc_c, sc_n], [ok_c, ok_n], [wc[:, HEAD_DIM:].astype(BF16), wn[:, HEAD_DIM:].astype(BF16)])

    oc_row = oc_ref[pl.ds(t, 1), :]
    gates = _pick_head_gates(misc_ref[pl.ds(t, 1), 0:GATE_W], n)
    outs = []
    for g in range(NSA_GROUP):
        outs.append(gates[:, 3 * g:3 * g + 1] * oc_row[:, g * HEAD_DIM:(g + 1) * HEAD_DIM]
                    + gates[:, 3 * g + 1:3 * g + 2] * o_s[g:g + 1]
                    + gates[:, 3 * g + 2:3 * g + 3] * o_w[g:g + 1])
    o_ref[...] = jnp.concatenate(outs, axis=1)


def _nsa_sample(proj8, cache_s, cache_w, o_c, idx_flat, page_table_flat, slopes, b, t_valid, past_len):
    n_pages = past_len // PAGE_SIZE
    nb_past = past_len // SEL_BLOCK
    per_page = PAGE_SIZE // SEL_BLOCK
    buf_len = cache_w.shape[1]

    def blk_spec(kk):
        def imap(i, n, t, idx, pt, sl):
            jb = jnp.minimum(idx[((i * NSA_KV_HEADS + n) * SUBLANES + t) * N_SELECT + kk], nb_past - 1)
            return (pt[i * n_pages + jb // per_page] * per_page + jb % per_page, 0, n)
        return pl.BlockSpec((None, SEL_BLOCK, KV_SLAB), imap)

    grid_spec = pltpu.PrefetchScalarGridSpec(
        num_scalar_prefetch=3,
        grid=(b, NSA_KV_HEADS, t_valid),
        in_specs=[blk_spec(kk) for kk in range(N_SELECT)]
        + [pl.BlockSpec((SUBLANES, Q_SLAB), lambda i, n, t, *_: (i, COL_Q // Q_SLAB + n)),
           pl.BlockSpec((SUBLANES, KV_SLAB), lambda i, n, t, *_: (i, COL_KVS // KV_SLAB + n)),
           pl.BlockSpec((SUBLANES, KV_SLAB), lambda i, n, t, *_: (i, COL_KVW // KV_SLAB + n)),
           pl.BlockSpec((None, buf_len, KV_SLAB), lambda i, n, t, *_: (i, 0, n)),
           pl.BlockSpec((None, None, SUBLANES, Q_SLAB), lambda i, n, t, *_: (i, n, 0, 0)),
           pl.BlockSpec((SUBLANES, LANES), lambda i, n, t, *_: (i, COL_MISC // LANES))],
        out_specs=pl.BlockSpec((None, 1, Q_SLAB), lambda i, n, t, *_: (i * SUBLANES + t, 0, n)),
    )
    return pl.pallas_call(
        functools.partial(_nsa_sample_kernel, past_len=past_len, n_pages=n_pages),
        grid_spec=grid_spec,
        out_shape=jax.ShapeDtypeStruct((b * SUBLANES, 1, NSA_WIDTH), F32),
        compiler_params=_cparams(("parallel", "parallel", "arbitrary")),
        name="nsa_sample",
    )(idx_flat, page_table_flat, slopes, *([cache_s] * N_SELECT), proj8, proj8, proj8, cache_w, o_c, proj8)


def _alibi_slopes():
    h = np.arange(1, NSA_HEADS + 1, dtype=np.float32)
    return jnp.asarray(2.0 ** (-8.0 * h / NSA_HEADS), dtype=F32)


def _permute_w_in(w_in):
    offs = np.cumsum([0, NSA_WIDTH, KV_WIDTH, KV_WIDTH, KV_WIDTH, GATE_W, GLA_HEADS * GLA_DK, GLA_HEADS * GLA_DK,
                      GLA_WIDTH, GLA_WIDTH, GLA_RANK])
    piece = [w_in[:, offs[k]:offs[k + 1]] for k in range(10)]
    q, kvc, kvs, kvw, gl, gq, gk, gv, gr, ga = piece
    pad = jnp.zeros((w_in.shape[0], LANES - GATE_W - GLA_RANK), w_in.dtype)
    return jnp.concatenate([q, kvc, kvs, kvw, gq, gk, gv, gr, gl, ga, pad], axis=1).astype(BF16)


def _row_tile(m, pref):
    return pref if m % pref == 0 else m


def _dense_tail(h_in, o_nsa, o_gla, p_emb, wts, conv_p1, conv_p2, seq_rows, last_layer, tm):
    (g_nsa, w_out, g_ffn, w_up, conv_w, conv_b, w_down, w_ple_proj, g_ple, w_ple_gate, g_final) = wts
    h = _out_proj(o_nsa, o_gla, h_in, g_nsa, w_out, tm, D_MODEL // 2)
    h, tails = _ffn(h, g_ffn, w_up, conv_w, conv_b, w_down, conv_p1, conv_p2, tm, 512, seq_rows)
    y = _ple(h, p_emb, w_ple_gate, w_ple_proj, g_ple, g_final, tm, last_layer)
    return y, tails


def kernel(x_prompt, x_sample, p_prompt, p_sample, cache_cmp_kv, cache_sel_kv, cache_win_kv, state_gla, state_ffn_conv, page_table, g_attn, w_in, w_cmp_k1, w_cmp_k2, pe_cmp_k, w_cmp_v1, w_cmp_v2, pe_cmp_v, w_gla_a2, b_gla_a, g_nsa_out, g_gla_out, w_out, g_ffn, w_up, conv_w, conv_b, w_down, w_ple_proj, g_ple, w_ple_gate, g_final):
    depth = w_in.shape[0]
    bp, tp, _ = x_prompt.shape
    bs, ts, _ = x_sample.shape
    n_pages = page_table.shape[1]
    past_len = n_pages * PAGE_SIZE
    n_pool = cache_cmp_kv.shape[1]
    assert tp % Q_BLOCK == 0 and tp >= WINDOW and ts <= SUBLANES and ts <= SEL_BLOCK and ts >= CONV_W - 1
    assert n_pages % CMP_PAGES == 0 and tp // SEL_BLOCK >= N_SELECT
    slopes = _alibi_slopes()
    pt_flat = page_table.reshape(-1).astype(jnp.int32)
    kv_shape = (NSA_KV_HEADS, 2, HEAD_DIM)

    hp = x_prompt.reshape(bp * tp, D_MODEL)
    hs = x_sample.reshape(bs * ts, D_MODEL)
    new_p = [[] for _ in range(5)]
    new_s = [[] for _ in range(5)]
    tm_p = _row_tile(bp * tp, 512)
    tm_s = bs * ts
    for i in range(depth):
        last = i == depth - 1
        w_in_p = _permute_w_in(w_in[i])
        g_a = g_attn[i].reshape(1, D_MODEL)
        w1c, w2c, pec = _cmp_weights(w_cmp_k1[i], w_cmp_k2[i], pe_cmp_k[i], w_cmp_v1[i], w_cmp_v2[i], pe_cmp_v[i])
        b_a = b_gla_a[i].reshape(1, -1)
        g_go = g_gla_out[i].reshape(1, GLA_DV)
        wts = (g_nsa_out[i].reshape(1, -1), w_out[i].astype(BF16), g_ffn[i].reshape(1, -1), w_up[i].astype(BF16),
               conv_w[i], conv_b[i].reshape(1, -1), w_down[i].astype(BF16), w_ple_proj[i].astype(BF16),
               g_ple[i].reshape(1, -1), w_ple_gate[i].astype(BF16), g_final.reshape(1, -1))

        proj = _norm_matmul(hp, g_a, w_in_p, tm_p, D_IN_PAD // 5)
        kcv = _cmp_prompt(proj, bp, tp, w1c, w2c, pec)
        o_nsa = _nsa_prompt(proj, kcv, slopes, bp, tp)
        s0 = jnp.zeros((bp, GLA_HEADS, GLA_DK, GLA_DV), F32)
        o_gla, s_new = _gla(proj, w_gla_a2[i], b_a, g_go, s0, bp, tp, 64, 16, 64)
        zbuf = jnp.zeros((bp, CONV_W - 1, D_FF), F32)
        hp, tails = _dense_tail(hp, o_nsa, o_gla, p_prompt[i].reshape(bp * tp, PLE_DIM), wts, zbuf, zbuf, tp, last, tm_p)
        proj3 = proj.reshape(bp, tp, D_IN_PAD)
        new_p[0].append(proj3[:, :, COL_KVC:COL_KVC + KV_WIDTH].reshape((bp, tp) + kv_shape))
        new_p[1].append(proj3[:, :, COL_KVS:COL_KVS + KV_WIDTH].reshape((bp, tp) + kv_shape))
        new_p[2].append(proj3[:, tp - WINDOW:, COL_KVW:COL_KVW + KV_WIDTH].reshape((bp, WINDOW) + kv_shape))
        new_p[3].append(s_new)
        tiles_per_seq = tp // tm_p
        new_p[4].append(tails.reshape(bp, tiles_per_seq, SUBLANES, D_FF)[:, -1, SUBLANES - (CONV_W - 1):, :])

        proj_s = _norm_matmul(hs, g_a, w_in_p, tm_s, D_IN_PAD // 5)
        proj8 = jnp.pad(proj_s.reshape(bs, ts, D_IN_PAD), ((0, 0), (0, SUBLANES - ts), (0, 0))).reshape(bs * SUBLANES, D_IN_PAD)
        cache_c = cache_cmp_kv[i].reshape(n_pool, PAGE_SIZE, KV_WIDTH)
        cache_s = cache_sel_kv[i].reshape(n_pool * (PAGE_SIZE // SEL_BLOCK), SEL_BLOCK, KV_WIDTH)
        cache_w = cache_win_kv[i].reshape(bs, -1, KV_WIDTH)
        kcv_s = _cmp_sample(cache_c, pt_flat, w1c, w2c, pec, bs, n_pages)
        o_c, idx = _sel_sample(proj8, kcv_s, slopes, bs, past_len, ts)
        o_nsa_s = _nsa_sample(proj8, cache_s, cache_w, o_c, idx.reshape(-1), pt_flat, slopes, bs, ts, past_len)
        o_nsa_s = o_nsa_s.reshape(bs, SUBLANES, NSA_WIDTH)[:, :ts].reshape(bs * ts, NSA_WIDTH)
        o_gla_s, s_new_s = _gla(proj8, w_gla_a2[i], b_a, g_go, state_gla[i].astype(F32), bs, SUBLANES, SUBLANES, SUBLANES, ts)
        o_gla_s = o_gla_s.reshape(bs, SUBLANES, GLA_WIDTH)[:, :ts].reshape(bs * ts, GLA_WIDTH)
        buf = state_ffn_conv[i]
        zrow = jnp.zeros((bs, ts - 1, D_FF), F32)
        p1 = jnp.concatenate([buf[:, 1:2], zrow], axis=1).reshape(bs * ts, D_FF)
        p2 = jnp.concatenate([buf, jnp.zeros((bs, ts - 2, D_FF), F32)], axis=1).reshape(bs * ts, D_FF)
        hs, tails_s = _dense_tail(hs, o_nsa_s, o_gla_s, p_sample[i].reshape(bs * ts, PLE_DIM), wts, p1, p2, ts, last, tm_s)
        ps3 = proj_s.reshape(bs, ts, D_IN_PAD)
        new_s[0].append(ps3[:, :, COL_KVC:COL_KVC + KV_WIDTH].reshape((bs, ts) + kv_shape))
        new_s[1].append(ps3[:, :, COL_KVS:COL_KVS + KV_WIDTH].reshape((bs, ts) + kv_shape))
        new_s[2].append(ps3[:, :, COL_KVW:COL_KVW + KV_WIDTH].reshape((bs, ts) + kv_shape))
        new_s[3].append(s_new_s)
        new_s[4].append(tails_s.reshape(bs, ts, D_FF)[:, ts - (CONV_W - 1):, :])

    y_prompt = hp.reshape(bp, tp, D_MODEL)
    y_sample = hs.reshape(bs, ts, D_MODEL)
    cmp_p, sel_p, win_p, gla_p, conv_p = [jnp.stack(l) for l in new_p]
    cmp_s, sel_s, win_s, gla_s, conv_s = [jnp.stack(l) for l in new_s]
    return (y_prompt, y_sample, cmp_p, sel_p, win_p, gla_p, conv_p, cmp_s, sel_s, win_s, gla_s, conv_s)
```

```python
import functools

import numpy as np
import jax
import jax.numpy as jnp
from jax import lax
from jax.experimental import pallas as pl
from jax.experimental.pallas import tpu as pltpu

F32 = jnp.float32
BF16 = jnp.bfloat16
HIGHEST = lax.Precision.HIGHEST

D_MODEL = 2048
PAGE_SIZE = 128
NSA_HEADS = 16
NSA_KV_HEADS = 4
NSA_GROUP = NSA_HEADS // NSA_KV_HEADS
HEAD_DIM = 64
CMP_BLOCK = 32
CMP_STRIDE = 16
CMP_HIDDEN = 2 * HEAD_DIM
SEL_BLOCK = 64
N_SELECT = 16
WINDOW = 512
Q_BLOCK = 128
GLA_HEADS = 4
GLA_DK = 128
GLA_DV = 256
GLA_RANK = 16
GLA_TAU = 16.0
D_FF = 5632
CONV_W = 3
PLE_DIM = 256
EPS = 1e-6
NEG_INF = -1e30
FORCE_SCORE = 1e9
REMOVED_SCORE = -3e38
SCALE = HEAD_DIM ** -0.5

NSA_WIDTH = NSA_HEADS * HEAD_DIM
GLA_WIDTH = GLA_HEADS * GLA_DV
KV_WIDTH = 2 * NSA_KV_HEADS * HEAD_DIM
KV_SLAB = 2 * HEAD_DIM
Q_SLAB = NSA_GROUP * HEAD_DIM
GATE_W = 3 * NSA_HEADS

LANES = 128
SUBLANES = 8
VMEM_LIMIT = 56 * 1024 * 1024

COL_Q = 0
COL_KVC = COL_Q + NSA_WIDTH
COL_KVS = COL_KVC + KV_WIDTH
COL_KVW = COL_KVS + KV_WIDTH
COL_GQ = COL_KVW + KV_WIDTH
COL_GK = COL_GQ + GLA_HEADS * GLA_DK
COL_GV = COL_GK + GLA_HEADS * GLA_DK
COL_GR = COL_GV + GLA_WIDTH
COL_MISC = COL_GR + GLA_WIDTH
D_IN_PAD = COL_MISC + LANES
MISC_GA = GATE_W


def _cparams(sem):
    return pltpu.CompilerParams(dimension_semantics=sem, vmem_limit_bytes=VMEM_LIMIT)


def _rms(x, g):
    return x * lax.rsqrt(jnp.mean(x * x, axis=-1, keepdims=True) + EPS) * g


def _gelu_tanh(x):
    return 0.5 * x * (1.0 + jnp.tanh(np.float32(np.sqrt(2.0 / np.pi)) * (x + 0.044715 * (x * x * x))))


def _sigmoid(x):
    return 1.0 / (1.0 + jnp.exp(-x))


def _dot(a, b):
    return jnp.dot(a, b, preferred_element_type=F32)


def _dot_nt(a, b):
    return lax.dot_general(a, b, (((1,), (1,)), ((), ())), preferred_element_type=F32)


def _dot_tn(a, b):
    return lax.dot_general(a, b, (((0,), (0,)), ((), ())), preferred_element_type=F32)


def _norm_matmul_kernel(x_ref, g_ref, w_ref, o_ref, xn_ref):
    @pl.when(pl.program_id(1) == 0)
    def _():
        xn_ref[...] = _rms(x_ref[...], g_ref[...]).astype(BF16)

    o_ref[...] = _dot(xn_ref[...], w_ref[...])


def _norm_matmul(x, g, w, tm, tn):
    m, d = x.shape
    n = w.shape[1]
    return pl.pallas_call(
        _norm_matmul_kernel,
        grid=(m // tm, n // tn),
        in_specs=[pl.BlockSpec((tm, d), lambda i, j: (i, 0)),
                  pl.BlockSpec((1, d), lambda i, j: (0, 0)),
                  pl.BlockSpec((d, tn), lambda i, j: (0, j))],
        out_specs=pl.BlockSpec((tm, tn), lambda i, j: (i, j)),
        out_shape=jax.ShapeDtypeStruct((m, n), F32),
        scratch_shapes=[pltpu.VMEM((tm, d), BF16)],
        compiler_params=_cparams(("parallel", "arbitrary")),
        name="in_proj",
    )(x, g, w)


def _cmp_pe_hidden(pe_ref, w1_ref):
    span = CMP_BLOCK // CMP_STRIDE
    acc = jnp.zeros((SUBLANES, 2 * CMP_HIDDEN), F32)
    for j in range(span):
        for s in range(CMP_STRIDE):
            row = jnp.broadcast_to(pe_ref[pl.ds(j * CMP_STRIDE + s, 1), :], (SUBLANES, KV_SLAB)).astype(BF16)
            acc = acc + _dot(row, w1_ref[s][:, j * 2 * CMP_HIDDEN:(j + 1) * 2 * CMP_HIDDEN])
    return acc[0:1, :]


def _cmp_prompt_kernel(x_ref, w1_ref, w2_ref, pe_ref, o_ref, *, n16):
    acc = jnp.zeros((n16, 4 * CMP_HIDDEN), F32)
    for s in range(CMP_STRIDE):
        xs = x_ref[pl.ds(s, n16, stride=CMP_STRIDE), :].astype(BF16)
        acc = acc + _dot(xs, w1_ref[s])
    a = acc[:, :2 * CMP_HIDDEN]
    b_next = pltpu.roll(acc[:, 2 * CMP_HIDDEN:], n16 - 1, axis=0)
    hid = a + b_next + _cmp_pe_hidden(pe_ref, w1_ref)
    o_ref[...] = _dot(_gelu_tanh(hid).astype(BF16), w2_ref[...])


def _cmp_weights(w_k1, w_k2, pe_k, w_v1, w_v2, pe_v):
    z = jnp.zeros((CMP_BLOCK, HEAD_DIM, CMP_HIDDEN), F32)
    wl = jnp.concatenate([jnp.concatenate([w_k1, z], axis=2), jnp.concatenate([z, w_v1], axis=2)], axis=1)
    w1 = jnp.concatenate([wl[:CMP_STRIDE], wl[CMP_STRIDE:]], axis=2).astype(BF16)
    z2 = jnp.zeros((CMP_HIDDEN, HEAD_DIM), F32)
    w2 = jnp.concatenate([jnp.concatenate([w_k2, z2], axis=1), jnp.concatenate([z2, w_v2], axis=1)], axis=0)
    pe = jnp.concatenate([pe_k, pe_v], axis=1)
    return w1, w2.astype(BF16), pe


def _cmp_prompt(proj, b, t, w1, w2, pe):
    n16 = t // CMP_STRIDE
    col0 = COL_KVC // KV_SLAB
    return pl.pallas_call(
        functools.partial(_cmp_prompt_kernel, n16=n16),
        grid=(b, NSA_KV_HEADS),
        in_specs=[pl.BlockSpec((t, KV_SLAB), lambda i, n: (i, col0 + n)),
                  pl.BlockSpec(w1.shape, lambda i, n: (0, 0, 0)),
                  pl.BlockSpec(w2.shape, lambda i, n: (0, 0)),
                  pl.BlockSpec(pe.shape, lambda i, n: (0, 0))],
        out_specs=pl.BlockSpec((None, None, n16, KV_SLAB), lambda i, n: (i, n, 0, 0)),
        out_shape=jax.ShapeDtypeStruct((b, NSA_KV_HEADS, n16, KV_SLAB), F32),
        compiler_params=_cparams(("parallel", "parallel")),
        name="cmp_prompt",
    )(proj, w1, w2, pe)


def _overlap_matrix(n_cmp_rows, n_sel_cols, shift):
    c = lax.broadcasted_iota(jnp.int32, (n_cmp_rows, n_sel_cols), 0) - shift
    j = lax.broadcasted_iota(jnp.int32, (n_cmp_rows, n_sel_cols), 1)
    per_sel = SEL_BLOCK // CMP_STRIDE
    ov = jnp.zeros((n_cmp_rows, n_sel_cols), F32)
    for n in range(CMP_BLOCK // CMP_STRIDE):
        ov = ov + ((c + n >= per_sel * j) & (c + n < per_sel * (j + 1))).astype(F32)
    return jnp.where(c >= 0, ov, 0.0)


def _overlap_matrix_t(n_sel_rows, n_cmp_cols):
    j = lax.broadcasted_iota(jnp.int32, (n_sel_rows, n_cmp_cols), 0)
    c = lax.broadcasted_iota(jnp.int32, (n_sel_rows, n_cmp_cols), 1)
    per_sel = SEL_BLOCK // CMP_STRIDE
    ov = jnp.zeros((n_sel_rows, n_cmp_cols), F32)
    for n in range(CMP_BLOCK // CMP_STRIDE):
        ov = ov + jnp.where((c + n >= per_sel * j) & (c + n < per_sel * (j + 1)), 1.0, 0.0)
    return ov


def _top_blocks(imp, n_pick):
    rows, nsel = imp.shape
    j = lax.broadcasted_iota(jnp.int32, (rows, nsel), 1).astype(F32)
    mask = jnp.zeros((rows, nsel), F32)
    picks = []
    for _ in range(n_pick):
        m = jnp.max(imp, axis=-1, keepdims=True)
        jmin = jnp.min(jnp.where(imp == m, j, float(nsel)), axis=-1, keepdims=True)
        hit = j == jmin
        mask = jnp.where(hit, 1.0, mask)
        imp = jnp.where(hit, REMOVED_SCORE, imp)
        picks.append(jmin.astype(jnp.int32))
    return mask, picks


def _slope_col(slopes_ref, n, rows_per_head, rows):
    g = lax.broadcasted_iota(jnp.int32, (rows, 1), 0) // rows_per_head
    col = jnp.zeros((rows, 1), F32)
    for gg in range(NSA_GROUP):
        col = jnp.where(g == gg, slopes_ref[n * NSA_GROUP + gg], col)
    return col


def _pick_head_gates(gl, n):
    out = jnp.zeros((gl.shape[0], 3 * NSA_GROUP), F32)
    for nn in range(NSA_KV_HEADS):
        out = jnp.where(n == nn, gl[:, nn * 3 * NSA_GROUP:(nn + 1) * 3 * NSA_GROUP], out)
    return _sigmoid(out)


MASK_BIAS = -131072.0
M_FLOOR = -65536.0
POS_HI, POS_LO = HEAD_DIM, HEAD_DIM + 3
KEY_TILE = 2 * Q_BLOCK


def _slope_features():
    h = np.arange(1, NSA_HEADS + 1, dtype=np.float32)
    slopes = (2.0 ** (-8.0 * h / NSA_HEADS)).astype(np.float32)
    tab = np.zeros((NSA_HEADS, LANES), np.float32)
    rest = slopes
    for c in range(3):
        piece = rest.astype(BF16).astype(np.float32)
        tab[:, POS_HI + c] = piece
        tab[:, POS_LO + c] = piece
        rest = rest - piece
    assert not rest.any()
    return jnp.asarray(tab)


def _kv_prep_kernel(kvs_ref, kvw_ref, ks_ref, vs_ref, kw_ref, vw_ref, *, t):
    lane = lax.broadcasted_iota(jnp.int32, (t, LANES), 1)
    pos = lax.broadcasted_iota(jnp.int32, (t, LANES), 0)
    hi = ((pos // SEL_BLOCK) * SEL_BLOCK).astype(F32)
    lo = (pos % SEL_BLOCK).astype(F32)
    feat = jnp.where((lane >= POS_HI) & (lane < POS_HI + 3), hi,
                     jnp.where((lane >= POS_LO) & (lane < POS_LO + 3), lo, 0.0))
    ones_col = jnp.where(lane == 0, 1.0, 0.0)
    for src, k_out, v_out in ((kvs_ref, ks_ref, vs_ref), (kvw_ref, kw_ref, vw_ref)):
        x = src[...]
        k_out[:, 0:LANES] = jnp.where(lane < HEAD_DIM, x, feat).astype(BF16)
        v_out[...] = jnp.where(lane >= HEAD_DIM, x, ones_col).astype(BF16)
    ks_ref[:, LANES:2 * LANES] = jnp.where(lane == pos // SEL_BLOCK, 1.0, 0.0).astype(BF16)


def _kv_prep(proj, b, t):
    spec_in = lambda col: pl.BlockSpec((t, KV_SLAB), lambda i, n: (i, col // KV_SLAB + n))
    spec_out = lambda w: pl.BlockSpec((None, None, t, w), lambda i, n: (i, n, 0, 0))
    shape = lambda w: jax.ShapeDtypeStruct((b, NSA_KV_HEADS, t, w), BF16)
    return pl.pallas_call(
        functools.partial(_kv_prep_kernel, t=t),
        grid=(b, NSA_KV_HEADS),
        in_specs=[spec_in(COL_KVS), spec_in(COL_KVW)],
        out_specs=[spec_out(2 * LANES), spec_out(LANES), spec_out(LANES), spec_out(LANES)],
        out_shape=[shape(2 * LANES), shape(LANES), shape(LANES), shape(LANES)],
        compiler_params=_cparams(("parallel", "parallel")),
        name="kv_prep",
    )(proj, proj)


def _nsa_prompt_kernel(slopes_ref, q_ref, kcv_ref, ks_ref, vs_ref, kw_ref, vw_ref, misc_ref, sfeat_ref, o_ref,
                       *, n16, n_sel):
    n = pl.program_id(1)
    qb = pl.program_id(2)
    rows = NSA_GROUP * Q_BLOCK
    qf = q_ref[...] * SCALE
    q = jnp.concatenate([qf[:, g * HEAD_DIM:(g + 1) * HEAD_DIM] for g in range(NSA_GROUP)], axis=0).astype(BF16)
    tok = lax.broadcasted_iota(jnp.int32, (rows, 1), 0) % Q_BLOCK
    qpos = qb * Q_BLOCK + tok
    slope = _slope_col(slopes_ref, n, Q_BLOCK, rows)

    kcv = kcv_ref[...]
    kc = kcv[:, :HEAD_DIM].astype(BF16)
    vc = kcv[:, HEAD_DIM:].astype(BF16)
    s = _dot_nt(q, kc)
    end = lax.broadcasted_iota(jnp.int32, (1, n16), 1) * CMP_STRIDE + (CMP_BLOCK - 1)
    dist = (qpos - end).astype(F32)
    valid = (dist >= 0) & (end < n16 * CMP_STRIDE)
    s = jnp.where(valid, s - slope * dist, NEG_INF)
    e = jnp.exp(s - jnp.max(s, axis=-1, keepdims=True))
    p = jnp.where(valid, e / jnp.sum(e, axis=-1, keepdims=True), 0.0)
    o_c = _dot(p.astype(BF16), vc)
    p_grp = p[0:Q_BLOCK]
    for g in range(1, NSA_GROUP):
        p_grp = p_grp + p[g * Q_BLOCK:(g + 1) * Q_BLOCK]
    imp = lax.dot_general(_overlap_matrix_t(n_sel, n16), p_grp, (((1,), (1,)), ((), ())),
                          preferred_element_type=F32, precision=HIGHEST)

    jj = lax.broadcasted_iota(jnp.int32, (n_sel, Q_BLOCK), 0)
    cur = (qb * Q_BLOCK + lax.broadcasted_iota(jnp.int32, (1, Q_BLOCK), 1)) // SEL_BLOCK
    forced = (jj == 0) | (jj == cur) | (jj == cur - 1)
    imp = jnp.where(forced, FORCE_SCORE, jnp.where(jj > cur, -FORCE_SCORE, imp))
    beaten = jnp.zeros((n_sel, Q_BLOCK), F32)
    for jp in range(n_sel):
        other = imp[jp:jp + 1, :]
        beats = (other > imp) | ((other == imp) & (jj > jp))
        beaten = beaten + jnp.where(beats, 1.0, 0.0)
    not_picked = jnp.where(beaten < min(N_SELECT, n_sel), 0.0, 1.0)
    if n_sel < LANES:
        not_picked = jnp.concatenate([not_picked, jnp.zeros((LANES - n_sel, Q_BLOCK), F32)], axis=0)
    q_bias = jnp.transpose(not_picked) * MASK_BIAS

    q_main = jnp.concatenate(
        [jnp.concatenate([qf[:, g * HEAD_DIM:(g + 1) * HEAD_DIM],
                          jnp.broadcast_to(sfeat_ref[pl.ds(n * NSA_GROUP + g, 1), HEAD_DIM:LANES],
                                           (Q_BLOCK, LANES - HEAD_DIM))], axis=1)
         for g in range(NSA_GROUP)], axis=0)
    q_win = q_main.astype(BF16)
    q_sel = jnp.concatenate([q_main, jnp.concatenate([q_bias] * NSA_GROUP, axis=0)], axis=1).astype(BF16)

    def flash_step(carry, sc, v):
        m, acc = carry
        m_new = jnp.maximum(m, jnp.max(sc, axis=-1, keepdims=True))
        pr = jnp.exp(sc - m_new).astype(BF16)
        return m_new, jnp.exp(m - m_new) * acc + _dot(pr, v)

    def finish(carry):
        _, acc = carry
        return acc[:, HEAD_DIM:] / acc[:, 0:1]

    init = (jnp.full((rows, 1), M_FLOOR, F32), jnp.zeros((rows, LANES), F32))

    def sel_body(kt, carry):
        r = pl.ds(pl.multiple_of(kt * KEY_TILE, KEY_TILE), KEY_TILE)
        return flash_step(carry, _dot_nt(q_sel, ks_ref[r, :]), vs_ref[r, :])

    n_full = (qb * Q_BLOCK) // KEY_TILE
    carry = lax.fori_loop(0, n_full, sel_body, init)
    r = pl.ds(pl.multiple_of(n_full * KEY_TILE, KEY_TILE), KEY_TILE)
    kpos = n_full * KEY_TILE + lax.broadcasted_iota(jnp.int32, (1, KEY_TILE), 1)
    sc = jnp.where(kpos <= qpos, _dot_nt(q_sel, ks_ref[r, :]), NEG_INF)
    o_s = finish(flash_step(carry, sc, vs_ref[r, :]))

    def win_tile(kt):
        r = pl.ds(pl.multiple_of(kt * Q_BLOCK, Q_BLOCK), Q_BLOCK)
        return _dot_nt(q_win, kw_ref[r, :]), vw_ref[r, :]

    lane_q = lax.broadcasted_iota(jnp.int32, (1, Q_BLOCK), 1)
    kt0 = jnp.maximum(qb - WINDOW // Q_BLOCK, 0)
    sc, v = win_tile(kt0)
    dist = qpos - (kt0 * Q_BLOCK + lane_q)
    ok = (dist >= 0) & (dist < WINDOW) & (jnp.full((1, Q_BLOCK), kt0, jnp.int32) < qb)
    carry = flash_step(init, jnp.where(ok, sc, NEG_INF), v)

    def win_body(kt, carry):
        sc, v = win_tile(kt)
        return flash_step(carry, sc, v)

    carry = lax.fori_loop(kt0 + 1, qb, win_body, carry)
    sc, v = win_tile(qb)
    o_w = finish(flash_step(carry, jnp.where(qb * Q_BLOCK + lane_q <= qpos, sc, NEG_INF), v))

    gates = _pick_head_gates(misc_ref[:, 0:GATE_W], n)
    outs = []
    for g in range(NSA_GROUP):
        r = slice(g * Q_BLOCK, (g + 1) * Q_BLOCK)
        outs.append(gates[:, 3 * g:3 * g + 1] * o_c[r] + gates[:, 3 * g + 1:3 * g + 2] * o_s[r]
                    + gates[:, 3 * g + 2:3 * g + 3] * o_w[r])
    o_ref[...] = jnp.concatenate(outs, axis=1)


def _nsa_prompt(proj, kcv, slopes, b, t):
    n16 = t // CMP_STRIDE
    n_sel = t // SEL_BLOCK
    nqb = t // Q_BLOCK
    assert n_sel <= LANES and t % KEY_TILE == 0
    k_sel, v_sel, k_win, v_win = _kv_prep(proj, b, t)
    seq = lambda w: pl.BlockSpec((None, None, t, w), lambda i, n, qb, sl: (i, n, 0, 0))
    grid_spec = pltpu.PrefetchScalarGridSpec(
        num_scalar_prefetch=1,
        grid=(b, NSA_KV_HEADS, nqb),
        in_specs=[pl.BlockSpec((Q_BLOCK, Q_SLAB), lambda i, n, qb, sl: (i * nqb + qb, COL_Q // Q_SLAB + n)),
                  pl.BlockSpec((None, None, n16, KV_SLAB), lambda i, n, qb, sl: (i, n, 0, 0)),
                  seq(2 * LANES), seq(LANES), seq(LANES), seq(LANES),
                  pl.BlockSpec((Q_BLOCK, LANES), lambda i, n, qb, sl: (i * nqb + qb, COL_MISC // LANES)),
                  pl.BlockSpec((NSA_HEADS, LANES), lambda i, n, qb, sl: (0, 0))],
        out_specs=pl.BlockSpec((Q_BLOCK, Q_SLAB), lambda i, n, qb, sl: (i * nqb + qb, n)),
    )
    return pl.pallas_call(
        functools.partial(_nsa_prompt_kernel, n16=n16, n_sel=n_sel),
        grid_spec=grid_spec,
        out_shape=jax.ShapeDtypeStruct((b * t, NSA_WIDTH), F32),
        compiler_params=_cparams(("parallel", "parallel", "arbitrary")),
        name="nsa_prompt",
    )(slopes, proj, kcv, k_sel, v_sel, k_win, v_win, proj, _slope_features())


def _gla_kernel(q_ref, k_ref, v_ref, r_ref, misc_ref, wa_ref, ba_ref, gout_ref, s0_ref, o_ref, sfin_ref, state_ref,
                *, chunk, sub, valid_rows):
    c = pl.program_id(2)

    @pl.when(c == 0)
    def _():
        state_ref[...] = s0_ref[...]

    ga = misc_ref[:, MISC_GA:MISC_GA + GLA_RANK]
    x = jnp.dot(ga, wa_ref[...], preferred_element_type=F32, precision=HIGHEST) + ba_ref[...]
    lg = (jnp.minimum(x, 0.0) - jnp.log1p(jnp.exp(-jnp.abs(x)))) / GLA_TAU
    row = lax.broadcasted_iota(jnp.int32, (chunk, 1), 0)
    if valid_rows < chunk:
        lg = jnp.where(row < valid_rows, lg, 0.0)
    tri = (lax.broadcasted_iota(jnp.int32, (chunk, chunk), 0)
           >= lax.broadcasted_iota(jnp.int32, (chunk, chunk), 1)).astype(F32)
    cum = jnp.dot(tri, lg, preferred_element_type=F32, precision=HIGHEST)
    q = q_ref[...] * (GLA_DK ** -0.5)
    k = k_ref[...]
    v = v_ref[...]
    vb = v.astype(BF16)
    state = state_ref[...]
    inter = _dot((q * jnp.exp(cum)).astype(BF16), state.astype(BF16))

    outs = []
    for i in range(chunk // sub):
        r0 = i * sub
        qi, ki, ci, vi = q[r0:r0 + sub], k[r0:r0 + sub], cum[r0:r0 + sub], v[r0:r0 + sub]
        o_i = inter[r0:r0 + sub]
        if i > 0:
            anchor = cum[r0:r0 + 1]
            qd = (qi * jnp.exp(ci - anchor)).astype(BF16)
            kd = (k[0:r0] * jnp.exp(anchor - cum[0:r0])).astype(BF16)
            o_i = o_i + _dot(_dot_nt(qd, kd).astype(BF16), vb[0:r0])
        trow = lax.broadcasted_iota(jnp.int32, (sub, 1), 0)
        for s_ in range(sub):
            w = jnp.sum(qi * (ki[s_:s_ + 1] * jnp.exp(jnp.minimum(ci - ci[s_:s_ + 1], 0.0))), axis=-1, keepdims=True)
            o_i = o_i + jnp.where(trow >= s_, w, 0.0) * vi[s_:s_ + 1]
        outs.append(o_i)
    o = jnp.concatenate(outs, axis=0) if len(outs) > 1 else outs[0]

    last = cum[chunk - 1:chunk]
    kdec = (k * jnp.exp(last - cum)).astype(BF16)
    decay_col = jnp.transpose(jnp.broadcast_to(jnp.exp(last), (SUBLANES, GLA_DK)))[:, 0:1]
    new_state = decay_col * state + _dot_tn(kdec, vb)
    state_ref[...] = new_state

    @pl.when(c == pl.num_programs(2) - 1)
    def _():
        sfin_ref[...] = new_state

    o_ref[...] = _rms(o, gout_ref[...]) * (r_ref[...] * _sigmoid(r_ref[...]))


def _gla(proj, w_a2, b_a, g_out, s0, b, t, chunk, sub, valid_rows):
    nck = t // chunk
    hk = GLA_DK
    return pl.pallas_call(
        functools.partial(_gla_kernel, chunk=chunk, sub=sub, valid_rows=valid_rows),
        grid=(b, GLA_HEADS, nck),
        in_specs=[pl.BlockSpec((chunk, GLA_DK), lambda i, h, c: (i * nck + c, COL_GQ // GLA_DK + h)),
                  pl.BlockSpec((chunk, GLA_DK), lambda i, h, c: (i * nck + c, COL_GK // GLA_DK + h)),
                  pl.BlockSpec((chunk, GLA_DV), lambda i, h, c: (i * nck + c, COL_GV // GLA_DV + h)),
                  pl.BlockSpec((chunk, GLA_DV), lambda i, h, c: (i * nck + c, COL_GR // GLA_DV + h)),
                  pl.BlockSpec((chunk, LANES), lambda i, h, c: (i * nck + c, COL_MISC // LANES)),
                  pl.BlockSpec((GLA_RANK, hk), lambda i, h, c: (0, h)),
                  pl.BlockSpec((1, hk), lambda i, h, c: (0, h)),
                  pl.BlockSpec((1, GLA_DV), lambda i, h, c: (0, 0)),
                  pl.BlockSpec((None, None, GLA_DK, GLA_DV), lambda i, h, c: (i, h, 0, 0))],
        out_specs=[pl.BlockSpec((chunk, GLA_DV), lambda i, h, c: (i * nck + c, h)),
                   pl.BlockSpec((None, None, GLA_DK, GLA_DV), lambda i, h, c: (i, h, 0, 0))],
        out_shape=[jax.ShapeDtypeStruct((b * t, GLA_WIDTH), F32),
                   jax.ShapeDtypeStruct((b, GLA_HEADS, GLA_DK, GLA_DV), F32)],
        scratch_shapes=[pltpu.VMEM((GLA_DK, GLA_DV), F32)],
        compiler_params=_cparams(("parallel", "parallel", "arbitrary")),
        name="gla",
    )(proj, proj, proj, proj, proj, w_a2, b_a, g_out, s0)


def _out_proj_kernel(on_ref, og_ref, x_ref, g_ref, w_ref, o_ref, a_ref):
    @pl.when(pl.program_id(1) == 0)
    def _():
        a_ref[:, :NSA_WIDTH] = _rms(on_ref[...], g_ref[...]).astype(BF16)
        a_ref[:, NSA_WIDTH:] = og_ref[...].astype(BF16)

    o_ref[...] = x_ref[...] + _dot(a_ref[...], w_ref[...])


def _out_proj(o_nsa, o_gla, x, g_nsa, w_out, tm, tn):
    m = x.shape[0]
    return pl.pallas_call(
        _out_proj_kernel,
        grid=(m // tm, D_MODEL // tn),
        in_specs=[pl.BlockSpec((tm, NSA_WIDTH), lambda i, j: (i, 0)),
                  pl.BlockSpec((tm, GLA_WIDTH), lambda i, j: (i, 0)),
                  pl.BlockSpec((tm, tn), lambda i, j: (i, j)),
                  pl.BlockSpec((1, NSA_WIDTH), lambda i, j: (0, 0)),
                  pl.BlockSpec((NSA_WIDTH + GLA_WIDTH, tn), lambda i, j: (0, j))],
        out_specs=pl.BlockSpec((tm, tn), lambda i, j: (i, j)),
        out_shape=jax.ShapeDtypeStruct((m, D_MODEL), F32),
        scratch_shapes=[pltpu.VMEM((tm, NSA_WIDTH + GLA_WIDTH), BF16)],
        compiler_params=_cparams(("parallel", "arbitrary")),
        name="out_proj",
    )(o_nsa, o_gla, x, g_nsa, w_out)


def _ffn_kernel(h_ref, g_ref, wa_ref, wg_ref, cw_ref, cb_ref, wd_ref, p1_ref, p2_ref, o_ref, tail_ref,
                n2_ref, acc_ref, carry_ref, *, tm, tf, seq_rows):
    i = pl.program_id(0)
    j = pl.program_id(1)

    @pl.when(j == 0)
    def _():
        n2_ref[...] = _rms(h_ref[...], g_ref[...]).astype(BF16)
        acc_ref[...] = jnp.zeros_like(acc_ref)

    n2 = n2_ref[...]
    a = _dot(n2, wa_ref[...])
    gate = _dot(n2, wg_ref[...])
    row = lax.broadcasted_iota(jnp.int32, (tm, 1), 0)
    r1 = pltpu.roll(a, 1, axis=0)
    r2 = pltpu.roll(a, 2, axis=0)
    if seq_rows >= tm:
        cols = pl.ds(pl.multiple_of(j * tf, tf), tf)
        first = (i % (seq_rows // tm)) == 0
        prev = jnp.where(first, p2_ref[...], carry_ref[:, cols])
        a1 = jnp.where(row == 0, prev[1:2], r1)
        a2 = jnp.where(row == 0, prev[0:1], jnp.where(row == 1, prev[1:2], r2))
        carry_ref[:, cols] = a[tm - 2:tm]
    else:
        t = row % seq_rows
        a1 = jnp.where(t == 0, p1_ref[...], r1)
        a2 = jnp.where(t < 2, p2_ref[...], r2)
    cw = cw_ref[...]
    conv = cb_ref[...] + a2 * cw[0:1] + a1 * cw[1:2] + a * cw[2:3]
    y = (_gelu_tanh(conv) * gate).astype(BF16)
    acc_ref[...] += _dot(y, wd_ref[...])
    tail_ref[...] = a[tm - tail_ref.shape[0]:tm]

    @pl.when(j == pl.num_programs(1) - 1)
    def _():
        o_ref[...] = h_ref[...] + acc_ref[...]


def _ffn(h, g_ffn, w_up, conv_w, conv_b, w_down, p1, p2, tm, tf, seq_rows):
    m = h.shape[0]
    nj = D_FF // tf
    if seq_rows >= tm:
        tiles_per_seq = seq_rows // tm
        p1_spec = pl.BlockSpec((None, CONV_W - 1, tf), lambda i, j: (i // tiles_per_seq, 0, j))
        p2_spec = pl.BlockSpec((None, CONV_W - 1, tf), lambda i, j: (i // tiles_per_seq, 0, j))
    else:
        p1_spec = pl.BlockSpec((tm, tf), lambda i, j: (i, j))
        p2_spec = pl.BlockSpec((tm, tf), lambda i, j: (i, j))
    if seq_rows >= tm:
        tail_spec = pl.BlockSpec((None, SUBLANES, tf), lambda i, j: (i, 0, j))
        tail_shape = jax.ShapeDtypeStruct((m // tm, SUBLANES, D_FF), F32)
    else:
        tail_spec = pl.BlockSpec((tm, tf), lambda i, j: (i, j))
        tail_shape = jax.ShapeDtypeStruct((m, D_FF), F32)
    return pl.pallas_call(
        functools.partial(_ffn_kernel, tm=tm, tf=tf, seq_rows=seq_rows),
        grid=(m // tm, nj),
        in_specs=[pl.BlockSpec((tm, D_MODEL), lambda i, j: (i, 0)),
                  pl.BlockSpec((1, D_MODEL), lambda i, j: (0, 0)),
                  pl.BlockSpec((D_MODEL, tf), lambda i, j: (0, j)),
                  pl.BlockSpec((D_MODEL, tf), lambda i, j: (0, nj + j)),
                  pl.BlockSpec((CONV_W, tf), lambda i, j: (0, j)),
                  pl.BlockSpec((1, tf), lambda i, j: (0, j)),
                  pl.BlockSpec((tf, D_MODEL), lambda i, j: (j, 0)),
                  p1_spec, p2_spec],
        out_specs=[pl.BlockSpec((tm, D_MODEL), lambda i, j: (i, 0)),
                   tail_spec],
        out_shape=[jax.ShapeDtypeStruct((m, D_MODEL), F32), tail_shape],
        scratch_shapes=[pltpu.VMEM((tm, D_MODEL), BF16), pltpu.VMEM((tm, D_MODEL), F32),
                        pltpu.VMEM((CONV_W - 1, D_FF), F32)],
        compiler_params=_cparams(("arbitrary", "arbitrary")),
        name="conv_ffn",
    )(h, g_ffn, w_up, w_up, conv_w, conv_b, w_down, p1, p2)


def _ple_kernel(h_ref, p_ref, wg_ref, wp_ref, gp_ref, gf_ref, o_ref, *, final_norm):
    h = h_ref[...]
    gate = _sigmoid(_dot(h.astype(BF16), wg_ref[...]))
    pe = _rms(_dot(p_ref[...].astype(BF16), wp_ref[...]), gp_ref[...])
    h = h + gate * pe
    o_ref[...] = _rms(h, gf_ref[...]) if final_norm else h


def _ple(h, p, w_gate, w_proj, g_ple, g_final, tm, final_norm):
    m = h.shape[0]
    return pl.pallas_call(
        functools.partial(_ple_kernel, final_norm=final_norm),
        grid=(m // tm,),
        in_specs=[pl.BlockSpec((tm, D_MODEL), lambda i: (i, 0)),
                  pl.BlockSpec((tm, PLE_DIM), lambda i: (i, 0)),
                  pl.BlockSpec((D_MODEL, D_MODEL), lambda i: (0, 0)),
                  pl.BlockSpec((PLE_DIM, D_MODEL), lambda i: (0, 0)),
                  pl.BlockSpec((1, D_MODEL), lambda i: (0, 0)),
                  pl.BlockSpec((1, D_MODEL), lambda i: (0, 0))],
        out_specs=pl.BlockSpec((tm, D_MODEL), lambda i: (i, 0)),
        out_shape=jax.ShapeDtypeStruct((m, D_MODEL), F32),
        compiler_params=_cparams(("parallel",)),
        name="ple_norm",
    )(h, p, w_gate, w_proj, g_ple, g_final)


CMP_PAGES = 16
CHUNKS_PER_PAGE = PAGE_SIZE // CMP_STRIDE


def _cmp_sample_kernel(pt_ref, *refs):
    page_refs = refs[:CMP_PAGES]
    perm_ref, w1_ref, w1p_ref, w2_ref, pe_ref, o_ref, carry_ref, x_ref = refs[CMP_PAGES:]
    grp = pl.program_id(1)
    nck = CMP_PAGES * CHUNKS_PER_PAGE

    @pl.when(grp == 0)
    def _():
        carry_ref[...] = jnp.zeros_like(carry_ref)

    perm = perm_ref[...]
    for k, r in enumerate(page_refs):
        xp = _dot_nt(perm, r[...].astype(BF16))
        for s in range(CMP_STRIDE):
            for n in range(NSA_KV_HEADS):
                x_ref[s, n, k * CHUNKS_PER_PAGE:(k + 1) * CHUNKS_PER_PAGE, :] = (
                    xp[s * CHUNKS_PER_PAGE:(s + 1) * CHUNKS_PER_PAGE, n * KV_SLAB:(n + 1) * KV_SLAB])
    acc = jnp.zeros((NSA_KV_HEADS * nck, 4 * CMP_HIDDEN), F32)
    for sp in range(CMP_STRIDE // 2):
        xs = jnp.concatenate([x_ref[2 * sp].reshape(NSA_KV_HEADS * nck, KV_SLAB),
                              x_ref[2 * sp + 1].reshape(NSA_KV_HEADS * nck, KV_SLAB)], axis=1)
        acc = acc + _dot(xs.astype(BF16), w1p_ref[sp])
    a = acc[:, :2 * CMP_HIDDEN]
    b = acc[:, 2 * CMP_HIDDEN:]
    a_prev = pltpu.roll(a, 1, axis=0)
    row = lax.broadcasted_iota(jnp.int32, (NSA_KV_HEADS * nck, 1), 0)
    for n in range(NSA_KV_HEADS):
        a_prev = jnp.where(row == n * nck, carry_ref[n:n + 1, :], a_prev)
        carry_ref[n:n + 1, :] = a[(n + 1) * nck - 1:(n + 1) * nck]
    hid = a_prev + b + _cmp_pe_hidden(pe_ref, w1_ref)
    res = _dot(_gelu_tanh(hid).astype(BF16), w2_ref[...])
    for n in range(NSA_KV_HEADS):
        o_ref[n] = res[n * nck:(n + 1) * nck]


def _cmp_sample(cache_c, page_table_flat, w1, w2, pe, b, n_pages):
    n_grp = n_pages // CMP_PAGES
    nck = CMP_PAGES * CHUNKS_PER_PAGE

    def page_spec(k):
        return pl.BlockSpec((None, KV_WIDTH, PAGE_SIZE),
                            lambda i, g, pt: (pt[i * n_pages + g * CMP_PAGES + k], 0, 0))

    r = np.arange(PAGE_SIZE)
    perm_np = np.zeros((PAGE_SIZE, PAGE_SIZE), np.float32)
    perm_np[(r % CMP_STRIDE) * CHUNKS_PER_PAGE + r // CMP_STRIDE, r] = 1.0
    perm = jnp.asarray(perm_np, dtype=BF16)
    w1p = w1.reshape(CMP_STRIDE // 2, 2 * KV_SLAB, 4 * CMP_HIDDEN)

    grid_spec = pltpu.PrefetchScalarGridSpec(
        num_scalar_prefetch=1,
        grid=(b, n_grp),
        in_specs=[page_spec(k) for k in range(CMP_PAGES)]
        + [pl.BlockSpec(perm.shape, lambda i, g, pt: (0, 0)),
           pl.BlockSpec(w1.shape, lambda i, g, pt: (0, 0, 0)),
           pl.BlockSpec(w1p.shape, lambda i, g, pt: (0, 0, 0)),
           pl.BlockSpec(w2.shape, lambda i, g, pt: (0, 0)),
           pl.BlockSpec(pe.shape, lambda i, g, pt: (0, 0))],
        out_specs=pl.BlockSpec((None, NSA_KV_HEADS, nck, KV_SLAB), lambda i, g, pt: (i, 0, g, 0)),
        scratch_shapes=[pltpu.VMEM((SUBLANES, 2 * CMP_HIDDEN), F32),
                        pltpu.VMEM((CMP_STRIDE, NSA_KV_HEADS, nck, KV_SLAB), F32)],
    )
    return pl.pallas_call(
        _cmp_sample_kernel,
        grid_spec=grid_spec,
        out_shape=jax.ShapeDtypeStruct((b, NSA_KV_HEADS, n_pages * CHUNKS_PER_PAGE, KV_SLAB), F32),
        compiler_params=_cparams(("parallel", "arbitrary")),
        name="cmp_sample",
    )(page_table_flat, *([cache_c] * CMP_PAGES), perm, w1, w1p, w2, pe)


def _sel_sample_kernel(slopes_ref, q_ref, kcv_ref, oc_ref, idx_ref, *, past_len, n_sel_pad, t_valid):
    n = pl.program_id(1)
    tp = SUBLANES
    rows = NSA_GROUP * tp
    n_rows_c = past_len // CMP_STRIDE
    nb_past = past_len // SEL_BLOCK
    qf = q_ref[...] * SCALE
    q = jnp.concatenate([qf[:, g * HEAD_DIM:(g + 1) * HEAD_DIM] for g in range(NSA_GROUP)], axis=0).astype(BF16)
    tok = lax.broadcasted_iota(jnp.int32, (rows, 1), 0) % tp
    qpos = past_len + tok
    slope = _slope_col(slopes_ref, n, tp, rows)
    kcv = kcv_ref[...]
    kc = kcv[:, :HEAD_DIM].astype(BF16)
    vc = kcv[:, HEAD_DIM:].astype(BF16)
    s = _dot_nt(q, kc)
    cp = lax.broadcasted_iota(jnp.int32, (1, n_rows_c), 1)
    end = (cp - 1) * CMP_STRIDE + (CMP_BLOCK - 1)
    dist = (qpos - end).astype(F32)
    valid = (cp >= 1) & (dist >= 0)
    s = jnp.where(valid, s - slope * dist, NEG_INF)
    e = jnp.exp(s - jnp.max(s, axis=-1, keepdims=True))
    p = jnp.where(valid, e / jnp.sum(e, axis=-1, keepdims=True), 0.0)
    o_c = _dot(p.astype(BF16), vc)
    oc_ref[...] = jnp.concatenate([o_c[g * tp:(g + 1) * tp] for g in range(NSA_GROUP)], axis=1)
    p_grp = p[0:tp]
    for g in range(1, NSA_GROUP):
        p_grp = p_grp + p[g * tp:(g + 1) * tp]
    imp = jnp.dot(p_grp, _overlap_matrix(n_rows_c, n_sel_pad, 1), preferred_element_type=F32, precision=HIGHEST)
    n_tail = -(-t_valid // SEL_BLOCK)
    n_sel = nb_past + n_tail
    j = lax.broadcasted_iota(jnp.int32, (tp, n_sel_pad), 1)
    cur = qpos[0:tp] // SEL_BLOCK
    forced = (j == 0) | (j == cur) | (j == cur - 1)
    imp = jnp.where(forced, FORCE_SCORE, jnp.where(j > cur, -FORCE_SCORE, imp))
    imp = jnp.where(j < n_sel, imp, REMOVED_SCORE)
    _, picks = _top_blocks(imp, min(N_SELECT, n_sel))
    kcol = lax.broadcasted_iota(jnp.int32, (tp, N_SELECT), 1)
    idx = jnp.zeros((tp, N_SELECT), jnp.int32)
    for kk, pk in enumerate(picks):
        idx = jnp.where(kcol == kk, pk, idx)
    idx_ref[...] = idx


def _sel_sample(proj8, kcv, slopes, b, past_len, t_valid):
    n_rows_c = past_len // CMP_STRIDE
    nb_past = past_len // SEL_BLOCK
    n_sel_pad = -(-(nb_past + 1) // LANES) * LANES
    grid_spec = pltpu.PrefetchScalarGridSpec(
        num_scalar_prefetch=1,
        grid=(b, NSA_KV_HEADS),
        in_specs=[pl.BlockSpec((SUBLANES, Q_SLAB), lambda i, n, sl: (i, COL_Q // Q_SLAB + n)),
                  pl.BlockSpec((None, None, n_rows_c, KV_SLAB), lambda i, n, sl: (i, n, 0, 0))],
        out_specs=[pl.BlockSpec((None, None, SUBLANES, Q_SLAB), lambda i, n, sl: (i, n, 0, 0)),
                   pl.BlockSpec((None, None, SUBLANES, N_SELECT), lambda i, n, sl: (i, n, 0, 0))],
    )
    return pl.pallas_call(
        functools.partial(_sel_sample_kernel, past_len=past_len, n_sel_pad=n_sel_pad, t_valid=t_valid),
        grid_spec=grid_spec,
        out_shape=[jax.ShapeDtypeStruct((b, NSA_KV_HEADS, SUBLANES, Q_SLAB), F32),
                   jax.ShapeDtypeStruct((b, NSA_KV_HEADS, SUBLANES, N_SELECT), jnp.int32)],
        compiler_params=_cparams(("parallel", "parallel")),
        name="sel_sample",
    )(slopes, proj8, kcv)


def _nsa_sample_kernel(idx_ref, pt_ref, slopes_ref, *refs, past_len, n_pages):
    blk_refs = refs[:N_SELECT]
    q_ref, tail_ref, wnew_ref, wcache_ref, oc_ref, misc_ref, o_ref = refs[N_SELECT:]
    i = pl.program_id(0)
    n = pl.program_id(1)
    t = pl.program_id(2)
    nb_past = past_len // SEL_BLOCK
    g8 = SUBLANES
    qrow = q_ref[pl.ds(t, 1), :] * SCALE
    q = jnp.concatenate([qrow[:, g * HEAD_DIM:(g + 1) * HEAD_DIM] for g in range(NSA_GROUP)]
                        + [jnp.zeros((g8 - NSA_GROUP, HEAD_DIM), F32)], axis=0).astype(BF16)
    slope = _slope_col(slopes_ref, n, 1, g8)
    qpos = past_len + t
    base = ((i * NSA_KV_HEADS + n) * SUBLANES + t) * N_SELECT

    parts = []
    tail_count = jnp.zeros((), jnp.int32)
    per_page = PAGE_SIZE // SEL_BLOCK
    lane = lax.broadcasted_iota(jnp.int32, (1, PAGE_SIZE), 1)
    for kk in range(N_SELECT):
        jb = idx_ref[base + kk]
        tail_count = tail_count + (jb >= nb_past).astype(jnp.int32)
        kvt = blk_refs[kk][...]
        kpos = (jb // per_page) * PAGE_SIZE + lane
        dist = (qpos - kpos).astype(F32)
        jb_row = jnp.full((1, PAGE_SIZE), jb, jnp.int32)
        ok = (dist >= 0) & (kpos // SEL_BLOCK == jb_row) & (jb_row < nb_past)
        sc = _dot(q, kvt[:HEAD_DIM].astype(BF16)) - slope * dist
        parts.append((jnp.where(ok, sc, NEG_INF), ok, kvt[HEAD_DIM:].astype(BF16), True))
    tl = tail_ref[...]
    lane8 = lax.broadcasted_iota(jnp.int32, (1, SUBLANES), 1)
    dist = (t - lane8).astype(F32)
    ok = (dist >= 0) & (jnp.full((1, SUBLANES), tail_count, jnp.int32) > 0)
    sc = _dot_nt(q, tl[:, :HEAD_DIM].astype(BF16)) - slope * dist
    parts.append((jnp.where(ok, sc, NEG_INF), ok, tl[:, HEAD_DIM:].astype(BF16), False))

    def softmax_av(parts):
        m = parts[0][0].max(axis=-1, keepdims=True)
        for sc, _, _, _ in parts[1:]:
            m = jnp.maximum(m, sc.max(axis=-1, keepdims=True))
        l = jnp.zeros((g8, 1), F32)
        acc = jnp.zeros((g8, HEAD_DIM), F32)
        for sc, ok, v, v_transposed in parts:
            pr = jnp.where(ok, jnp.exp(sc - m), 0.0)
            l = l + pr.sum(axis=-1, keepdims=True)
            acc = acc + (_dot_nt(pr.astype(BF16), v) if v_transposed else _dot(pr.astype(BF16), v))
        return acc / l

    o_s = softmax_av(parts)

    wct = wcache_ref[...]
    buf_len = wct.shape[1]
    lane_w = lax.broadcasted_iota(jnp.int32, (1, buf_len), 1)
    dist_c = (qpos - (past_len - buf_len + lane_w)).astype(F32)
    ok_c = (dist_c >= 0) & (dist_c < WINDOW)
    sc_c = jnp.where(ok_c, _dot(q, wct[:HEAD_DIM].astype(BF16)) - slope * dist_c, NEG_INF)
    wn = wnew_ref[...]
    dist_n = (t - lane8).astype(F32)
    ok_n = (dist_n >= 0) & (dist_n < WINDOW)
    sc_n = jnp.where(ok_n, _dot_nt(q, wn[:, :HEAD_DIM].astype(BF16)) - slope * dist_n, NEG_INF)
    o_w = softmax_av([(sc_c, ok_c, wct[HEAD_DIM:].astype(BF16), True),
                      (sc_n, ok_n, wn[:, HEAD_DIM:].astype(BF16), False)])

    oc_row = oc_ref[pl.ds(t, 1), :]
    gates = _pick_head_gates(misc_ref[pl.ds(t, 1), 0:GATE_W], n)
    outs = []
    for g in range(NSA_GROUP):
        outs.append(gates[:, 3 * g:3 * g + 1] * oc_row[:, g * HEAD_DIM:(g + 1) * HEAD_DIM]
                    + gates[:, 3 * g + 1:3 * g + 2] * o_s[g:g + 1]
                    + gates[:, 3 * g + 2:3 * g + 3] * o_w[g:g + 1])
    o_ref[...] = jnp.concatenate(outs, axis=1)


def _nsa_sample(proj8, cache_s, cache_w, o_c, idx_flat, page_table_flat, slopes, b, t_valid, past_len):
    n_pages = past_len // PAGE_SIZE
    nb_past = past_len // SEL_BLOCK
    per_page = PAGE_SIZE // SEL_BLOCK
    buf_len = cache_w.shape[-1]

    def blk_spec(kk):
        def imap(i, n, t, idx, pt, sl):
            jb = jnp.minimum(idx[((i * NSA_KV_HEADS + n) * SUBLANES + t) * N_SELECT + kk], nb_past - 1)
            return (pt[i * n_pages + jb // per_page], n, 0, 0)
        return pl.BlockSpec((None, None, KV_SLAB, PAGE_SIZE), imap)

    grid_spec = pltpu.PrefetchScalarGridSpec(
        num_scalar_prefetch=3,
        grid=(b, NSA_KV_HEADS, t_valid),
        in_specs=[blk_spec(kk) for kk in range(N_SELECT)]
        + [pl.BlockSpec((SUBLANES, Q_SLAB), lambda i, n, t, *_: (i, COL_Q // Q_SLAB + n)),
           pl.BlockSpec((SUBLANES, KV_SLAB), lambda i, n, t, *_: (i, COL_KVS // KV_SLAB + n)),
           pl.BlockSpec((SUBLANES, KV_SLAB), lambda i, n, t, *_: (i, COL_KVW // KV_SLAB + n)),
           pl.BlockSpec((None, None, KV_SLAB, buf_len), lambda i, n, t, *_: (i, n, 0, 0)),
           pl.BlockSpec((None, None, SUBLANES, Q_SLAB), lambda i, n, t, *_: (i, n, 0, 0)),
           pl.BlockSpec((SUBLANES, LANES), lambda i, n, t, *_: (i, COL_MISC // LANES))],
        out_specs=pl.BlockSpec((None, 1, Q_SLAB), lambda i, n, t, *_: (i * SUBLANES + t, 0, n)),
    )
    return pl.pallas_call(
        functools.partial(_nsa_sample_kernel, past_len=past_len, n_pages=n_pages),
        grid_spec=grid_spec,
        out_shape=jax.ShapeDtypeStruct((b * SUBLANES, 1, NSA_WIDTH), F32),
        compiler_params=_cparams(("parallel", "parallel", "arbitrary")),
        name="nsa_sample",
    )(idx_flat, page_table_flat, slopes, *([cache_s] * N_SELECT), proj8, proj8, proj8, cache_w, o_c, proj8)


def _alibi_slopes():
    h = np.arange(1, NSA_HEADS + 1, dtype=np.float32)
    return jnp.asarray(2.0 ** (-8.0 * h / NSA_HEADS), dtype=F32)


def _permute_w_in(w_in):
    offs = np.cumsum([0, NSA_WIDTH, KV_WIDTH, KV_WIDTH, KV_WIDTH, GATE_W, GLA_HEADS * GLA_DK, GLA_HEADS * GLA_DK,
                      GLA_WIDTH, GLA_WIDTH, GLA_RANK])
    piece = [w_in[:, offs[k]:offs[k + 1]] for k in range(10)]
    q, kvc, kvs, kvw, gl, gq, gk, gv, gr, ga = piece
    pad = jnp.zeros((w_in.shape[0], LANES - GATE_W - GLA_RANK), w_in.dtype)
    return jnp.concatenate([q, kvc, kvs, kvw, gq, gk, gv, gr, gl, ga, pad], axis=1).astype(BF16)


def _row_tile(m, pref):
    return pref if m % pref == 0 else m


def _dense_tail(h_in, o_nsa, o_gla, p_emb, wts, conv_p1, conv_p2, seq_rows, last_layer, tm):
    (g_nsa, w_out, g_ffn, w_up, conv_w, conv_b, w_down, w_ple_proj, g_ple, w_ple_gate, g_final) = wts
    h = _out_proj(o_nsa, o_gla, h_in, g_nsa, w_out, tm, D_MODEL // 2)
    h, tails = _ffn(h, g_ffn, w_up, conv_w, conv_b, w_down, conv_p1, conv_p2, tm, 512, seq_rows)
    y = _ple(h, p_emb, w_ple_gate, w_ple_proj, g_ple, g_final, tm, last_layer)
    return y, tails


def kernel(x_prompt, x_sample, p_prompt, p_sample, cache_cmp_kv, cache_sel_kv, cache_win_kv, state_gla, state_ffn_conv, page_table, g_attn, w_in, w_cmp_k1, w_cmp_k2, pe_cmp_k, w_cmp_v1, w_cmp_v2, pe_cmp_v, w_gla_a2, b_gla_a, g_nsa_out, g_gla_out, w_out, g_ffn, w_up, conv_w, conv_b, w_down, w_ple_proj, g_ple, w_ple_gate, g_final):
    depth = w_in.shape[0]
    bp, tp, _ = x_prompt.shape
    bs, ts, _ = x_sample.shape
    n_pages = page_table.shape[1]
    past_len = n_pages * PAGE_SIZE
    n_pool = cache_cmp_kv.shape[1]
    assert tp % Q_BLOCK == 0 and tp >= WINDOW and ts <= SUBLANES and ts <= SEL_BLOCK and ts >= CONV_W - 1
    assert n_pages % CMP_PAGES == 0 and tp // SEL_BLOCK >= N_SELECT
    slopes = _alibi_slopes()
    pt_flat = page_table.reshape(-1).astype(jnp.int32)
    kv_shape = (NSA_KV_HEADS, 2, HEAD_DIM)

    hp = x_prompt.reshape(bp * tp, D_MODEL)
    hs = x_sample.reshape(bs * ts, D_MODEL)
    new_p = [[] for _ in range(5)]
    new_s = [[] for _ in range(5)]
    tm_p = _row_tile(bp * tp, 512)
    tm_s = bs * ts
    for i in range(depth):
        last = i == depth - 1
        w_in_p = _permute_w_in(w_in[i])
        g_a = g_attn[i].reshape(1, D_MODEL)
        w1c, w2c, pec = _cmp_weights(w_cmp_k1[i], w_cmp_k2[i], pe_cmp_k[i], w_cmp_v1[i], w_cmp_v2[i], pe_cmp_v[i])
        b_a = b_gla_a[i].reshape(1, -1)
        g_go = g_gla_out[i].reshape(1, GLA_DV)
        wts = (g_nsa_out[i].reshape(1, -1), w_out[i].astype(BF16), g_ffn[i].reshape(1, -1), w_up[i].astype(BF16),
               conv_w[i], conv_b[i].reshape(1, -1), w_down[i].astype(BF16), w_ple_proj[i].astype(BF16),
               g_ple[i].reshape(1, -1), w_ple_gate[i].astype(BF16), g_final.reshape(1, -1))

        proj = _norm_matmul(hp, g_a, w_in_p, tm_p, D_IN_PAD // 5)
        kcv = _cmp_prompt(proj, bp, tp, w1c, w2c, pec)
        o_nsa = _nsa_prompt(proj, kcv, slopes, bp, tp)
        s0 = jnp.zeros((bp, GLA_HEADS, GLA_DK, GLA_DV), F32)
        o_gla, s_new = _gla(proj, w_gla_a2[i], b_a, g_go, s0, bp, tp, 64, 16, 64)
        zbuf = jnp.zeros((bp, CONV_W - 1, D_FF), F32)
        hp, tails = _dense_tail(hp, o_nsa, o_gla, p_prompt[i].reshape(bp * tp, PLE_DIM), wts, zbuf, zbuf, tp, last, tm_p)
        proj3 = proj.reshape(bp, tp, D_IN_PAD)
        new_p[0].append(proj3[:, :, COL_KVC:COL_KVC + KV_WIDTH].reshape((bp, tp) + kv_shape))
        new_p[1].append(proj3[:, :, COL_KVS:COL_KVS + KV_WIDTH].reshape((bp, tp) + kv_shape))
        new_p[2].append(proj3[:, tp - WINDOW:, COL_KVW:COL_KVW + KV_WIDTH].reshape((bp, WINDOW) + kv_shape))
        new_p[3].append(s_new)
        tiles_per_seq = tp // tm_p
        new_p[4].append(tails.reshape(bp, tiles_per_seq, SUBLANES, D_FF)[:, -1, SUBLANES - (CONV_W - 1):, :])

        proj_s = _norm_matmul(hs, g_a, w_in_p, tm_s, D_IN_PAD // 5)
        proj8 = jnp.pad(proj_s.reshape(bs, ts, D_IN_PAD), ((0, 0), (0, SUBLANES - ts), (0, 0))).reshape(bs * SUBLANES, D_IN_PAD)
        cache_c = jnp.transpose(cache_cmp_kv[i], (0, 2, 3, 4, 1)).reshape(n_pool, KV_WIDTH, PAGE_SIZE)
        cache_s = jnp.transpose(cache_sel_kv[i], (0, 2, 3, 4, 1)).reshape(n_pool, NSA_KV_HEADS, KV_SLAB, PAGE_SIZE)
        cache_w = jnp.transpose(cache_win_kv[i], (0, 2, 3, 4, 1)).reshape(bs, NSA_KV_HEADS, KV_SLAB, -1)
        kcv_s = _cmp_sample(cache_c, pt_flat, w1c, w2c, pec, bs, n_pages)
        o_c, idx = _sel_sample(proj8, kcv_s, slopes, bs, past_len, ts)
        o_nsa_s = _nsa_sample(proj8, cache_s, cache_w, o_c, idx.reshape(-1), pt_flat, slopes, bs, ts, past_len)
        o_nsa_s = o_nsa_s.reshape(bs, SUBLANES, NSA_WIDTH)[:, :ts].reshape(bs * ts, NSA_WIDTH)
        o_gla_s, s_new_s = _gla(proj8, w_gla_a2[i], b_a, g_go, state_gla[i].astype(F32), bs, SUBLANES, SUBLANES, SUBLANES, ts)
        o_gla_s = o_gla_s.reshape(bs, SUBLANES, GLA_WIDTH)[:, :ts].reshape(bs * ts, GLA_WIDTH)
        buf = state_ffn_conv[i]
        zrow = jnp.zeros((bs, ts - 1, D_FF), F32)
        p1 = jnp.concatenate([buf[:, 1:2], zrow], axis=1).reshape(bs * ts, D_FF)
        p2 = jnp.concatenate([buf, jnp.zeros((bs, ts - 2, D_FF), F32)], axis=1).reshape(bs * ts, D_FF)
        hs, tails_s = _dense_tail(hs, o_nsa_s, o_gla_s, p_sample[i].reshape(bs * ts, PLE_DIM), wts, p1, p2, ts, last, tm_s)
        ps3 = proj_s.reshape(bs, ts, D_IN_PAD)
        new_s[0].append(ps3[:, :, COL_KVC:COL_KVC + KV_WIDTH].reshape((bs, ts) + kv_shape))
        new_s[1].append(ps3[:, :, COL_KVS:COL_KVS + KV_WIDTH].reshape((bs, ts) + kv_shape))
        new_s[2].append(ps3[:, :, COL_KVW:COL_KVW + KV_WIDTH].reshape((bs, ts) + kv_shape))
        new_s[3].append(s_new_s)
        new_s[4].append(tails_s.reshape(bs, ts, D_FF)[:, ts - (CONV_W - 1):, :])

    y_prompt = hp.reshape(bp, tp, D_MODEL)
    y_sample = hs.reshape(bs, ts, D_MODEL)
    cmp_p, sel_p, win_p, gla_p, conv_p = [jnp.stack(l) for l in new_p]
    cmp_s, sel_s, win_s, gla_s, conv_s = [jnp.stack(l) for l in new_s]
    return (y_prompt, y_sample, cmp_p, sel_p, win_p, gla_p, conv_p, cmp_s, sel_s, win_s, gla_s, conv_s)
```

```python
import functools

import numpy as np
import jax
import jax.numpy as jnp
from jax import lax
from jax.experimental import pallas as pl
from jax.experimental.pallas import tpu as pltpu

F32 = jnp.float32
BF16 = jnp.bfloat16
HIGHEST = lax.Precision.HIGHEST

D_MODEL = 2048
PAGE_SIZE = 128
NSA_HEADS = 16
NSA_KV_HEADS = 4
NSA_GROUP = NSA_HEADS // NSA_KV_HEADS
HEAD_DIM = 64
CMP_BLOCK = 32
CMP_STRIDE = 16
CMP_HIDDEN = 2 * HEAD_DIM
SEL_BLOCK = 64
N_SELECT = 16
WINDOW = 512
TQ = 256
GLA_HEADS = 4
GLA_DK = 128
GLA_DV = 256
GLA_RANK = 16
GLA_TAU = 16.0
D_FF = 5632
CONV_W = 3
PLE_DIM = 256
EPS = 1e-6
NEG_INF = -1e30
FORCE_SCORE = 1e9
REMOVED_SCORE = -3e38
SCALE = HEAD_DIM ** -0.5

NSA_WIDTH = NSA_HEADS * HEAD_DIM
GLA_WIDTH = GLA_HEADS * GLA_DV
KV_WIDTH = 2 * NSA_KV_HEADS * HEAD_DIM
KV_SLAB = 2 * HEAD_DIM
Q_SLAB = NSA_GROUP * HEAD_DIM
GATE_W = 3 * NSA_HEADS

LANES = 128
SUBLANES = 8
VMEM_LIMIT = 56 * 1024 * 1024

COL_Q = 0
COL_GV = COL_Q + NSA_WIDTH
COL_GR = COL_GV + GLA_WIDTH
COL_KVC = COL_GR + GLA_WIDTH
COL_KVS = COL_KVC + KV_WIDTH
COL_KVW = COL_KVS + KV_WIDTH
COL_GQ = COL_KVW + KV_WIDTH
COL_GK = COL_GQ + GLA_HEADS * GLA_DK
COL_MISC = COL_GK + GLA_HEADS * GLA_DK
D_IN_PAD = COL_MISC + LANES
MISC_GA = GATE_W


def _cparams(sem):
    return pltpu.CompilerParams(dimension_semantics=sem, vmem_limit_bytes=VMEM_LIMIT)


def _rms(x, g):
    return x * lax.rsqrt(jnp.mean(x * x, axis=-1, keepdims=True) + EPS) * g


def _gelu_tanh(x):
    return 0.5 * x * (1.0 + jnp.tanh(np.float32(np.sqrt(2.0 / np.pi)) * (x + 0.044715 * (x * x * x))))


def _sigmoid(x):
    return 1.0 / (1.0 + jnp.exp(-x))


def _dot(a, b):
    return jnp.dot(a, b, preferred_element_type=F32)


def _dot_nt(a, b):
    return lax.dot_general(a, b, (((1,), (1,)), ((), ())), preferred_element_type=F32)


def _dot_tn(a, b):
    return lax.dot_general(a, b, (((0,), (0,)), ((), ())), preferred_element_type=F32)


def _norm_matmul_kernel(x_ref, g_ref, w_ref, o_ref, xn_ref):
    @pl.when(pl.program_id(1) == 0)
    def _():
        xn_ref[...] = _rms(x_ref[...], g_ref[...]).astype(BF16)

    o_ref[...] = _dot(xn_ref[...], w_ref[...])


def _norm_matmul(x, g, w, tm, tn):
    m, d = x.shape
    n = w.shape[1]
    return pl.pallas_call(
        _norm_matmul_kernel,
        grid=(m // tm, n // tn),
        in_specs=[pl.BlockSpec((tm, d), lambda i, j: (i, 0)),
                  pl.BlockSpec((1, d), lambda i, j: (0, 0)),
                  pl.BlockSpec((d, tn), lambda i, j: (0, j))],
        out_specs=pl.BlockSpec((tm, tn), lambda i, j: (i, j)),
        out_shape=jax.ShapeDtypeStruct((m, n), F32),
        scratch_shapes=[pltpu.VMEM((tm, d), BF16)],
        compiler_params=_cparams(("parallel", "arbitrary")),
        name="in_proj",
    )(x, g, w)


def _cmp_pe_hidden(pe_ref, w1_ref):
    span = CMP_BLOCK // CMP_STRIDE
    acc = jnp.zeros((SUBLANES, 2 * CMP_HIDDEN), F32)
    for j in range(span):
        for s in range(CMP_STRIDE):
            row = jnp.broadcast_to(pe_ref[pl.ds(j * CMP_STRIDE + s, 1), :], (SUBLANES, KV_SLAB)).astype(BF16)
            acc = acc + _dot(row, w1_ref[s][:, j * 2 * CMP_HIDDEN:(j + 1) * 2 * CMP_HIDDEN])
    return acc[0:1, :]


def _cmp_prompt_kernel(x_ref, w1_ref, w2_ref, pe_ref, o_ref, *, n16):
    acc = jnp.zeros((n16, 4 * CMP_HIDDEN), F32)
    for s in range(CMP_STRIDE):
        xs = x_ref[pl.ds(s, n16, stride=CMP_STRIDE), :].astype(BF16)
        acc = acc + _dot(xs, w1_ref[s])
    a = acc[:, :2 * CMP_HIDDEN]
    b_next = pltpu.roll(acc[:, 2 * CMP_HIDDEN:], n16 - 1, axis=0)
    hid = a + b_next + _cmp_pe_hidden(pe_ref, w1_ref)
    o_ref[...] = _dot(_gelu_tanh(hid).astype(BF16), w2_ref[...])


def _cmp_weights(w_k1, w_k2, pe_k, w_v1, w_v2, pe_v):
    z = jnp.zeros((CMP_BLOCK, HEAD_DIM, CMP_HIDDEN), F32)
    wl = jnp.concatenate([jnp.concatenate([w_k1, z], axis=2), jnp.concatenate([z, w_v1], axis=2)], axis=1)
    w1 = jnp.concatenate([wl[:CMP_STRIDE], wl[CMP_STRIDE:]], axis=2).astype(BF16)
    z2 = jnp.zeros((CMP_HIDDEN, HEAD_DIM), F32)
    w2 = jnp.concatenate([jnp.concatenate([w_k2, z2], axis=1), jnp.concatenate([z2, w_v2], axis=1)], axis=0)
    pe = jnp.concatenate([pe_k, pe_v], axis=1)
    return w1, w2.astype(BF16), pe


def _cmp_prompt(proj, b, t, w1, w2, pe):
    n16 = t // CMP_STRIDE
    col0 = COL_KVC // KV_SLAB
    return pl.pallas_call(
        functools.partial(_cmp_prompt_kernel, n16=n16),
        grid=(b, NSA_KV_HEADS),
        in_specs=[pl.BlockSpec((t, KV_SLAB), lambda i, n: (i, col0 + n)),
                  pl.BlockSpec(w1.shape, lambda i, n: (0, 0, 0)),
                  pl.BlockSpec(w2.shape, lambda i, n: (0, 0)),
                  pl.BlockSpec(pe.shape, lambda i, n: (0, 0))],
        out_specs=pl.BlockSpec((None, None, n16, KV_SLAB), lambda i, n: (i, n, 0, 0)),
        out_shape=jax.ShapeDtypeStruct((b, NSA_KV_HEADS, n16, KV_SLAB), F32),
        compiler_params=_cparams(("parallel", "parallel")),
        name="cmp_prompt",
    )(proj, w1, w2, pe)


def _overlap_matrix_t(n_sel_rows, n_cmp_cols):
    j = lax.broadcasted_iota(jnp.int32, (n_sel_rows, n_cmp_cols), 0)
    c = lax.broadcasted_iota(jnp.int32, (n_sel_rows, n_cmp_cols), 1)
    per_sel = SEL_BLOCK // CMP_STRIDE
    ov = jnp.zeros((n_sel_rows, n_cmp_cols), F32)
    for n in range(CMP_BLOCK // CMP_STRIDE):
        ov = ov + jnp.where((c + n >= per_sel * j) & (c + n < per_sel * (j + 1)), 1.0, 0.0)
    return ov


def _top_blocks(imp, n_pick):
    rows, nsel = imp.shape
    j = lax.broadcasted_iota(jnp.int32, (rows, nsel), 1).astype(F32)
    mask = jnp.zeros((rows, nsel), F32)
    picks = []
    for _ in range(n_pick):
        m = jnp.max(imp, axis=-1, keepdims=True)
        jmin = jnp.min(jnp.where(imp == m, j, float(nsel)), axis=-1, keepdims=True)
        hit = j == jmin
        mask = jnp.where(hit, 1.0, mask)
        imp = jnp.where(hit, REMOVED_SCORE, imp)
        picks.append(jmin.astype(jnp.int32))
    return mask, picks


def _slope_col(slopes_ref, n, rows_per_head, rows):
    g = lax.broadcasted_iota(jnp.int32, (rows, 1), 0) // rows_per_head
    col = jnp.zeros((rows, 1), F32)
    for gg in range(NSA_GROUP):
        col = jnp.where(g == gg, slopes_ref[n * NSA_GROUP + gg], col)
    return col


def _pick_head_gates(gl, n):
    out = jnp.zeros((gl.shape[0], 3 * NSA_GROUP), F32)
    for nn in range(NSA_KV_HEADS):
        out = jnp.where(n == nn, gl[:, nn * 3 * NSA_GROUP:(nn + 1) * 3 * NSA_GROUP], out)
    return _sigmoid(out)


MASK_BIAS = -131072.0
M_FLOOR = -65536.0
POS_HI, POS_LO = HEAD_DIM, HEAD_DIM + 3
KEY_TILE = TQ


def _slope_features():
    h = np.arange(1, NSA_HEADS + 1, dtype=np.float32)
    slopes = (2.0 ** (-8.0 * h / NSA_HEADS)).astype(np.float32)
    tab = np.zeros((NSA_HEADS, LANES), np.float32)
    rest = slopes
    for c in range(3):
        piece = rest.astype(BF16).astype(np.float32)
        tab[:, POS_HI + c] = piece
        tab[:, POS_LO + c] = piece
        rest = rest - piece
    assert not rest.any()
    return jnp.asarray(tab)


def _kv_prep_kernel(kvs_ref, kvw_ref, ks_ref, vs_ref, kw_ref, vw_ref, *, t):
    lane = lax.broadcasted_iota(jnp.int32, (t, LANES), 1)
    pos = lax.broadcasted_iota(jnp.int32, (t, LANES), 0)
    hi = ((pos // SEL_BLOCK) * SEL_BLOCK).astype(F32)
    lo = (pos % SEL_BLOCK).astype(F32)
    feat = jnp.where((lane >= POS_HI) & (lane < POS_HI + 3), hi,
                     jnp.where((lane >= POS_LO) & (lane < POS_LO + 3), lo, 0.0))
    ones_col = jnp.where(lane == 0, 1.0, 0.0)
    for src, k_out, v_out in ((kvs_ref, ks_ref, vs_ref), (kvw_ref, kw_ref, vw_ref)):
        x = src[...]
        k_out[:, 0:LANES] = jnp.where(lane < HEAD_DIM, x, feat).astype(BF16)
        v_out[...] = jnp.where(lane >= HEAD_DIM, x, ones_col).astype(BF16)
    ks_ref[:, LANES:2 * LANES] = jnp.where(lane == pos // SEL_BLOCK, 1.0, 0.0).astype(BF16)


def _kv_prep(proj, b, t):
    spec_in = lambda col: pl.BlockSpec((t, KV_SLAB), lambda i, n: (i, col // KV_SLAB + n))
    spec_out = lambda w: pl.BlockSpec((None, None, t, w), lambda i, n: (i, n, 0, 0))
    shape = lambda w: jax.ShapeDtypeStruct((b, NSA_KV_HEADS, t, w), BF16)
    return pl.pallas_call(
        functools.partial(_kv_prep_kernel, t=t),
        grid=(b, NSA_KV_HEADS),
        in_specs=[spec_in(COL_KVS), spec_in(COL_KVW)],
        out_specs=[spec_out(2 * LANES), spec_out(LANES), spec_out(LANES), spec_out(LANES)],
        out_shape=[shape(2 * LANES), shape(LANES), shape(LANES), shape(LANES)],
        compiler_params=_cparams(("parallel", "parallel")),
        name="kv_prep",
    )(proj, proj)


def _nsa_prompt_kernel(slopes_ref, q_ref, kcv_ref, ks_ref, vs_ref, kw_ref, vw_ref, misc_ref, sfeat_ref, o_ref,
                       *, n16, n_sel):
    n = pl.program_id(1)
    qb = pl.program_id(2)
    rows = NSA_GROUP * TQ
    qf = q_ref[...] * SCALE
    q = jnp.concatenate([qf[:, g * HEAD_DIM:(g + 1) * HEAD_DIM] for g in range(NSA_GROUP)], axis=0).astype(BF16)
    tok = lax.broadcasted_iota(jnp.int32, (rows, 1), 0) % TQ
    qpos = qb * TQ + tok
    slope = _slope_col(slopes_ref, n, TQ, rows)

    kcv = kcv_ref[...]
    kc = kcv[:, :HEAD_DIM].astype(BF16)
    vc = kcv[:, HEAD_DIM:].astype(BF16)
    s = _dot_nt(q, kc)
    end = lax.broadcasted_iota(jnp.int32, (1, n16), 1) * CMP_STRIDE + (CMP_BLOCK - 1)
    dist = (qpos - end).astype(F32)
    valid = (dist >= 0) & (end < n16 * CMP_STRIDE)
    s = jnp.where(valid, s - slope * dist, NEG_INF)
    e = jnp.exp(s - jnp.max(s, axis=-1, keepdims=True))
    p = jnp.where(valid, e / jnp.sum(e, axis=-1, keepdims=True), 0.0)
    o_c = _dot(p.astype(BF16), vc)
    p_grp = p[0:TQ]
    for g in range(1, NSA_GROUP):
        p_grp = p_grp + p[g * TQ:(g + 1) * TQ]
    imp = lax.dot_general(_overlap_matrix_t(n_sel, n16), p_grp, (((1,), (1,)), ((), ())),
                          preferred_element_type=F32, precision=HIGHEST)

    jj = lax.broadcasted_iota(jnp.int32, (n_sel, TQ), 0)
    cur = (qb * TQ + lax.broadcasted_iota(jnp.int32, (1, TQ), 1)) // SEL_BLOCK
    forced = (jj == 0) | (jj == cur) | (jj == cur - 1)
    imp = jnp.where(forced, FORCE_SCORE, jnp.where(jj > cur, -FORCE_SCORE, imp))
    beaten = jnp.zeros((n_sel, TQ), F32)
    for jp in range(n_sel):
        other = imp[jp:jp + 1, :]
        beats = (other > imp) | ((other == imp) & (jj > jp))
        beaten = beaten + jnp.where(beats, 1.0, 0.0)
    not_picked = jnp.where(beaten < min(N_SELECT, n_sel), 0.0, 1.0)
    if n_sel < LANES:
        not_picked = jnp.concatenate([not_picked, jnp.zeros((LANES - n_sel, TQ), F32)], axis=0)
    q_bias = jnp.transpose(not_picked) * MASK_BIAS

    q_main = jnp.concatenate(
        [jnp.concatenate([qf[:, g * HEAD_DIM:(g + 1) * HEAD_DIM],
                          jnp.broadcast_to(sfeat_ref[pl.ds(n * NSA_GROUP + g, 1), HEAD_DIM:LANES],
                                           (TQ, LANES - HEAD_DIM))], axis=1)
         for g in range(NSA_GROUP)], axis=0)
    groups = [slice(g * TQ, (g + 1) * TQ) for g in range(NSA_GROUP)]
    q_win = [q_main[r].astype(BF16) for r in groups]
    q_sel = [jnp.concatenate([q_main[r], q_bias], axis=1).astype(BF16) for r in groups]
    tpos = qb * TQ + lax.broadcasted_iota(jnp.int32, (TQ, 1), 0)

    def flash_step(carry, scores, v):
        out = []
        for (m, acc), sc in zip(carry, scores):
            m_new = jnp.maximum(m, jnp.max(sc, axis=-1, keepdims=True))
            pr = jnp.exp(sc - m_new).astype(BF16)
            out.append((m_new, jnp.exp(m - m_new) * acc + _dot(pr, v)))
        return tuple(out)

    def finish(carry):
        return [acc[:, HEAD_DIM:] / acc[:, 0:1] for _, acc in carry]

    init = tuple((jnp.full((TQ, 1), M_FLOOR, F32), jnp.zeros((TQ, LANES), F32)) for _ in groups)

    def sel_scores(r, keep=None):
        k = ks_ref[r, :]
        sc = [_dot_nt(qg, k) for qg in q_sel]
        return sc if keep is None else [jnp.where(keep, s_, NEG_INF) for s_ in sc]

    def key_rows(kt):
        return pl.ds(pl.multiple_of(kt * KEY_TILE, KEY_TILE), KEY_TILE)

    def sel_body(kt, carry):
        state, sc = carry
        sc_next = sel_scores(key_rows(kt + 1))
        return flash_step(state, sc, vs_ref[key_rows(kt), :]), tuple(sc_next)

    n_full = (qb * TQ) // KEY_TILE
    carry, sc = lax.fori_loop(0, n_full, sel_body, (init, tuple(sel_scores(key_rows(0)))))
    kpos = n_full * KEY_TILE + lax.broadcasted_iota(jnp.int32, (1, KEY_TILE), 1)
    sc = [jnp.where(kpos <= tpos, s_, NEG_INF) for s_ in sc]
    o_s = finish(flash_step(carry, sc, vs_ref[key_rows(n_full), :]))

    def win_scores(r, keep=None):
        k = kw_ref[r, :]
        sc = [_dot_nt(qg, k) for qg in q_win]
        return sc if keep is None else [jnp.where(keep, s_, NEG_INF) for s_ in sc]

    def tile_rows(kt):
        return pl.ds(pl.multiple_of(kt * TQ, TQ), TQ)

    lane_q = lax.broadcasted_iota(jnp.int32, (1, TQ), 1)
    kt0 = jnp.maximum(qb - WINDOW // TQ, 0)
    dist = tpos - (kt0 * TQ + lane_q)
    ok = (dist >= 0) & (dist < WINDOW) & (jnp.full((1, TQ), kt0, jnp.int32) < qb)
    carry = flash_step(init, win_scores(tile_rows(kt0), ok), vw_ref[tile_rows(kt0), :])

    def win_body(kt, carry):
        return flash_step(carry, win_scores(tile_rows(kt)), vw_ref[tile_rows(kt), :])

    carry = lax.fori_loop(kt0 + 1, qb, win_body, carry)
    o_w = finish(flash_step(carry, win_scores(tile_rows(qb), qb * TQ + lane_q <= tpos), vw_ref[tile_rows(qb), :]))

    gates = _pick_head_gates(misc_ref[:, 0:GATE_W], n)
    outs = []
    for g, r in enumerate(groups):
        outs.append(gates[:, 3 * g:3 * g + 1] * o_c[r] + gates[:, 3 * g + 1:3 * g + 2] * o_s[g]
                    + gates[:, 3 * g + 2:3 * g + 3] * o_w[g])
    o_ref[...] = jnp.concatenate(outs, axis=1)


def _nsa_prompt(proj, kcv, slopes, b, t):
    n16 = t // CMP_STRIDE
    n_sel = t // SEL_BLOCK
    nqb = t // TQ
    assert n_sel <= LANES and t % KEY_TILE == 0
    k_sel, v_sel, k_win, v_win = _kv_prep(proj, b, t)
    seq = lambda w: pl.BlockSpec((None, None, t, w), lambda i, n, qb, sl: (i, n, 0, 0))
    grid_spec = pltpu.PrefetchScalarGridSpec(
        num_scalar_prefetch=1,
        grid=(b, NSA_KV_HEADS, nqb),
        in_specs=[pl.BlockSpec((TQ, Q_SLAB), lambda i, n, qb, sl: (i * nqb + qb, COL_Q // Q_SLAB + n)),
                  pl.BlockSpec((None, None, n16, KV_SLAB), lambda i, n, qb, sl: (i, n, 0, 0)),
                  seq(2 * LANES), seq(LANES), seq(LANES), seq(LANES),
                  pl.BlockSpec((TQ, LANES), lambda i, n, qb, sl: (i * nqb + qb, COL_MISC // LANES)),
                  pl.BlockSpec((NSA_HEADS, LANES), lambda i, n, qb, sl: (0, 0))],
        out_specs=pl.BlockSpec((TQ, Q_SLAB), lambda i, n, qb, sl: (i * nqb + qb, n)),
    )
    return pl.pallas_call(
        functools.partial(_nsa_prompt_kernel, n16=n16, n_sel=n_sel),
        grid_spec=grid_spec,
        out_shape=jax.ShapeDtypeStruct((b * t, NSA_WIDTH), F32),
        compiler_params=_cparams(("parallel", "parallel", "arbitrary")),
        name="nsa_prompt",
    )(slopes, proj, kcv, k_sel, v_sel, k_win, v_win, proj, _slope_features())


def _gla_kernel(q_ref, k_ref, v_ref, r_ref, misc_ref, wa_ref, ba_ref, gout_ref, s0_ref, o_ref, sfin_ref, state_ref,
                *, chunk, sub, valid_rows):
    c = pl.program_id(1)

    @pl.when(c == 0)
    def _():
        state_ref[...] = s0_ref[...]

    ga = misc_ref[:, MISC_GA:MISC_GA + GLA_RANK]
    x = jnp.dot(ga, wa_ref[...], preferred_element_type=F32, precision=HIGHEST) + ba_ref[...]
    lg = (jnp.minimum(x, 0.0) - jnp.log1p(jnp.exp(-jnp.abs(x)))) / GLA_TAU
    row = lax.broadcasted_iota(jnp.int32, (chunk, 1), 0)
    if valid_rows < chunk:
        lg = jnp.where(row < valid_rows, lg, 0.0)
    tri = (lax.broadcasted_iota(jnp.int32, (chunk, chunk), 0)
           >= lax.broadcasted_iota(jnp.int32, (chunk, chunk), 1)).astype(F32)
    cum_all = jnp.dot(tri, lg, preferred_element_type=F32, precision=HIGHEST)
    for h in range(GLA_HEADS):
        kcols = slice(h * GLA_DK, (h + 1) * GLA_DK)
        vcols = slice(h * GLA_DV, (h + 1) * GLA_DV)
        o, new_state = _gla_head_chunk(q_ref[:, kcols] * (GLA_DK ** -0.5), k_ref[:, kcols], v_ref[:, vcols],
                                       cum_all[:, kcols], state_ref[h], chunk, sub)
        state_ref[h] = new_state
        rg = r_ref[:, vcols]
        o_ref[:, vcols] = _rms(o, gout_ref[...]) * (rg * _sigmoid(rg))

    @pl.when(c == pl.num_programs(1) - 1)
    def _():
        sfin_ref[...] = state_ref[...]


def _gla_head_chunk(q, k, v, cum, state, chunk, sub):
    vb = v.astype(BF16)
    inter = _dot((q * jnp.exp(cum)).astype(BF16), state.astype(BF16))

    outs = []
    for i in range(chunk // sub):
        r0 = i * sub
        qi, ki, ci, vi = q[r0:r0 + sub], k[r0:r0 + sub], cum[r0:r0 + sub], v[r0:r0 + sub]
        o_i = inter[r0:r0 + sub]
        if i > 0:
            anchor = cum[r0:r0 + 1]
            qd = (qi * jnp.exp(ci - anchor)).astype(BF16)
            kd = (k[0:r0] * jnp.exp(anchor - cum[0:r0])).astype(BF16)
            o_i = o_i + _dot(_dot_nt(qd, kd).astype(BF16), vb[0:r0])
        trow = lax.broadcasted_iota(jnp.int32, (sub, 1), 0)
        for s_ in range(sub):
            w = jnp.sum(qi * (ki[s_:s_ + 1] * jnp.exp(jnp.minimum(ci - ci[s_:s_ + 1], 0.0))), axis=-1, keepdims=True)
            o_i = o_i + jnp.where(trow >= s_, w, 0.0) * vi[s_:s_ + 1]
        outs.append(o_i)
    o = jnp.concatenate(outs, axis=0) if len(outs) > 1 else outs[0]

    last = cum[chunk - 1:chunk]
    kdec = (k * jnp.exp(last - cum)).astype(BF16)
    decay_col = jnp.transpose(jnp.broadcast_to(jnp.exp(last), (SUBLANES, GLA_DK)))[:, 0:1]
    return o, decay_col * state + _dot_tn(kdec, vb)


def _gla(proj, w_a2, b_a, g_out, s0, b, t, chunk, sub, valid_rows):
    nck = t // chunk
    kw = GLA_HEADS * GLA_DK
    state_spec = pl.BlockSpec((None, GLA_HEADS, GLA_DK, GLA_DV), lambda i, c: (i, 0, 0, 0))
    return pl.pallas_call(
        functools.partial(_gla_kernel, chunk=chunk, sub=sub, valid_rows=valid_rows),
        grid=(b, nck),
        in_specs=[pl.BlockSpec((chunk, kw), lambda i, c: (i * nck + c, COL_GQ // kw)),
                  pl.BlockSpec((chunk, kw), lambda i, c: (i * nck + c, COL_GK // kw)),
                  pl.BlockSpec((chunk, GLA_WIDTH), lambda i, c: (i * nck + c, COL_GV // GLA_WIDTH)),
                  pl.BlockSpec((chunk, GLA_WIDTH), lambda i, c: (i * nck + c, COL_GR // GLA_WIDTH)),
                  pl.BlockSpec((chunk, LANES), lambda i, c: (i * nck + c, COL_MISC // LANES)),
                  pl.BlockSpec((GLA_RANK, kw), lambda i, c: (0, 0)),
                  pl.BlockSpec((1, kw), lambda i, c: (0, 0)),
                  pl.BlockSpec((1, GLA_DV), lambda i, c: (0, 0)),
                  state_spec],
        out_specs=[pl.BlockSpec((chunk, GLA_WIDTH), lambda i, c: (i * nck + c, 0)), state_spec],
        out_shape=[jax.ShapeDtypeStruct((b * t, GLA_WIDTH), F32),
                   jax.ShapeDtypeStruct((b, GLA_HEADS, GLA_DK, GLA_DV), F32)],
        scratch_shapes=[pltpu.VMEM((GLA_HEADS, GLA_DK, GLA_DV), F32)],
        compiler_params=_cparams(("parallel", "arbitrary")),
        name="gla",
    )(proj, proj, proj, proj, proj, w_a2, b_a, g_out, s0)


def _out_proj_kernel(on_ref, og_ref, x_ref, g_ref, w_ref, o_ref, a_ref):
    @pl.when(pl.program_id(1) == 0)
    def _():
        a_ref[:, :NSA_WIDTH] = _rms(on_ref[...], g_ref[...]).astype(BF16)
        a_ref[:, NSA_WIDTH:] = og_ref[...].astype(BF16)

    o_ref[...] = x_ref[...] + _dot(a_ref[...], w_ref[...])


def _out_proj(o_nsa, o_gla, x, g_nsa, w_out, tm, tn):
    m = x.shape[0]
    return pl.pallas_call(
        _out_proj_kernel,
        grid=(m // tm, D_MODEL // tn),
        in_specs=[pl.BlockSpec((tm, NSA_WIDTH), lambda i, j: (i, 0)),
                  pl.BlockSpec((tm, GLA_WIDTH), lambda i, j: (i, 0)),
                  pl.BlockSpec((tm, tn), lambda i, j: (i, j)),
                  pl.BlockSpec((1, NSA_WIDTH), lambda i, j: (0, 0)),
                  pl.BlockSpec((NSA_WIDTH + GLA_WIDTH, tn), lambda i, j: (0, j))],
        out_specs=pl.BlockSpec((tm, tn), lambda i, j: (i, j)),
        out_shape=jax.ShapeDtypeStruct((m, D_MODEL), F32),
        scratch_shapes=[pltpu.VMEM((tm, NSA_WIDTH + GLA_WIDTH), BF16)],
        compiler_params=_cparams(("parallel", "arbitrary")),
        name="out_proj",
    )(o_nsa, o_gla, x, g_nsa, w_out)


def _ffn_kernel(h_ref, g_ref, wa_ref, wg_ref, cw_ref, cb_ref, wd_ref, p1_ref, p2_ref, o_ref, tail_ref,
                n2_ref, acc_ref, carry_ref, *, tm, tf, seq_rows):
    i = pl.program_id(0)
    j = pl.program_id(1)

    @pl.when(j == 0)
    def _():
        n2_ref[...] = _rms(h_ref[...], g_ref[...]).astype(BF16)
        acc_ref[...] = jnp.zeros_like(acc_ref)

    n2 = n2_ref[...]
    a = _dot(n2, wa_ref[...])
    gate = _dot(n2, wg_ref[...])
    row = lax.broadcasted_iota(jnp.int32, (tm, 1), 0)
    r1 = pltpu.roll(a, 1, axis=0)
    r2 = pltpu.roll(a, 2, axis=0)
    if seq_rows >= tm:
        cols = pl.ds(pl.multiple_of(j * tf, tf), tf)
        first = (i % (seq_rows // tm)) == 0
        prev = jnp.where(first, p2_ref[...], carry_ref[:, cols])
        a1 = jnp.where(row == 0, prev[1:2], r1)
        a2 = jnp.where(row == 0, prev[0:1], jnp.where(row == 1, prev[1:2], r2))
        carry_ref[:, cols] = a[tm - 2:tm]
    else:
        t = row % seq_rows
        a1 = jnp.where(t == 0, p1_ref[...], r1)
        a2 = jnp.where(t < 2, p2_ref[...], r2)
    cw = cw_ref[...]
    conv = cb_ref[...] + a2 * cw[0:1] + a1 * cw[1:2] + a * cw[2:3]
    y = (_gelu_tanh(conv) * gate).astype(BF16)
    acc_ref[...] += _dot(y, wd_ref[...])
    tail_ref[...] = a[tm - tail_ref.shape[0]:tm]

    @pl.when(j == pl.num_programs(1) - 1)
    def _():
        o_ref[...] = h_ref[...] + acc_ref[...]


def _ffn(h, g_ffn, w_up, conv_w, conv_b, w_down, p1, p2, tm, tf, seq_rows):
    m = h.shape[0]
    nj = D_FF // tf
    if seq_rows >= tm:
        tiles_per_seq = seq_rows // tm
        p1_spec = pl.BlockSpec((None, CONV_W - 1, tf), lambda i, j: (i // tiles_per_seq, 0, j))
        p2_spec = pl.BlockSpec((None, CONV_W - 1, tf), lambda i, j: (i // tiles_per_seq, 0, j))
    else:
        p1_spec = pl.BlockSpec((tm, tf), lambda i, j: (i, j))
        p2_spec = pl.BlockSpec((tm, tf), lambda i, j: (i, j))
    if seq_rows >= tm:
        tail_spec = pl.BlockSpec((None, SUBLANES, tf), lambda i, j: (i, 0, j))
        tail_shape = jax.ShapeDtypeStruct((m // tm, SUBLANES, D_FF), F32)
    else:
        tail_spec = pl.BlockSpec((tm, tf), lambda i, j: (i, j))
        tail_shape = jax.ShapeDtypeStruct((m, D_FF), F32)
    return pl.pallas_call(
        functools.partial(_ffn_kernel, tm=tm, tf=tf, seq_rows=seq_rows),
        grid=(m // tm, nj),
        in_specs=[pl.BlockSpec((tm, D_MODEL), lambda i, j: (i, 0)),
                  pl.BlockSpec((1, D_MODEL), lambda i, j: (0, 0)),
                  pl.BlockSpec((D_MODEL, tf), lambda i, j: (0, j)),
                  pl.BlockSpec((D_MODEL, tf), lambda i, j: (0, nj + j)),
                  pl.BlockSpec((CONV_W, tf), lambda i, j: (0, j)),
                  pl.BlockSpec((1, tf), lambda i, j: (0, j)),
                  pl.BlockSpec((tf, D_MODEL), lambda i, j: (j, 0)),
                  p1_spec, p2_spec],
        out_specs=[pl.BlockSpec((tm, D_MODEL), lambda i, j: (i, 0)),
                   tail_spec],
        out_shape=[jax.ShapeDtypeStruct((m, D_MODEL), F32), tail_shape],
        scratch_shapes=[pltpu.VMEM((tm, D_MODEL), BF16), pltpu.VMEM((tm, D_MODEL), F32),
                        pltpu.VMEM((CONV_W - 1, D_FF), F32)],
        compiler_params=_cparams(("arbitrary", "arbitrary")),
        name="conv_ffn",
    )(h, g_ffn, w_up, w_up, conv_w, conv_b, w_down, p1, p2)


def _ple_kernel(h_ref, p_ref, wg_ref, wp_ref, gp_ref, gf_ref, o_ref, *, final_norm):
    h = h_ref[...]
    gate = _sigmoid(_dot(h.astype(BF16), wg_ref[...]))
    pe = _rms(_dot(p_ref[...].astype(BF16), wp_ref[...]), gp_ref[...])
    h = h + gate * pe
    o_ref[...] = _rms(h, gf_ref[...]) if final_norm else h


def _ple(h, p, w_gate, w_proj, g_ple, g_final, tm, final_norm):
    m = h.shape[0]
    return pl.pallas_call(
        functools.partial(_ple_kernel, final_norm=final_norm),
        grid=(m // tm,),
        in_specs=[pl.BlockSpec((tm, D_MODEL), lambda i: (i, 0)),
                  pl.BlockSpec((tm, PLE_DIM), lambda i: (i, 0)),
                  pl.BlockSpec((D_MODEL, D_MODEL), lambda i: (0, 0)),
                  pl.BlockSpec((PLE_DIM, D_MODEL), lambda i: (0, 0)),
                  pl.BlockSpec((1, D_MODEL), lambda i: (0, 0)),
                  pl.BlockSpec((1, D_MODEL), lambda i: (0, 0))],
        out_specs=pl.BlockSpec((tm, D_MODEL), lambda i: (i, 0)),
        out_shape=jax.ShapeDtypeStruct((m, D_MODEL), F32),
        compiler_params=_cparams(("parallel",)),
        name="ple_norm",
    )(h, p, w_gate, w_proj, g_ple, g_final)


CMP_PAGES = 16
CHUNKS_PER_PAGE = PAGE_SIZE // CMP_STRIDE


def _cmp_sample_kernel(pt_ref, *refs):
    page_refs = refs[:CMP_PAGES]
    perm_ref, w1_ref, w1p_ref, w2_ref, pe_ref, o_ref, carry_ref, x_ref = refs[CMP_PAGES:]
    grp = pl.program_id(1)
    nck = CMP_PAGES * CHUNKS_PER_PAGE

    @pl.when(grp == 0)
    def _():
        carry_ref[...] = jnp.zeros_like(carry_ref)

    perm = perm_ref[...]
    for k, r in enumerate(page_refs):
        xp = _dot_nt(perm, r[...].astype(BF16))
        for s in range(CMP_STRIDE):
            for n in range(NSA_KV_HEADS):
                x_ref[s, n, k * CHUNKS_PER_PAGE:(k + 1) * CHUNKS_PER_PAGE, :] = (
                    xp[s * CHUNKS_PER_PAGE:(s + 1) * CHUNKS_PER_PAGE, n * KV_SLAB:(n + 1) * KV_SLAB])
    acc = jnp.zeros((NSA_KV_HEADS * nck, 4 * CMP_HIDDEN), F32)
    for sp in range(CMP_STRIDE // 2):
        xs = jnp.concatenate([x_ref[2 * sp].reshape(NSA_KV_HEADS * nck, KV_SLAB),
                              x_ref[2 * sp + 1].reshape(NSA_KV_HEADS * nck, KV_SLAB)], axis=1)
        acc = acc + _dot(xs.astype(BF16), w1p_ref[sp])
    a = acc[:, :2 * CMP_HIDDEN]
    b = acc[:, 2 * CMP_HIDDEN:]
    a_prev = pltpu.roll(a, 1, axis=0)
    row = lax.broadcasted_iota(jnp.int32, (NSA_KV_HEADS * nck, 1), 0)
    for n in range(NSA_KV_HEADS):
        a_prev = jnp.where(row == n * nck, carry_ref[n:n + 1, :], a_prev)
        carry_ref[n:n + 1, :] = a[(n + 1) * nck - 1:(n + 1) * nck]
    hid = a_prev + b + _cmp_pe_hidden(pe_ref, w1_ref)
    res = _dot(_gelu_tanh(hid).astype(BF16), w2_ref[...])
    for n in range(NSA_KV_HEADS):
        o_ref[n] = res[n * nck:(n + 1) * nck]


def _cmp_sample(cache_c, page_table_flat, w1, w2, pe, b, n_pages):
    n_grp = n_pages // CMP_PAGES
    nck = CMP_PAGES * CHUNKS_PER_PAGE

    def page_spec(k):
        return pl.BlockSpec((None, KV_WIDTH, PAGE_SIZE),
                            lambda i, g, pt: (pt[i * n_pages + g * CMP_PAGES + k], 0, 0))

    r = np.arange(PAGE_SIZE)
    perm_np = np.zeros((PAGE_SIZE, PAGE_SIZE), np.float32)
    perm_np[(r % CMP_STRIDE) * CHUNKS_PER_PAGE + r // CMP_STRIDE, r] = 1.0
    perm = jnp.asarray(perm_np, dtype=BF16)
    w1p = w1.reshape(CMP_STRIDE // 2, 2 * KV_SLAB, 4 * CMP_HIDDEN)

    grid_spec = pltpu.PrefetchScalarGridSpec(
        num_scalar_prefetch=1,
        grid=(b, n_grp),
        in_specs=[page_spec(k) for k in range(CMP_PAGES)]
        + [pl.BlockSpec(perm.shape, lambda i, g, pt: (0, 0)),
           pl.BlockSpec(w1.shape, lambda i, g, pt: (0, 0, 0)),
           pl.BlockSpec(w1p.shape, lambda i, g, pt: (0, 0, 0)),
           pl.BlockSpec(w2.shape, lambda i, g, pt: (0, 0)),
           pl.BlockSpec(pe.shape, lambda i, g, pt: (0, 0))],
        out_specs=pl.BlockSpec((None, NSA_KV_HEADS, nck, KV_SLAB), lambda i, g, pt: (i, 0, g, 0)),
        scratch_shapes=[pltpu.VMEM((SUBLANES, 2 * CMP_HIDDEN), F32),
                        pltpu.VMEM((CMP_STRIDE, NSA_KV_HEADS, nck, KV_SLAB), F32)],
    )
    return pl.pallas_call(
        _cmp_sample_kernel,
        grid_spec=grid_spec,
        out_shape=jax.ShapeDtypeStruct((b, NSA_KV_HEADS, n_pages * CHUNKS_PER_PAGE, KV_SLAB), F32),
        compiler_params=_cparams(("parallel", "arbitrary")),
        name="cmp_sample",
    )(page_table_flat, *([cache_c] * CMP_PAGES), perm, w1, w1p, w2, pe)


def _sel_sample_kernel(slopes_ref, q_ref, kcv_ref, ov_ref, oc_ref, idx_ref, *, past_len, n_sel_pad, t_valid):
    for n in range(NSA_KV_HEADS):
        oc, idx = _sel_sample_head(n, slopes_ref, q_ref[:, n * Q_SLAB:(n + 1) * Q_SLAB] * SCALE, kcv_ref[n],
                                   ov_ref[...], past_len, n_sel_pad, t_valid)
        oc_ref[n] = oc
        idx_ref[n] = idx


def _sel_sample_head(n, slopes_ref, qf, kcv, overlap, past_len, n_sel_pad, t_valid):
    tp = SUBLANES
    rows = NSA_GROUP * tp
    n_rows_c = past_len // CMP_STRIDE
    nb_past = past_len // SEL_BLOCK
    q = jnp.concatenate([qf[:, g * HEAD_DIM:(g + 1) * HEAD_DIM] for g in range(NSA_GROUP)], axis=0).astype(BF16)
    tok = lax.broadcasted_iota(jnp.int32, (rows, 1), 0) % tp
    qpos = past_len + tok
    slope = _slope_col(slopes_ref, n, tp, rows)
    kc = kcv[:, :HEAD_DIM].astype(BF16)
    vc = kcv[:, HEAD_DIM:].astype(BF16)
    s = _dot_nt(q, kc)
    cp = lax.broadcasted_iota(jnp.int32, (1, n_rows_c), 1)
    end = (cp - 1) * CMP_STRIDE + (CMP_BLOCK - 1)
    dist = (qpos - end).astype(F32)
    valid = (cp >= 1) & (dist >= 0)
    s = jnp.where(valid, s - slope * dist, NEG_INF)
    e = jnp.exp(s - jnp.max(s, axis=-1, keepdims=True))
    p = jnp.where(valid, e / jnp.sum(e, axis=-1, keepdims=True), 0.0)
    o_c = _dot(p.astype(BF16), vc)
    oc = jnp.concatenate([o_c[g * tp:(g + 1) * tp] for g in range(NSA_GROUP)], axis=1)
    p_grp = p[0:tp]
    for g in range(1, NSA_GROUP):
        p_grp = p_grp + p[g * tp:(g + 1) * tp]
    imp = jnp.dot(p_grp, overlap, preferred_element_type=F32, precision=HIGHEST)
    n_tail = -(-t_valid // SEL_BLOCK)
    n_sel = nb_past + n_tail
    j = lax.broadcasted_iota(jnp.int32, (tp, n_sel_pad), 1)
    cur = qpos[0:tp] // SEL_BLOCK
    forced = (j == 0) | (j == cur) | (j == cur - 1)
    imp = jnp.where(forced, FORCE_SCORE, jnp.where(j > cur, -FORCE_SCORE, imp))
    imp = jnp.where(j < n_sel, imp, REMOVED_SCORE)
    _, picks = _top_blocks(imp, min(N_SELECT, n_sel))
    kcol = lax.broadcasted_iota(jnp.int32, (tp, N_SELECT), 1)
    idx = jnp.zeros((tp, N_SELECT), jnp.int32)
    for kk, pk in enumerate(picks):
        idx = jnp.where(kcol == kk, pk, idx)
    return oc, idx


def _sel_sample(proj8, kcv, slopes, b, past_len, t_valid):
    n_rows_c = past_len // CMP_STRIDE
    nb_past = past_len // SEL_BLOCK
    n_sel_pad = -(-(nb_past + 1) // LANES) * LANES
    c = np.arange(n_rows_c)[:, None] - 1
    j = np.arange(n_sel_pad)[None, :]
    per_sel = SEL_BLOCK // CMP_STRIDE
    ov = sum(((c + k >= per_sel * j) & (c + k < per_sel * (j + 1))) for k in range(CMP_BLOCK // CMP_STRIDE))
    overlap = jnp.asarray(np.where(c >= 0, ov, 0).astype(np.float32))
    grid_spec = pltpu.PrefetchScalarGridSpec(
        num_scalar_prefetch=1,
        grid=(b,),
        in_specs=[pl.BlockSpec((SUBLANES, NSA_WIDTH), lambda i, sl: (i, COL_Q // NSA_WIDTH)),
                  pl.BlockSpec((None, NSA_KV_HEADS, n_rows_c, KV_SLAB), lambda i, sl: (i, 0, 0, 0)),
                  pl.BlockSpec(overlap.shape, lambda i, sl: (0, 0))],
        out_specs=[pl.BlockSpec((None, NSA_KV_HEADS, SUBLANES, Q_SLAB), lambda i, sl: (i, 0, 0, 0)),
                   pl.BlockSpec((None, NSA_KV_HEADS, SUBLANES, N_SELECT), lambda i, sl: (i, 0, 0, 0))],
    )
    return pl.pallas_call(
        functools.partial(_sel_sample_kernel, past_len=past_len, n_sel_pad=n_sel_pad, t_valid=t_valid),
        grid_spec=grid_spec,
        out_shape=[jax.ShapeDtypeStruct((b, NSA_KV_HEADS, SUBLANES, Q_SLAB), F32),
                   jax.ShapeDtypeStruct((b, NSA_KV_HEADS, SUBLANES, N_SELECT), jnp.int32)],
        compiler_params=_cparams(("parallel",)),
        name="sel_sample",
    )(slopes, proj8, kcv, overlap)


def _nsa_sample_kernel(idx_ref, pt_ref, slopes_ref, *refs, past_len, t_valid):
    blk_refs = refs[:t_valid * N_SELECT]
    q_ref, tail_ref, wnew_ref, wcache_ref, oc_ref, misc_ref, o_ref = refs[t_valid * N_SELECT:]
    i = pl.program_id(0)
    n = pl.program_id(1)
    slope = _slope_col(slopes_ref, n, 1, SUBLANES)
    gates = _pick_head_gates(misc_ref[:, 0:GATE_W], n)
    tl = tail_ref[...]
    wn = wnew_ref[...]
    wct = wcache_ref[...]
    win_k = wct[:HEAD_DIM].astype(BF16)
    win_v = wct[HEAD_DIM:].astype(BF16)
    rows_out = []
    for t in range(t_valid):
        picked = [blk_refs[t * N_SELECT + kk] for kk in range(N_SELECT)]
        base = ((i * NSA_KV_HEADS + n) * SUBLANES + t) * N_SELECT
        jbs = [idx_ref[base + kk] for kk in range(N_SELECT)]
        rows_out.append(_nsa_sample_token(t, q_ref[t:t + 1, :] * SCALE, slope, picked, jbs, tl, wn, win_k, win_v,
                                          oc_ref[t:t + 1, :], gates[t:t + 1, :], past_len))
    rows_out.append(jnp.zeros((SUBLANES - t_valid, Q_SLAB), F32))
    o_ref[...] = jnp.concatenate(rows_out, axis=0)


def _nsa_sample_token(t, qrow, slope, picked, jbs, tl, wn, win_k, win_v, oc_row, gates, past_len):
    nb_past = past_len // SEL_BLOCK
    g8 = SUBLANES
    q = jnp.concatenate([qrow[:, g * HEAD_DIM:(g + 1) * HEAD_DIM] for g in range(NSA_GROUP)]
                        + [jnp.zeros((g8 - NSA_GROUP, HEAD_DIM), F32)], axis=0).astype(BF16)
    qpos = past_len + t

    per_page = PAGE_SIZE // SEL_BLOCK
    k_all = jnp.concatenate([r[0:HEAD_DIM, :] for r in picked], axis=1).astype(BF16)
    v_all = jnp.concatenate([r[HEAD_DIM:KV_SLAB, :] for r in picked], axis=1).astype(BF16)
    jb_row = jnp.concatenate([jnp.full((1, PAGE_SIZE), jb, jnp.int32) for jb in jbs], axis=1)
    tail_count = jnp.zeros((), jnp.int32)
    for jb in jbs:
        tail_count = tail_count + (jb >= nb_past).astype(jnp.int32)
    lane = lax.broadcasted_iota(jnp.int32, (1, N_SELECT * PAGE_SIZE), 1) % PAGE_SIZE
    kpos = (jb_row // per_page) * PAGE_SIZE + lane
    dist = (qpos - kpos).astype(F32)
    ok = (dist >= 0) & (kpos // SEL_BLOCK == jb_row) & (jb_row < nb_past)
    sc = _dot(q, k_all) - slope * dist
    parts = [(jnp.where(ok, sc, NEG_INF), ok, v_all, True)]
    lane8 = lax.broadcasted_iota(jnp.int32, (1, SUBLANES), 1)
    dist = (t - lane8).astype(F32)
    ok = (dist >= 0) & (jnp.full((1, SUBLANES), tail_count, jnp.int32) > 0)
    sc = _dot_nt(q, tl[:, :HEAD_DIM].astype(BF16)) - slope * dist
    parts.append((jnp.where(ok, sc, NEG_INF), ok, tl[:, HEAD_DIM:].astype(BF16), False))

    def softmax_av(parts):
        m = parts[0][0].max(axis=-1, keepdims=True)
        for sc, _, _, _ in parts[1:]:
            m = jnp.maximum(m, sc.max(axis=-1, keepdims=True))
        l = jnp.zeros((g8, 1), F32)
        acc = jnp.zeros((g8, HEAD_DIM), F32)
        for sc, ok, v, v_transposed in parts:
            pr = jnp.where(ok, jnp.exp(sc - m), 0.0)
            l = l + pr.sum(axis=-1, keepdims=True)
            acc = acc + (_dot_nt(pr.astype(BF16), v) if v_transposed else _dot(pr.astype(BF16), v))
        return acc / l

    o_s = softmax_av(parts)

    buf_len = win_k.shape[1]
    lane_w = lax.broadcasted_iota(jnp.int32, (1, buf_len), 1)
    dist_c = (qpos - (past_len - buf_len + lane_w)).astype(F32)
    ok_c = (dist_c >= 0) & (dist_c < WINDOW)
    sc_c = jnp.where(ok_c, _dot(q, win_k) - slope * dist_c, NEG_INF)
    dist_n = (t - lane8).astype(F32)
    ok_n = (dist_n >= 0) & (dist_n < WINDOW)
    sc_n = jnp.where(ok_n, _dot_nt(q, wn[:, :HEAD_DIM].astype(BF16)) - slope * dist_n, NEG_INF)
    o_w = softmax_av([(sc_c, ok_c, win_v, True), (sc_n, ok_n, wn[:, HEAD_DIM:].astype(BF16), False)])

    outs = []
    for g in range(NSA_GROUP):
        outs.append(gates[:, 3 * g:3 * g + 1] * oc_row[:, g * HEAD_DIM:(g + 1) * HEAD_DIM]
                    + gates[:, 3 * g + 1:3 * g + 2] * o_s[g:g + 1]
                    + gates[:, 3 * g + 2:3 * g + 3] * o_w[g:g + 1])
    return jnp.concatenate(outs, axis=1)


def _nsa_sample(proj8, cache_s, cache_w, o_c, idx_flat, page_table_flat, slopes, b, t_valid, past_len):
    n_pages = past_len // PAGE_SIZE
    nb_past = past_len // SEL_BLOCK
    per_page = PAGE_SIZE // SEL_BLOCK
    buf_len = cache_w.shape[-1]

    def blk_spec(t, kk):
        def imap(i, n, idx, pt, sl):
            jb = jnp.minimum(idx[((i * NSA_KV_HEADS + n) * SUBLANES + t) * N_SELECT + kk], nb_past - 1)
            return (pt[i * n_pages + jb // per_page], n, 0, 0)
        return pl.BlockSpec((None, None, KV_SLAB, PAGE_SIZE), imap)

    n_blk = t_valid * N_SELECT
    grid_spec = pltpu.PrefetchScalarGridSpec(
        num_scalar_prefetch=3,
        grid=(b, NSA_KV_HEADS),
        in_specs=[blk_spec(t, kk) for t in range(t_valid) for kk in range(N_SELECT)]
        + [pl.BlockSpec((SUBLANES, Q_SLAB), lambda i, n, *_: (i, COL_Q // Q_SLAB + n)),
           pl.BlockSpec((SUBLANES, KV_SLAB), lambda i, n, *_: (i, COL_KVS // KV_SLAB + n)),
           pl.BlockSpec((SUBLANES, KV_SLAB), lambda i, n, *_: (i, COL_KVW // KV_SLAB + n)),
           pl.BlockSpec((None, None, KV_SLAB, buf_len), lambda i, n, *_: (i, n, 0, 0)),
           pl.BlockSpec((None, None, SUBLANES, Q_SLAB), lambda i, n, *_: (i, n, 0, 0)),
           pl.BlockSpec((SUBLANES, LANES), lambda i, n, *_: (i, COL_MISC // LANES))],
        out_specs=pl.BlockSpec((SUBLANES, Q_SLAB), lambda i, n, *_: (i, n)),
    )
    return pl.pallas_call(
        functools.partial(_nsa_sample_kernel, past_len=past_len, t_valid=t_valid),
        grid_spec=grid_spec,
        out_shape=jax.ShapeDtypeStruct((b * SUBLANES, NSA_WIDTH), F32),
        compiler_params=_cparams(("parallel", "arbitrary")),
        name="nsa_sample",
    )(idx_flat, page_table_flat, slopes, *([cache_s] * n_blk), proj8, proj8, proj8, cache_w, o_c, proj8)


def _alibi_slopes():
    h = np.arange(1, NSA_HEADS + 1, dtype=np.float32)
    return jnp.asarray(2.0 ** (-8.0 * h / NSA_HEADS), dtype=F32)


def _permute_w_in(w_in):
    offs = np.cumsum([0, NSA_WIDTH, KV_WIDTH, KV_WIDTH, KV_WIDTH, GATE_W, GLA_HEADS * GLA_DK, GLA_HEADS * GLA_DK,
                      GLA_WIDTH, GLA_WIDTH, GLA_RANK])
    piece = [w_in[:, offs[k]:offs[k + 1]] for k in range(10)]
    q, kvc, kvs, kvw, gl, gq, gk, gv, gr, ga = piece
    pad = jnp.zeros((w_in.shape[0], LANES - GATE_W - GLA_RANK), w_in.dtype)
    return jnp.concatenate([q, gv, gr, kvc, kvs, kvw, gq, gk, gl, ga, pad], axis=1).astype(BF16)


def _row_tile(m, pref):
    return pref if m % pref == 0 else m


def _dense_tail(h_in, o_nsa, o_gla, p_emb, wts, conv_p1, conv_p2, seq_rows, last_layer, tm):
    (g_nsa, w_out, g_ffn, w_up, conv_w, conv_b, w_down, w_ple_proj, g_ple, w_ple_gate, g_final) = wts
    h = _out_proj(o_nsa, o_gla, h_in, g_nsa, w_out, tm, D_MODEL // 2)
    h, tails = _ffn(h, g_ffn, w_up, conv_w, conv_b, w_down, conv_p1, conv_p2, tm, 512, seq_rows)
    y = _ple(h, p_emb, w_ple_gate, w_ple_proj, g_ple, g_final, tm, last_layer)
    return y, tails


def kernel(x_prompt, x_sample, p_prompt, p_sample, cache_cmp_kv, cache_sel_kv, cache_win_kv, state_gla, state_ffn_conv, page_table, g_attn, w_in, w_cmp_k1, w_cmp_k2, pe_cmp_k, w_cmp_v1, w_cmp_v2, pe_cmp_v, w_gla_a2, b_gla_a, g_nsa_out, g_gla_out, w_out, g_ffn, w_up, conv_w, conv_b, w_down, w_ple_proj, g_ple, w_ple_gate, g_final):
    depth = w_in.shape[0]
    bp, tp, _ = x_prompt.shape
    bs, ts, _ = x_sample.shape
    n_pages = page_table.shape[1]
    past_len = n_pages * PAGE_SIZE
    n_pool = cache_cmp_kv.shape[1]
    assert tp % TQ == 0 and tp >= WINDOW and ts <= SUBLANES and ts <= SEL_BLOCK and ts >= CONV_W - 1
    assert n_pages % CMP_PAGES == 0 and tp // SEL_BLOCK >= N_SELECT
    slopes = _alibi_slopes()
    pt_flat = page_table.reshape(-1).astype(jnp.int32)
    kv_shape = (NSA_KV_HEADS, 2, HEAD_DIM)

    hp = x_prompt.reshape(bp * tp, D_MODEL)
    hs = x_sample.reshape(bs * ts, D_MODEL)
    new_p = [[] for _ in range(5)]
    new_s = [[] for _ in range(5)]
    tm_p = _row_tile(bp * tp, 512)
    tm_s = bs * ts
    for i in range(depth):
        last = i == depth - 1
        w_in_p = _permute_w_in(w_in[i])
        g_a = g_attn[i].reshape(1, D_MODEL)
        w1c, w2c, pec = _cmp_weights(w_cmp_k1[i], w_cmp_k2[i], pe_cmp_k[i], w_cmp_v1[i], w_cmp_v2[i], pe_cmp_v[i])
        b_a = b_gla_a[i].reshape(1, -1)
        g_go = g_gla_out[i].reshape(1, GLA_DV)
        wts = (g_nsa_out[i].reshape(1, -1), w_out[i].astype(BF16), g_ffn[i].reshape(1, -1), w_up[i].astype(BF16),
               conv_w[i], conv_b[i].reshape(1, -1), w_down[i].astype(BF16), w_ple_proj[i].astype(BF16),
               g_ple[i].reshape(1, -1), w_ple_gate[i].astype(BF16), g_final.reshape(1, -1))

        proj = _norm_matmul(hp, g_a, w_in_p, tm_p, D_IN_PAD // 5)
        kcv = _cmp_prompt(proj, bp, tp, w1c, w2c, pec)
        o_nsa = _nsa_prompt(proj, kcv, slopes, bp, tp)
        s0 = jnp.zeros((bp, GLA_HEADS, GLA_DK, GLA_DV), F32)
        o_gla, s_new = _gla(proj, w_gla_a2[i], b_a, g_go, s0, bp, tp, 64, 16, 64)
        zbuf = jnp.zeros((bp, CONV_W - 1, D_FF), F32)
        hp, tails = _dense_tail(hp, o_nsa, o_gla, p_prompt[i].reshape(bp * tp, PLE_DIM), wts, zbuf, zbuf, tp, last, tm_p)
        proj3 = proj.reshape(bp, tp, D_IN_PAD)
        new_p[0].append(proj3[:, :, COL_KVC:COL_KVC + KV_WIDTH].reshape((bp, tp) + kv_shape))
        new_p[1].append(proj3[:, :, COL_KVS:COL_KVS + KV_WIDTH].reshape((bp, tp) + kv_shape))
        new_p[2].append(proj3[:, tp - WINDOW:, COL_KVW:COL_KVW + KV_WIDTH].reshape((bp, WINDOW) + kv_shape))
        new_p[3].append(s_new)
        tiles_per_seq = tp // tm_p
        new_p[4].append(tails.reshape(bp, tiles_per_seq, SUBLANES, D_FF)[:, -1, SUBLANES - (CONV_W - 1):, :])

        proj_s = _norm_matmul(hs, g_a, w_in_p, tm_s, D_IN_PAD // 5)
        proj8 = jnp.pad(proj_s.reshape(bs, ts, D_IN_PAD), ((0, 0), (0, SUBLANES - ts), (0, 0))).reshape(bs * SUBLANES, D_IN_PAD)
        cache_c = jnp.transpose(cache_cmp_kv[i], (0, 2, 3, 4, 1)).reshape(n_pool, KV_WIDTH, PAGE_SIZE)
        cache_s = jnp.transpose(cache_sel_kv[i], (0, 2, 3, 4, 1)).reshape(n_pool, NSA_KV_HEADS, KV_SLAB, PAGE_SIZE)
        cache_w = jnp.transpose(cache_win_kv[i], (0, 2, 3, 4, 1)).reshape(bs, NSA_KV_HEADS, KV_SLAB, -1)
        kcv_s = _cmp_sample(cache_c, pt_flat, w1c, w2c, pec, bs, n_pages)
        o_c, idx = _sel_sample(proj8, kcv_s, slopes, bs, past_len, ts)
        o_nsa_s = _nsa_sample(proj8, cache_s, cache_w, o_c, idx.reshape(-1), pt_flat, slopes, bs, ts, past_len)
        o_nsa_s = o_nsa_s.reshape(bs, SUBLANES, NSA_WIDTH)[:, :ts].reshape(bs * ts, NSA_WIDTH)
        o_gla_s, s_new_s = _gla(proj8, w_gla_a2[i], b_a, g_go, state_gla[i].astype(F32), bs, SUBLANES, SUBLANES, SUBLANES, ts)
        o_gla_s = o_gla_s.reshape(bs, SUBLANES, GLA_WIDTH)[:, :ts].reshape(bs * ts, GLA_WIDTH)
        buf = state_ffn_conv[i]
        zrow = jnp.zeros((bs, ts - 1, D_FF), F32)
        p1 = jnp.concatenate([buf[:, 1:2], zrow], axis=1).reshape(bs * ts, D_FF)
        p2 = jnp.concatenate([buf, jnp.zeros((bs, ts - 2, D_FF), F32)], axis=1).reshape(bs * ts, D_FF)
        hs, tails_s = _dense_tail(hs, o_nsa_s, o_gla_s, p_sample[i].reshape(bs * ts, PLE_DIM), wts, p1, p2, ts, last, tm_s)
        ps3 = proj_s.reshape(bs, ts, D_IN_PAD)
        new_s[0].append(ps3[:, :, COL_KVC:COL_KVC + KV_WIDTH].reshape((bs, ts) + kv_shape))
        new_s[1].append(ps3[:, :, COL_KVS:COL_KVS + KV_WIDTH].reshape((bs, ts) + kv_shape))
        new_s[2].append(ps3[:, :, COL_KVW:COL_KVW + KV_WIDTH].reshape((bs, ts) + kv_shape))
        new_s[3].append(s_new_s)
        new_s[4].append(tails_s.reshape(bs, ts, D_FF)[:, ts - (CONV_W - 1):, :])

    y_prompt = hp.reshape(bp, tp, D_MODEL)
    y_sample = hs.reshape(bs, ts, D_MODEL)
    cmp_p, sel_p, win_p, gla_p, conv_p = [jnp.stack(l) for l in new_p]
    cmp_s, sel_s, win_s, gla_s, conv_s = [jnp.stack(l) for l in new_s]
    return (y_prompt, y_sample, cmp_p, sel_p, win_p, gla_p, conv_p, cmp_s, sel_s, win_s, gla_s, conv_s)
```

```python
import functools

import numpy as np
import jax
import jax.numpy as jnp
from jax import lax
from jax.experimental import pallas as pl
from jax.experimental.pallas import tpu as pltpu

F32 = jnp.float32
BF16 = jnp.bfloat16
HIGHEST = lax.Precision.HIGHEST

D_MODEL = 2048
PAGE_SIZE = 128
NSA_HEADS = 16
NSA_KV_HEADS = 4
NSA_GROUP = NSA_HEADS // NSA_KV_HEADS
HEAD_DIM = 64
CMP_BLOCK = 32
CMP_STRIDE = 16
CMP_HIDDEN = 2 * HEAD_DIM
SEL_BLOCK = 64
N_SELECT = 16
WINDOW = 512
TQ = 256
GLA_HEADS = 4
GLA_DK = 128
GLA_DV = 256
GLA_RANK = 16
GLA_TAU = 16.0
D_FF = 5632
CONV_W = 3
PLE_DIM = 256
EPS = 1e-6
NEG_INF = -1e30
FORCE_SCORE = 1e9
REMOVED_SCORE = -3e38
SCALE = HEAD_DIM ** -0.5

NSA_WIDTH = NSA_HEADS * HEAD_DIM
GLA_WIDTH = GLA_HEADS * GLA_DV
KV_WIDTH = 2 * NSA_KV_HEADS * HEAD_DIM
KV_SLAB = 2 * HEAD_DIM
Q_SLAB = NSA_GROUP * HEAD_DIM
GATE_W = 3 * NSA_HEADS

LANES = 128
SUBLANES = 8
VMEM_LIMIT = 56 * 1024 * 1024

COL_Q = 0
COL_GV = COL_Q + NSA_WIDTH
COL_GR = COL_GV + GLA_WIDTH
COL_KVC = COL_GR + GLA_WIDTH
COL_KVS = COL_KVC + KV_WIDTH
COL_KVW = COL_KVS + KV_WIDTH
COL_GQ = COL_KVW + KV_WIDTH
COL_GK = COL_GQ + GLA_HEADS * GLA_DK
COL_MISC = COL_GK + GLA_HEADS * GLA_DK
D_IN_PAD = COL_MISC + LANES
MISC_GA = GATE_W


def _cparams(sem):
    return pltpu.CompilerParams(dimension_semantics=sem, vmem_limit_bytes=VMEM_LIMIT)


def _rms(x, g):
    return x * lax.rsqrt(jnp.mean(x * x, axis=-1, keepdims=True) + EPS) * g


def _gelu_tanh(x):
    return 0.5 * x * (1.0 + jnp.tanh(np.float32(np.sqrt(2.0 / np.pi)) * (x + 0.044715 * (x * x * x))))


def _sigmoid(x):
    return 1.0 / (1.0 + jnp.exp(-x))


def _dot(a, b):
    return jnp.dot(a, b, preferred_element_type=F32)


def _dot_nt(a, b):
    return lax.dot_general(a, b, (((1,), (1,)), ((), ())), preferred_element_type=F32)


def _dot_tn(a, b):
    return lax.dot_general(a, b, (((0,), (0,)), ((), ())), preferred_element_type=F32)


def _norm_matmul_kernel(x_ref, g_ref, w_ref, o_ref, xn_ref):
    @pl.when(pl.program_id(1) == 0)
    def _():
        xn_ref[...] = _rms(x_ref[...], g_ref[...]).astype(BF16)

    o_ref[...] = _dot(xn_ref[...], w_ref[...])


def _norm_matmul(x, g, w, tm, tn):
    m, d = x.shape
    n = w.shape[1]
    return pl.pallas_call(
        _norm_matmul_kernel,
        grid=(m // tm, n // tn),
        in_specs=[pl.BlockSpec((tm, d), lambda i, j: (i, 0)),
                  pl.BlockSpec((1, d), lambda i, j: (0, 0)),
                  pl.BlockSpec((d, tn), lambda i, j: (0, j))],
        out_specs=pl.BlockSpec((tm, tn), lambda i, j: (i, j)),
        out_shape=jax.ShapeDtypeStruct((m, n), F32),
        scratch_shapes=[pltpu.VMEM((tm, d), BF16)],
        compiler_params=_cparams(("parallel", "arbitrary")),
        name="in_proj",
    )(x, g, w)


def _cmp_pe_hidden(pe_ref, w1_ref):
    span = CMP_BLOCK // CMP_STRIDE
    acc = jnp.zeros((SUBLANES, 2 * CMP_HIDDEN), F32)
    for j in range(span):
        for s in range(CMP_STRIDE):
            row = jnp.broadcast_to(pe_ref[pl.ds(j * CMP_STRIDE + s, 1), :], (SUBLANES, KV_SLAB)).astype(BF16)
            acc = acc + _dot(row, w1_ref[s][:, j * 2 * CMP_HIDDEN:(j + 1) * 2 * CMP_HIDDEN])
    return acc[0:1, :]


def _cmp_prompt_kernel(x_ref, w1_ref, w2_ref, pe_ref, o_ref, *, n16):
    acc = jnp.zeros((n16, 4 * CMP_HIDDEN), F32)
    for s in range(CMP_STRIDE):
        xs = x_ref[pl.ds(s, n16, stride=CMP_STRIDE), :].astype(BF16)
        acc = acc + _dot(xs, w1_ref[s])
    a = acc[:, :2 * CMP_HIDDEN]
    b_next = pltpu.roll(acc[:, 2 * CMP_HIDDEN:], n16 - 1, axis=0)
    hid = a + b_next + _cmp_pe_hidden(pe_ref, w1_ref)
    o_ref[...] = _dot(_gelu_tanh(hid).astype(BF16), w2_ref[...])


def _cmp_weights(w_k1, w_k2, pe_k, w_v1, w_v2, pe_v):
    z = jnp.zeros((CMP_BLOCK, HEAD_DIM, CMP_HIDDEN), F32)
    wl = jnp.concatenate([jnp.concatenate([w_k1, z], axis=2), jnp.concatenate([z, w_v1], axis=2)], axis=1)
    w1 = jnp.concatenate([wl[:CMP_STRIDE], wl[CMP_STRIDE:]], axis=2).astype(BF16)
    z2 = jnp.zeros((CMP_HIDDEN, HEAD_DIM), F32)
    w2 = jnp.concatenate([jnp.concatenate([w_k2, z2], axis=1), jnp.concatenate([z2, w_v2], axis=1)], axis=0)
    pe = jnp.concatenate([pe_k, pe_v], axis=1)
    return w1, w2.astype(BF16), pe


def _cmp_prompt(proj, b, t, w1, w2, pe):
    n16 = t // CMP_STRIDE
    col0 = COL_KVC // KV_SLAB
    return pl.pallas_call(
        functools.partial(_cmp_prompt_kernel, n16=n16),
        grid=(b, NSA_KV_HEADS),
        in_specs=[pl.BlockSpec((t, KV_SLAB), lambda i, n: (i, col0 + n)),
                  pl.BlockSpec(w1.shape, lambda i, n: (0, 0, 0)),
                  pl.BlockSpec(w2.shape, lambda i, n: (0, 0)),
                  pl.BlockSpec(pe.shape, lambda i, n: (0, 0))],
        out_specs=pl.BlockSpec((None, None, n16, KV_SLAB), lambda i, n: (i, n, 0, 0)),
        out_shape=jax.ShapeDtypeStruct((b, NSA_KV_HEADS, n16, KV_SLAB), F32),
        compiler_params=_cparams(("parallel", "parallel")),
        name="cmp_prompt",
    )(proj, w1, w2, pe)


def _overlap_counts(c, j):
    per_sel = SEL_BLOCK // CMP_STRIDE
    ov = sum(((c + k >= per_sel * j) & (c + k < per_sel * (j + 1))) for k in range(CMP_BLOCK // CMP_STRIDE))
    return np.where(c >= 0, ov, 0).astype(np.float32)


def _top_blocks(imp, n_pick):
    rows, nsel = imp.shape
    j = lax.broadcasted_iota(jnp.int32, (rows, nsel), 1).astype(F32)
    mask = jnp.zeros((rows, nsel), F32)
    picks = []
    for _ in range(n_pick):
        m = jnp.max(imp, axis=-1, keepdims=True)
        jmin = jnp.min(jnp.where(imp == m, j, float(nsel)), axis=-1, keepdims=True)
        hit = j == jmin
        mask = jnp.where(hit, 1.0, mask)
        imp = jnp.where(hit, REMOVED_SCORE, imp)
        picks.append(jmin.astype(jnp.int32))
    return mask, picks


def _slope_col(slopes_ref, n, rows_per_head, rows):
    g = lax.broadcasted_iota(jnp.int32, (rows, 1), 0) // rows_per_head
    col = jnp.zeros((rows, 1), F32)
    for gg in range(NSA_GROUP):
        col = jnp.where(g == gg, slopes_ref[n * NSA_GROUP + gg], col)
    return col


def _pick_head_gates(gl, n):
    out = jnp.zeros((gl.shape[0], 3 * NSA_GROUP), F32)
    for nn in range(NSA_KV_HEADS):
        out = jnp.where(n == nn, gl[:, nn * 3 * NSA_GROUP:(nn + 1) * 3 * NSA_GROUP], out)
    return _sigmoid(out)


MASK_BIAS = -131072.0
M_FLOOR = -65536.0
POS_HI, POS_LO = HEAD_DIM, HEAD_DIM + 3
KEY_TILE = TQ


def _slope_features():
    h = np.arange(1, NSA_HEADS + 1, dtype=np.float32)
    slopes = (2.0 ** (-8.0 * h / NSA_HEADS)).astype(np.float32)
    tab = np.zeros((NSA_HEADS, LANES), np.float32)
    rest = slopes
    for c in range(3):
        piece = rest.astype(BF16).astype(np.float32)
        tab[:, POS_HI + c] = piece
        tab[:, POS_LO + c] = piece
        rest = rest - piece
    assert not rest.any()
    return jnp.asarray(tab)


def _kv_prep_kernel(kvs_ref, kvw_ref, ks_ref, vs_ref, kw_ref, vw_ref, *, t):
    lane = lax.broadcasted_iota(jnp.int32, (t, LANES), 1)
    pos = lax.broadcasted_iota(jnp.int32, (t, LANES), 0)
    hi = ((pos // SEL_BLOCK) * SEL_BLOCK).astype(F32)
    lo = (pos % SEL_BLOCK).astype(F32)
    feat = jnp.where((lane >= POS_HI) & (lane < POS_HI + 3), hi,
                     jnp.where((lane >= POS_LO) & (lane < POS_LO + 3), lo, 0.0))
    ones_col = jnp.where(lane == HEAD_DIM, 1.0, 0.0)
    for src, k_out, v_out in ((kvs_ref, ks_ref, vs_ref), (kvw_ref, kw_ref, vw_ref)):
        x = src[...]
        k_out[:, 0:LANES] = jnp.where(lane < HEAD_DIM, x, feat).astype(BF16)
        v_out[...] = jnp.where(lane < HEAD_DIM, pltpu.roll(x, HEAD_DIM, axis=1), ones_col).astype(BF16)
    ks_ref[:, LANES:2 * LANES] = jnp.where(lane == pos // SEL_BLOCK, 1.0, 0.0).astype(BF16)


def _kv_prep(proj, b, t):
    spec_in = lambda col: pl.BlockSpec((t, KV_SLAB), lambda i, n: (i, col // KV_SLAB + n))
    spec_out = lambda w: pl.BlockSpec((None, None, t, w), lambda i, n: (i, n, 0, 0))
    shape = lambda w: jax.ShapeDtypeStruct((b, NSA_KV_HEADS, t, w), BF16)
    return pl.pallas_call(
        functools.partial(_kv_prep_kernel, t=t),
        grid=(b, NSA_KV_HEADS),
        in_specs=[spec_in(COL_KVS), spec_in(COL_KVW)],
        out_specs=[spec_out(2 * LANES), spec_out(LANES), spec_out(LANES), spec_out(LANES)],
        out_shape=[shape(2 * LANES), shape(LANES), shape(LANES), shape(LANES)],
        compiler_params=_cparams(("parallel", "parallel")),
        name="kv_prep",
    )(proj, proj)


def _nsa_prompt_kernel(slopes_ref, q_ref, kcv_ref, ks_ref, vs_ref, kw_ref, vw_ref, misc_ref, sfeat_ref, ovt_ref, o_ref,
                       *, n16, n_sel):
    n = pl.program_id(1)
    qb = pl.program_id(2)
    rows = NSA_GROUP * TQ
    qf = q_ref[...] * SCALE
    q = jnp.concatenate([qf[:, g * HEAD_DIM:(g + 1) * HEAD_DIM] for g in range(NSA_GROUP)], axis=0).astype(BF16)
    tok = lax.broadcasted_iota(jnp.int32, (rows, 1), 0) % TQ
    qpos = qb * TQ + tok
    slope = _slope_col(slopes_ref, n, TQ, rows)

    kcv = kcv_ref[...]
    kc = kcv[:, :HEAD_DIM].astype(BF16)
    vc = kcv[:, HEAD_DIM:].astype(BF16)
    s = _dot_nt(q, kc)
    end = lax.broadcasted_iota(jnp.int32, (1, n16), 1) * CMP_STRIDE + (CMP_BLOCK - 1)
    dist = (qpos - end).astype(F32)
    valid = (dist >= 0) & (end < n16 * CMP_STRIDE)
    s = jnp.where(valid, s - slope * dist, NEG_INF)
    e = jnp.exp(s - jnp.max(s, axis=-1, keepdims=True))
    p = jnp.where(valid, e / jnp.sum(e, axis=-1, keepdims=True), 0.0)
    o_c = _dot(p.astype(BF16), vc)
    p_grp = p[0:TQ]
    for g in range(1, NSA_GROUP):
        p_grp = p_grp + p[g * TQ:(g + 1) * TQ]
    imp = lax.dot_general(ovt_ref[...], p_grp, (((1,), (1,)), ((), ())),
                          preferred_element_type=F32, precision=HIGHEST)

    jj = lax.broadcasted_iota(jnp.int32, (n_sel, TQ), 0)
    cur = (qb * TQ + lax.broadcasted_iota(jnp.int32, (1, TQ), 1)) // SEL_BLOCK
    forced = (jj == 0) | (jj == cur) | (jj == cur - 1)
    imp = jnp.where(forced, FORCE_SCORE, jnp.where(jj > cur, -FORCE_SCORE, imp))
    beaten = jnp.zeros((n_sel, TQ), F32)
    for jp in range(n_sel):
        other = imp[jp:jp + 1, :]
        beats = (other > imp) | ((other == imp) & (jj > jp))
        beaten = beaten + jnp.where(beats, 1.0, 0.0)
    not_picked = jnp.where(beaten < min(N_SELECT, n_sel), 0.0, 1.0)
    if n_sel < LANES:
        not_picked = jnp.concatenate([not_picked, jnp.zeros((LANES - n_sel, TQ), F32)], axis=0)
    q_bias = jnp.transpose(not_picked) * MASK_BIAS

    q_main = jnp.concatenate(
        [jnp.concatenate([qf[:, g * HEAD_DIM:(g + 1) * HEAD_DIM],
                          jnp.broadcast_to(sfeat_ref[pl.ds(n * NSA_GROUP + g, 1), HEAD_DIM:LANES],
                                           (TQ, LANES - HEAD_DIM))], axis=1)
         for g in range(NSA_GROUP)], axis=0)
    groups = [slice(g * TQ, (g + 1) * TQ) for g in range(NSA_GROUP)]
    q_win = [q_main[r].astype(BF16) for r in groups]
    q_sel = [jnp.concatenate([q_main[r], q_bias], axis=1).astype(BF16) for r in groups]
    tpos = qb * TQ + lax.broadcasted_iota(jnp.int32, (TQ, 1), 0)

    def flash_step(carry, scores, v):
        out = []
        for (m, acc), sc in zip(carry, scores):
            m_new = jnp.maximum(m, jnp.max(sc, axis=-1, keepdims=True))
            pr = jnp.exp(sc - m_new).astype(BF16)
            out.append((m_new, jnp.exp(m - m_new) * acc + _dot(pr, v)))
        return tuple(out)

    def finish(carry):
        return [(acc[:, :HEAD_DIM], acc[:, HEAD_DIM:HEAD_DIM + 1]) for _, acc in carry]

    init = tuple((jnp.full((TQ, 1), M_FLOOR, F32), jnp.zeros((TQ, LANES), F32)) for _ in groups)

    def sel_scores(r, keep=None):
        k = ks_ref[r, :]
        sc = [_dot_nt(qg, k) for qg in q_sel]
        return sc if keep is None else [jnp.where(keep, s_, NEG_INF) for s_ in sc]

    def key_rows(kt):
        return pl.ds(pl.multiple_of(kt * KEY_TILE, KEY_TILE), KEY_TILE)

    def sel_body(kt, carry):
        state, sc = carry
        sc_next = sel_scores(key_rows(kt + 1))
        return flash_step(state, sc, vs_ref[key_rows(kt), :]), tuple(sc_next)

    n_full = (qb * TQ) // KEY_TILE
    carry, sc = lax.fori_loop(0, n_full, sel_body, (init, tuple(sel_scores(key_rows(0)))))
    kpos = n_full * KEY_TILE + lax.broadcasted_iota(jnp.int32, (1, KEY_TILE), 1)
    sc = [jnp.where(kpos <= tpos, s_, NEG_INF) for s_ in sc]
    o_s = finish(flash_step(carry, sc, vs_ref[key_rows(n_full), :]))

    def win_scores(r, keep=None):
        k = kw_ref[r, :]
        sc = [_dot_nt(qg, k) for qg in q_win]
        return sc if keep is None else [jnp.where(keep, s_, NEG_INF) for s_ in sc]

    def tile_rows(kt):
        return pl.ds(pl.multiple_of(kt * TQ, TQ), TQ)

    lane_q = lax.broadcasted_iota(jnp.int32, (1, TQ), 1)
    kt0 = jnp.maximum(qb - WINDOW // TQ, 0)
    dist = tpos - (kt0 * TQ + lane_q)
    ok = (dist >= 0) & (dist < WINDOW) & (jnp.full((1, TQ), kt0, jnp.int32) < qb)
    carry = flash_step(init, win_scores(tile_rows(kt0), ok), vw_ref[tile_rows(kt0), :])

    def win_body(kt, carry):
        return flash_step(carry, win_scores(tile_rows(kt)), vw_ref[tile_rows(kt), :])

    carry = lax.fori_loop(kt0 + 1, qb, win_body, carry)
    o_w = finish(flash_step(carry, win_scores(tile_rows(qb), qb * TQ + lane_q <= tpos), vw_ref[tile_rows(qb), :]))

    gates = _pick_head_gates(misc_ref[:, 0:GATE_W], n)
    outs = []
    for g, r in enumerate(groups):
        (u_s, l_s), (u_w, l_w) = o_s[g], o_w[g]
        outs.append(gates[:, 3 * g:3 * g + 1] * o_c[r] + (gates[:, 3 * g + 1:3 * g + 2] / l_s) * u_s
                    + (gates[:, 3 * g + 2:3 * g + 3] / l_w) * u_w)
    o_ref[...] = jnp.concatenate(outs, axis=1)


def _nsa_prompt(proj, kcv, slopes, b, t):
    n16 = t // CMP_STRIDE
    n_sel = t // SEL_BLOCK
    nqb = t // TQ
    assert n_sel <= LANES and t % KEY_TILE == 0
    k_sel, v_sel, k_win, v_win = _kv_prep(proj, b, t)
    seq = lambda w: pl.BlockSpec((None, None, t, w), lambda i, n, qb, sl: (i, n, 0, 0))
    grid_spec = pltpu.PrefetchScalarGridSpec(
        num_scalar_prefetch=1,
        grid=(b, NSA_KV_HEADS, nqb),
        in_specs=[pl.BlockSpec((TQ, Q_SLAB), lambda i, n, qb, sl: (i * nqb + qb, COL_Q // Q_SLAB + n)),
                  pl.BlockSpec((None, None, n16, KV_SLAB), lambda i, n, qb, sl: (i, n, 0, 0)),
                  seq(2 * LANES), seq(LANES), seq(LANES), seq(LANES),
                  pl.BlockSpec((TQ, LANES), lambda i, n, qb, sl: (i * nqb + qb, COL_MISC // LANES)),
                  pl.BlockSpec((NSA_HEADS, LANES), lambda i, n, qb, sl: (0, 0)),
                  pl.BlockSpec((n_sel, n16), lambda i, n, qb, sl: (0, 0))],
        out_specs=pl.BlockSpec((TQ, Q_SLAB), lambda i, n, qb, sl: (i * nqb + qb, n)),
    )
    overlap_t = jnp.asarray(_overlap_counts(np.arange(n16)[None, :], np.arange(n_sel)[:, None]))
    return pl.pallas_call(
        functools.partial(_nsa_prompt_kernel, n16=n16, n_sel=n_sel),
        grid_spec=grid_spec,
        out_shape=jax.ShapeDtypeStruct((b * t, NSA_WIDTH), F32),
        compiler_params=_cparams(("parallel", "parallel", "arbitrary")),
        name="nsa_prompt",
    )(slopes, proj, kcv, k_sel, v_sel, k_win, v_win, proj, _slope_features(), overlap_t)


def _gla_kernel(q_ref, k_ref, v_ref, r_ref, misc_ref, wa_ref, ba_ref, gout_ref, s0_ref, o_ref, sfin_ref, state_ref,
                *, chunk, sub, valid_rows):
    c = pl.program_id(1)

    @pl.when(c == 0)
    def _():
        state_ref[...] = s0_ref[...]

    ga = misc_ref[:, MISC_GA:MISC_GA + GLA_RANK]
    x = jnp.dot(ga, wa_ref[...], preferred_element_type=F32, precision=HIGHEST) + ba_ref[...]
    lg = (jnp.minimum(x, 0.0) - jnp.log1p(jnp.exp(-jnp.abs(x)))) / GLA_TAU
    row = lax.broadcasted_iota(jnp.int32, (chunk, 1), 0)
    if valid_rows < chunk:
        lg = jnp.where(row < valid_rows, lg, 0.0)
    tri = (lax.broadcasted_iota(jnp.int32, (chunk, chunk), 0)
           >= lax.broadcasted_iota(jnp.int32, (chunk, chunk), 1)).astype(F32)
    cum_all = jnp.dot(tri, lg, preferred_element_type=F32, precision=HIGHEST)
    for h in range(GLA_HEADS):
        kcols = slice(h * GLA_DK, (h + 1) * GLA_DK)
        vcols = slice(h * GLA_DV, (h + 1) * GLA_DV)
        o, new_state = _gla_head_chunk(q_ref[:, kcols] * (GLA_DK ** -0.5), k_ref[:, kcols], v_ref[:, vcols],
                                       cum_all[:, kcols], state_ref[h], chunk, sub)
        state_ref[h] = new_state
        rg = r_ref[:, vcols]
        o_ref[:, vcols] = _rms(o, gout_ref[...]) * (rg * _sigmoid(rg))

    @pl.when(c == pl.num_programs(1) - 1)
    def _():
        sfin_ref[...] = state_ref[...]


def _gla_head_chunk(q, k, v, cum, state, chunk, sub):
    vb = v.astype(BF16)
    inter = _dot((q * jnp.exp(cum)).astype(BF16), state.astype(BF16))

    outs = []
    for i in range(chunk // sub):
        r0 = i * sub
        qi, ki, ci, vi = q[r0:r0 + sub], k[r0:r0 + sub], cum[r0:r0 + sub], v[r0:r0 + sub]
        o_i = inter[r0:r0 + sub]
        if i > 0:
            anchor = cum[r0:r0 + 1]
            qd = (qi * jnp.exp(ci - anchor)).astype(BF16)
            kd = (k[0:r0] * jnp.exp(anchor - cum[0:r0])).astype(BF16)
            o_i = o_i + _dot(_dot_nt(qd, kd).astype(BF16), vb[0:r0])
        trow = lax.broadcasted_iota(jnp.int32, (sub, 1), 0)
        for s_ in range(sub):
            w = jnp.sum(qi * (ki[s_:s_ + 1] * jnp.exp(jnp.minimum(ci - ci[s_:s_ + 1], 0.0))), axis=-1, keepdims=True)
            o_i = o_i + jnp.where(trow >= s_, w, 0.0) * vi[s_:s_ + 1]
        outs.append(o_i)
    o = jnp.concatenate(outs, axis=0) if len(outs) > 1 else outs[0]

    last = cum[chunk - 1:chunk]
    kdec = (k * jnp.exp(last - cum)).astype(BF16)
    decay_col = jnp.transpose(jnp.broadcast_to(jnp.exp(last), (SUBLANES, GLA_DK)))[:, 0:1]
    return o, decay_col * state + _dot_tn(kdec, vb)


def _gla(proj, w_a2, b_a, g_out, s0, b, t, chunk, sub, valid_rows):
    nck = t // chunk
    kw = GLA_HEADS * GLA_DK
    state_spec = pl.BlockSpec((None, GLA_HEADS, GLA_DK, GLA_DV), lambda i, c: (i, 0, 0, 0))
    return pl.pallas_call(
        functools.partial(_gla_kernel, chunk=chunk, sub=sub, valid_rows=valid_rows),
        grid=(b, nck),
        in_specs=[pl.BlockSpec((chunk, kw), lambda i, c: (i * nck + c, COL_GQ // kw)),
                  pl.BlockSpec((chunk, kw), lambda i, c: (i * nck + c, COL_GK // kw)),
                  pl.BlockSpec((chunk, GLA_WIDTH), lambda i, c: (i * nck + c, COL_GV // GLA_WIDTH)),
                  pl.BlockSpec((chunk, GLA_WIDTH), lambda i, c: (i * nck + c, COL_GR // GLA_WIDTH)),
                  pl.BlockSpec((chunk, LANES), lambda i, c: (i * nck + c, COL_MISC // LANES)),
                  pl.BlockSpec((GLA_RANK, kw), lambda i, c: (0, 0)),
                  pl.BlockSpec((1, kw), lambda i, c: (0, 0)),
                  pl.BlockSpec((1, GLA_DV), lambda i, c: (0, 0)),
                  state_spec],
        out_specs=[pl.BlockSpec((chunk, GLA_WIDTH), lambda i, c: (i * nck + c, 0)), state_spec],
        out_shape=[jax.ShapeDtypeStruct((b * t, GLA_WIDTH), F32),
                   jax.ShapeDtypeStruct((b, GLA_HEADS, GLA_DK, GLA_DV), F32)],
        scratch_shapes=[pltpu.VMEM((GLA_HEADS, GLA_DK, GLA_DV), F32)],
        compiler_params=_cparams(("parallel", "arbitrary")),
        name="gla",
    )(proj, proj, proj, proj, proj, w_a2, b_a, g_out, s0)


def _out_proj_kernel(on_ref, og_ref, x_ref, g_ref, w_ref, o_ref, a_ref):
    @pl.when(pl.program_id(1) == 0)
    def _():
        a_ref[:, :NSA_WIDTH] = _rms(on_ref[...], g_ref[...]).astype(BF16)
        a_ref[:, NSA_WIDTH:] = og_ref[...].astype(BF16)

    o_ref[...] = x_ref[...] + _dot(a_ref[...], w_ref[...])


def _out_proj(o_nsa, o_gla, x, g_nsa, w_out, tm, tn):
    m = x.shape[0]
    return pl.pallas_call(
        _out_proj_kernel,
        grid=(m // tm, D_MODEL // tn),
        in_specs=[pl.BlockSpec((tm, NSA_WIDTH), lambda i, j: (i, 0)),
                  pl.BlockSpec((tm, GLA_WIDTH), lambda i, j: (i, 0)),
                  pl.BlockSpec((tm, tn), lambda i, j: (i, j)),
                  pl.BlockSpec((1, NSA_WIDTH), lambda i, j: (0, 0)),
                  pl.BlockSpec((NSA_WIDTH + GLA_WIDTH, tn), lambda i, j: (0, j))],
        out_specs=pl.BlockSpec((tm, tn), lambda i, j: (i, j)),
        out_shape=jax.ShapeDtypeStruct((m, D_MODEL), F32),
        scratch_shapes=[pltpu.VMEM((tm, NSA_WIDTH + GLA_WIDTH), BF16)],
        compiler_params=_cparams(("parallel", "arbitrary")),
        name="out_proj",
    )(o_nsa, o_gla, x, g_nsa, w_out)


def _ffn_kernel(h_ref, g_ref, wa_ref, wg_ref, cw_ref, cb_ref, wd_ref, p1_ref, p2_ref, o_ref, tail_ref,
                n2_ref, acc_ref, carry_ref, *, tm, tf, seq_rows):
    i = pl.program_id(0)
    j = pl.program_id(1)

    @pl.when(j == 0)
    def _():
        n2_ref[...] = _rms(h_ref[...], g_ref[...]).astype(BF16)
        acc_ref[...] = jnp.zeros_like(acc_ref)

    n2 = n2_ref[...]
    a = _dot(n2, wa_ref[...])
    gate = _dot(n2, wg_ref[...])
    row = lax.broadcasted_iota(jnp.int32, (tm, 1), 0)
    r1 = pltpu.roll(a, 1, axis=0)
    r2 = pltpu.roll(a, 2, axis=0)
    if seq_rows >= tm:
        cols = pl.ds(pl.multiple_of(j * tf, tf), tf)
        first = (i % (seq_rows // tm)) == 0
        prev = jnp.where(first, p2_ref[...], carry_ref[:, cols])
        a1 = jnp.where(row == 0, prev[1:2], r1)
        a2 = jnp.where(row == 0, prev[0:1], jnp.where(row == 1, prev[1:2], r2))
        carry_ref[:, cols] = a[tm - 2:tm]
    else:
        t = row % seq_rows
        a1 = jnp.where(t == 0, p1_ref[...], r1)
        a2 = jnp.where(t < 2, p2_ref[...], r2)
    cw = cw_ref[...]
    conv = cb_ref[...] + a2 * cw[0:1] + a1 * cw[1:2] + a * cw[2:3]
    y = (_gelu_tanh(conv) * gate).astype(BF16)
    acc_ref[...] += _dot(y, wd_ref[...])
    tail_ref[...] = a[tm - tail_ref.shape[0]:tm]

    @pl.when(j == pl.num_programs(1) - 1)
    def _():
        o_ref[...] = h_ref[...] + acc_ref[...]


def _ffn(h, g_ffn, w_up, conv_w, conv_b, w_down, p1, p2, tm, tf, seq_rows):
    m = h.shape[0]
    nj = D_FF // tf
    if seq_rows >= tm:
        tiles_per_seq = seq_rows // tm
        p1_spec = pl.BlockSpec((None, CONV_W - 1, tf), lambda i, j: (i // tiles_per_seq, 0, j))
        p2_spec = pl.BlockSpec((None, CONV_W - 1, tf), lambda i, j: (i // tiles_per_seq, 0, j))
    else:
        p1_spec = pl.BlockSpec((tm, tf), lambda i, j: (i, j))
        p2_spec = pl.BlockSpec((tm, tf), lambda i, j: (i, j))
    if seq_rows >= tm:
        tail_spec = pl.BlockSpec((None, SUBLANES, tf), lambda i, j: (i, 0, j))
        tail_shape = jax.ShapeDtypeStruct((m // tm, SUBLANES, D_FF), F32)
    else:
        tail_spec = pl.BlockSpec((tm, tf), lambda i, j: (i, j))
        tail_shape = jax.ShapeDtypeStruct((m, D_FF), F32)
    return pl.pallas_call(
        functools.partial(_ffn_kernel, tm=tm, tf=tf, seq_rows=seq_rows),
        grid=(m // tm, nj),
        in_specs=[pl.BlockSpec((tm, D_MODEL), lambda i, j: (i, 0)),
                  pl.BlockSpec((1, D_MODEL), lambda i, j: (0, 0)),
                  pl.BlockSpec((D_MODEL, tf), lambda i, j: (0, j)),
                  pl.BlockSpec((D_MODEL, tf), lambda i, j: (0, nj + j)),
                  pl.BlockSpec((CONV_W, tf), lambda i, j: (0, j)),
                  pl.BlockSpec((1, tf), lambda i, j: (0, j)),
                  pl.BlockSpec((tf, D_MODEL), lambda i, j: (j, 0)),
                  p1_spec, p2_spec],
        out_specs=[pl.BlockSpec((tm, D_MODEL), lambda i, j: (i, 0)),
                   tail_spec],
        out_shape=[jax.ShapeDtypeStruct((m, D_MODEL), F32), tail_shape],
        scratch_shapes=[pltpu.VMEM((tm, D_MODEL), BF16), pltpu.VMEM((tm, D_MODEL), F32),
                        pltpu.VMEM((CONV_W - 1, D_FF), F32)],
        compiler_params=_cparams(("arbitrary", "arbitrary")),
        name="conv_ffn",
    )(h, g_ffn, w_up, w_up, conv_w, conv_b, w_down, p1, p2)


def _ple_kernel(h_ref, p_ref, wg_ref, wp_ref, gp_ref, gf_ref, o_ref, *, final_norm):
    h = h_ref[...]
    gate = _sigmoid(_dot(h.astype(BF16), wg_ref[...]))
    pe = _rms(_dot(p_ref[...].astype(BF16), wp_ref[...]), gp_ref[...])
    h = h + gate * pe
    o_ref[...] = _rms(h, gf_ref[...]) if final_norm else h


def _ple(h, p, w_gate, w_proj, g_ple, g_final, tm, final_norm):
    m = h.shape[0]
    return pl.pallas_call(
        functools.partial(_ple_kernel, final_norm=final_norm),
        grid=(m // tm,),
        in_specs=[pl.BlockSpec((tm, D_MODEL), lambda i: (i, 0)),
                  pl.BlockSpec((tm, PLE_DIM), lambda i: (i, 0)),
                  pl.BlockSpec((D_MODEL, D_MODEL), lambda i: (0, 0)),
                  pl.BlockSpec((PLE_DIM, D_MODEL), lambda i: (0, 0)),
                  pl.BlockSpec((1, D_MODEL), lambda i: (0, 0)),
                  pl.BlockSpec((1, D_MODEL), lambda i: (0, 0))],
        out_specs=pl.BlockSpec((tm, D_MODEL), lambda i: (i, 0)),
        out_shape=jax.ShapeDtypeStruct((m, D_MODEL), F32),
        compiler_params=_cparams(("parallel",)),
        name="ple_norm",
    )(h, p, w_gate, w_proj, g_ple, g_final)


CMP_PAGES = 16
CHUNKS_PER_PAGE = PAGE_SIZE // CMP_STRIDE


def _cmp_sample_kernel(pt_ref, *refs):
    page_refs = refs[:CMP_PAGES]
    perm_ref, w1_ref, w1p_ref, w2_ref, pe_ref, o_ref, carry_ref, x_ref = refs[CMP_PAGES:]
    grp = pl.program_id(1)
    nck = CMP_PAGES * CHUNKS_PER_PAGE

    @pl.when(grp == 0)
    def _():
        carry_ref[...] = jnp.zeros_like(carry_ref)

    perm = perm_ref[...]
    for k, r in enumerate(page_refs):
        xp = _dot_nt(perm, r[...].astype(BF16))
        for s in range(CMP_STRIDE):
            for n in range(NSA_KV_HEADS):
                x_ref[s, n, k * CHUNKS_PER_PAGE:(k + 1) * CHUNKS_PER_PAGE, :] = (
                    xp[s * CHUNKS_PER_PAGE:(s + 1) * CHUNKS_PER_PAGE, n * KV_SLAB:(n + 1) * KV_SLAB])
    acc = jnp.zeros((NSA_KV_HEADS * nck, 4 * CMP_HIDDEN), F32)
    for sp in range(CMP_STRIDE // 2):
        xs = jnp.concatenate([x_ref[2 * sp].reshape(NSA_KV_HEADS * nck, KV_SLAB),
                              x_ref[2 * sp + 1].reshape(NSA_KV_HEADS * nck, KV_SLAB)], axis=1)
        acc = acc + _dot(xs.astype(BF16), w1p_ref[sp])
    a = acc[:, :2 * CMP_HIDDEN]
    b = acc[:, 2 * CMP_HIDDEN:]
    a_prev = pltpu.roll(a, 1, axis=0)
    row = lax.broadcasted_iota(jnp.int32, (NSA_KV_HEADS * nck, 1), 0)
    for n in range(NSA_KV_HEADS):
        a_prev = jnp.where(row == n * nck, carry_ref[n:n + 1, :], a_prev)
        carry_ref[n:n + 1, :] = a[(n + 1) * nck - 1:(n + 1) * nck]
    hid = a_prev + b + _cmp_pe_hidden(pe_ref, w1_ref)
    res = _dot(_gelu_tanh(hid).astype(BF16), w2_ref[...])
    for n in range(NSA_KV_HEADS):
        o_ref[n] = res[n * nck:(n + 1) * nck]


def _cmp_sample(cache_c, page_table_flat, w1, w2, pe, b, n_pages):
    n_grp = n_pages // CMP_PAGES
    nck = CMP_PAGES * CHUNKS_PER_PAGE

    def page_spec(k):
        return pl.BlockSpec((None, KV_WIDTH, PAGE_SIZE),
                            lambda i, g, pt: (pt[i * n_pages + g * CMP_PAGES + k], 0, 0))

    r = np.arange(PAGE_SIZE)
    perm_np = np.zeros((PAGE_SIZE, PAGE_SIZE), np.float32)
    perm_np[(r % CMP_STRIDE) * CHUNKS_PER_PAGE + r // CMP_STRIDE, r] = 1.0
    perm = jnp.asarray(perm_np, dtype=BF16)
    w1p = w1.reshape(CMP_STRIDE // 2, 2 * KV_SLAB, 4 * CMP_HIDDEN)

    grid_spec = pltpu.PrefetchScalarGridSpec(
        num_scalar_prefetch=1,
        grid=(b, n_grp),
        in_specs=[page_spec(k) for k in range(CMP_PAGES)]
        + [pl.BlockSpec(perm.shape, lambda i, g, pt: (0, 0)),
           pl.BlockSpec(w1.shape, lambda i, g, pt: (0, 0, 0)),
           pl.BlockSpec(w1p.shape, lambda i, g, pt: (0, 0, 0)),
           pl.BlockSpec(w2.shape, lambda i, g, pt: (0, 0)),
           pl.BlockSpec(pe.shape, lambda i, g, pt: (0, 0))],
        out_specs=pl.BlockSpec((None, NSA_KV_HEADS, nck, KV_SLAB), lambda i, g, pt: (i, 0, g, 0)),
        scratch_shapes=[pltpu.VMEM((SUBLANES, 2 * CMP_HIDDEN), F32),
                        pltpu.VMEM((CMP_STRIDE, NSA_KV_HEADS, nck, KV_SLAB), F32)],
    )
    return pl.pallas_call(
        _cmp_sample_kernel,
        grid_spec=grid_spec,
        out_shape=jax.ShapeDtypeStruct((b, NSA_KV_HEADS, n_pages * CHUNKS_PER_PAGE, KV_SLAB), F32),
        compiler_params=_cparams(("parallel", "arbitrary")),
        name="cmp_sample",
    )(page_table_flat, *([cache_c] * CMP_PAGES), perm, w1, w1p, w2, pe)


def _sel_sample_kernel(slopes_ref, q_ref, kcv_ref, ov_ref, oc_ref, idx_ref, *, past_len, n_sel_pad, t_valid):
    for n in range(NSA_KV_HEADS):
        oc, idx = _sel_sample_head(n, slopes_ref, q_ref[:, n * Q_SLAB:(n + 1) * Q_SLAB] * SCALE, kcv_ref[n],
                                   ov_ref[...], past_len, n_sel_pad, t_valid)
        oc_ref[n] = oc
        idx_ref[n] = idx


def _sel_sample_head(n, slopes_ref, qf, kcv, overlap, past_len, n_sel_pad, t_valid):
    tp = SUBLANES
    rows = NSA_GROUP * tp
    n_rows_c = past_len // CMP_STRIDE
    nb_past = past_len // SEL_BLOCK
    q = jnp.concatenate([qf[:, g * HEAD_DIM:(g + 1) * HEAD_DIM] for g in range(NSA_GROUP)], axis=0).astype(BF16)
    tok = lax.broadcasted_iota(jnp.int32, (rows, 1), 0) % tp
    qpos = past_len + tok
    slope = _slope_col(slopes_ref, n, tp, rows)
    kc = kcv[:, :HEAD_DIM].astype(BF16)
    vc = kcv[:, HEAD_DIM:].astype(BF16)
    s = _dot_nt(q, kc)
    cp = lax.broadcasted_iota(jnp.int32, (1, n_rows_c), 1)
    end = (cp - 1) * CMP_STRIDE + (CMP_BLOCK - 1)
    dist = (qpos - end).astype(F32)
    valid = (cp >= 1) & (dist >= 0)
    s = jnp.where(valid, s - slope * dist, NEG_INF)
    e = jnp.exp(s - jnp.max(s, axis=-1, keepdims=True))
    p = jnp.where(valid, e / jnp.sum(e, axis=-1, keepdims=True), 0.0)
    o_c = _dot(p.astype(BF16), vc)
    oc = jnp.concatenate([o_c[g * tp:(g + 1) * tp] for g in range(NSA_GROUP)], axis=1)
    p_grp = p[0:tp]
    for g in range(1, NSA_GROUP):
        p_grp = p_grp + p[g * tp:(g + 1) * tp]
    imp = jnp.dot(p_grp, overlap, preferred_element_type=F32, precision=HIGHEST)
    n_tail = -(-t_valid // SEL_BLOCK)
    n_sel = nb_past + n_tail
    j = lax.broadcasted_iota(jnp.int32, (tp, n_sel_pad), 1)
    cur = qpos[0:tp] // SEL_BLOCK
    forced = (j == 0) | (j == cur) | (j == cur - 1)
    imp = jnp.where(forced, FORCE_SCORE, jnp.where(j > cur, -FORCE_SCORE, imp))
    imp = jnp.where(j < n_sel, imp, REMOVED_SCORE)
    _, picks = _top_blocks(imp, min(N_SELECT, n_sel))
    kcol = lax.broadcasted_iota(jnp.int32, (tp, N_SELECT), 1)
    idx = jnp.zeros((tp, N_SELECT), jnp.int32)
    for kk, pk in enumerate(picks):
        idx = jnp.where(kcol == kk, pk, idx)
    return oc, idx


def _sel_sample(proj8, kcv, slopes, b, past_len, t_valid):
    n_rows_c = past_len // CMP_STRIDE
    nb_past = past_len // SEL_BLOCK
    n_sel_pad = -(-(nb_past + 1) // LANES) * LANES
    overlap = jnp.asarray(_overlap_counts(np.arange(n_rows_c)[:, None] - 1, np.arange(n_sel_pad)[None, :]))
    grid_spec = pltpu.PrefetchScalarGridSpec(
        num_scalar_prefetch=1,
        grid=(b,),
        in_specs=[pl.BlockSpec((SUBLANES, NSA_WIDTH), lambda i, sl: (i, COL_Q // NSA_WIDTH)),
                  pl.BlockSpec((None, NSA_KV_HEADS, n_rows_c, KV_SLAB), lambda i, sl: (i, 0, 0, 0)),
                  pl.BlockSpec(overlap.shape, lambda i, sl: (0, 0))],
        out_specs=[pl.BlockSpec((None, NSA_KV_HEADS, SUBLANES, Q_SLAB), lambda i, sl: (i, 0, 0, 0)),
                   pl.BlockSpec((None, NSA_KV_HEADS, SUBLANES, N_SELECT), lambda i, sl: (i, 0, 0, 0))],
    )
    return pl.pallas_call(
        functools.partial(_sel_sample_kernel, past_len=past_len, n_sel_pad=n_sel_pad, t_valid=t_valid),
        grid_spec=grid_spec,
        out_shape=[jax.ShapeDtypeStruct((b, NSA_KV_HEADS, SUBLANES, Q_SLAB), F32),
                   jax.ShapeDtypeStruct((b, NSA_KV_HEADS, SUBLANES, N_SELECT), jnp.int32)],
        compiler_params=_cparams(("parallel",)),
        name="sel_sample",
    )(slopes, proj8, kcv, overlap)


def _nsa_sample_kernel(idx_ref, pt_ref, slopes_ref, cache_ref, q_ref, tail_ref, wnew_ref, wcache_ref, oc_ref,
                       misc_ref, o_ref, pages_ref, sem_ref, *, past_len, t_valid, n_pages):
    i = pl.program_id(0)
    n = pl.program_id(1)
    n_blk = t_valid * N_SELECT
    nb_past = past_len // SEL_BLOCK
    per_page = PAGE_SIZE // SEL_BLOCK
    step = i * NSA_KV_HEADS + n
    n_steps = pl.num_programs(0) * NSA_KV_HEADS

    def page_copy(s, j, slot):
        t, kk = j // N_SELECT, j % N_SELECT
        jb = jnp.minimum(idx_ref[(s * SUBLANES + t) * N_SELECT + kk], nb_past - 1)
        page = pt_ref[(s // NSA_KV_HEADS) * n_pages + jb // per_page]
        return pltpu.make_async_copy(cache_ref.at[page, s % NSA_KV_HEADS], pages_ref.at[slot, j], sem_ref.at[slot])

    def start_all(s, slot):
        def body(j, c):
            page_copy(s, j, slot).start()
            return c
        lax.fori_loop(0, n_blk, body, 0)

    @pl.when(step == 0)
    def _():
        start_all(step, 0)

    @pl.when(step + 1 < n_steps)
    def _():
        start_all(step + 1, (step + 1) % 2)

    slot = step % 2

    def wait_body(j, c):
        page_copy(step, j, slot).wait()
        return c
    lax.fori_loop(0, n_blk, wait_body, 0)
    blk_refs = [pages_ref.at[slot, j] for j in range(n_blk)]
    slope = _slope_col(slopes_ref, n, 1, SUBLANES)
    gates = _pick_head_gates(misc_ref[:, 0:GATE_W], n)
    tl = tail_ref[...]
    wn = wnew_ref[...]
    wct = wcache_ref[...]
    win_k = wct[:HEAD_DIM].astype(BF16)
    win_v = wct[HEAD_DIM:].astype(BF16)
    rows_out = []
    for t in range(t_valid):
        picked = [blk_refs[t * N_SELECT + kk] for kk in range(N_SELECT)]
        base = ((i * NSA_KV_HEADS + n) * SUBLANES + t) * N_SELECT
        jbs = [idx_ref[base + kk] for kk in range(N_SELECT)]
        rows_out.append(_nsa_sample_token(t, q_ref[t:t + 1, :] * SCALE, slope, picked, jbs, tl, wn, win_k, win_v,
                                          oc_ref[t:t + 1, :], gates[t:t + 1, :], past_len))
    rows_out.append(jnp.zeros((SUBLANES - t_valid, Q_SLAB), F32))
    o_ref[...] = jnp.concatenate(rows_out, axis=0)


def _nsa_sample_token(t, qrow, slope, picked, jbs, tl, wn, win_k, win_v, oc_row, gates, past_len):
    nb_past = past_len // SEL_BLOCK
    g8 = SUBLANES
    q = jnp.concatenate([qrow[:, g * HEAD_DIM:(g + 1) * HEAD_DIM] for g in range(NSA_GROUP)]
                        + [jnp.zeros((g8 - NSA_GROUP, HEAD_DIM), F32)], axis=0).astype(BF16)
    qpos = past_len + t

    per_page = PAGE_SIZE // SEL_BLOCK
    k_all = jnp.concatenate([r[0:HEAD_DIM, :] for r in picked], axis=1).astype(BF16)
    v_all = jnp.concatenate([r[HEAD_DIM:KV_SLAB, :] for r in picked], axis=1).astype(BF16)
    jb_row = jnp.concatenate([jnp.full((1, PAGE_SIZE), jb, jnp.int32) for jb in jbs], axis=1)
    tail_count = jnp.zeros((), jnp.int32)
    for jb in jbs:
        tail_count = tail_count + (jb >= nb_past).astype(jnp.int32)
    lane = lax.broadcasted_iota(jnp.int32, (1, N_SELECT * PAGE_SIZE), 1) % PAGE_SIZE
    kpos = (jb_row // per_page) * PAGE_SIZE + lane
    dist = (qpos - kpos).astype(F32)
    ok = (dist >= 0) & (kpos // SEL_BLOCK == jb_row) & (jb_row < nb_past)
    sc = _dot(q, k_all) - slope * dist
    parts = [(jnp.where(ok, sc, NEG_INF), ok, v_all, True)]
    lane8 = lax.broadcasted_iota(jnp.int32, (1, SUBLANES), 1)
    dist = (t - lane8).astype(F32)
    ok = (dist >= 0) & (jnp.full((1, SUBLANES), tail_count, jnp.int32) > 0)
    sc = _dot_nt(q, tl[:, :HEAD_DIM].astype(BF16)) - slope * dist
    parts.append((jnp.where(ok, sc, NEG_INF), ok, tl[:, HEAD_DIM:].astype(BF16), False))

    def softmax_av(parts):
        m = parts[0][0].max(axis=-1, keepdims=True)
        for sc, _, _, _ in parts[1:]:
            m = jnp.maximum(m, sc.max(axis=-1, keepdims=True))
        l = jnp.zeros((g8, 1), F32)
        acc = jnp.zeros((g8, HEAD_DIM), F32)
        for sc, ok, v, v_transposed in parts:
            pr = jnp.where(ok, jnp.exp(sc - m), 0.0)
            l = l + pr.sum(axis=-1, keepdims=True)
            acc = acc + (_dot_nt(pr.astype(BF16), v) if v_transposed else _dot(pr.astype(BF16), v))
        return acc / l

    o_s = softmax_av(parts)

    buf_len = win_k.shape[1]
    lane_w = lax.broadcasted_iota(jnp.int32, (1, buf_len), 1)
    dist_c = (qpos - (past_len - buf_len + lane_w)).astype(F32)
    ok_c = (dist_c >= 0) & (dist_c < WINDOW)
    sc_c = jnp.where(ok_c, _dot(q, win_k) - slope * dist_c, NEG_INF)
    dist_n = (t - lane8).astype(F32)
    ok_n = (dist_n >= 0) & (dist_n < WINDOW)
    sc_n = jnp.where(ok_n, _dot_nt(q, wn[:, :HEAD_DIM].astype(BF16)) - slope * dist_n, NEG_INF)
    o_w = softmax_av([(sc_c, ok_c, win_v, True), (sc_n, ok_n, wn[:, HEAD_DIM:].astype(BF16), False)])

    outs = []
    for g in range(NSA_GROUP):
        outs.append(gates[:, 3 * g:3 * g + 1] * oc_row[:, g * HEAD_DIM:(g + 1) * HEAD_DIM]
                    + gates[:, 3 * g + 1:3 * g + 2] * o_s[g:g + 1]
                    + gates[:, 3 * g + 2:3 * g + 3] * o_w[g:g + 1])
    return jnp.concatenate(outs, axis=1)


def _nsa_sample(proj8, cache_s, cache_w, o_c, idx_flat, page_table_flat, slopes, b, t_valid, past_len):
    n_pages = past_len // PAGE_SIZE
    nb_past = past_len // SEL_BLOCK
    per_page = PAGE_SIZE // SEL_BLOCK
    buf_len = cache_w.shape[-1]

    n_blk = t_valid * N_SELECT
    grid_spec = pltpu.PrefetchScalarGridSpec(
        num_scalar_prefetch=3,
        grid=(b, NSA_KV_HEADS),
        in_specs=[pl.BlockSpec(memory_space=pl.ANY),
                  pl.BlockSpec((SUBLANES, Q_SLAB), lambda i, n, *_: (i, COL_Q // Q_SLAB + n)),
                  pl.BlockSpec((SUBLANES, KV_SLAB), lambda i, n, *_: (i, COL_KVS // KV_SLAB + n)),
                  pl.BlockSpec((SUBLANES, KV_SLAB), lambda i, n, *_: (i, COL_KVW // KV_SLAB + n)),
                  pl.BlockSpec((None, None, KV_SLAB, buf_len), lambda i, n, *_: (i, n, 0, 0)),
                  pl.BlockSpec((None, None, SUBLANES, Q_SLAB), lambda i, n, *_: (i, n, 0, 0)),
                  pl.BlockSpec((SUBLANES, LANES), lambda i, n, *_: (i, COL_MISC // LANES))],
        out_specs=pl.BlockSpec((SUBLANES, Q_SLAB), lambda i, n, *_: (i, n)),
        scratch_shapes=[pltpu.VMEM((2, n_blk, KV_SLAB, PAGE_SIZE), F32), pltpu.SemaphoreType.DMA((2,))],
    )
    return pl.pallas_call(
        functools.partial(_nsa_sample_kernel, past_len=past_len, t_valid=t_valid, n_pages=n_pages),
        grid_spec=grid_spec,
        out_shape=jax.ShapeDtypeStruct((b * SUBLANES, NSA_WIDTH), F32),
        compiler_params=_cparams(("arbitrary", "arbitrary")),
        name="nsa_sample",
    )(idx_flat, page_table_flat, slopes, cache_s, proj8, proj8, proj8, cache_w, o_c, proj8)


def _alibi_slopes():
    h = np.arange(1, NSA_HEADS + 1, dtype=np.float32)
    return jnp.asarray(2.0 ** (-8.0 * h / NSA_HEADS), dtype=F32)


def _permute_w_in(w_in):
    offs = np.cumsum([0, NSA_WIDTH, KV_WIDTH, KV_WIDTH, KV_WIDTH, GATE_W, GLA_HEADS * GLA_DK, GLA_HEADS * GLA_DK,
                      GLA_WIDTH, GLA_WIDTH, GLA_RANK])
    piece = [w_in[:, offs[k]:offs[k + 1]] for k in range(10)]
    q, kvc, kvs, kvw, gl, gq, gk, gv, gr, ga = piece
    pad = jnp.zeros((w_in.shape[0], LANES - GATE_W - GLA_RANK), w_in.dtype)
    return jnp.concatenate([q, gv, gr, kvc, kvs, kvw, gq, gk, gl, ga, pad], axis=1).astype(BF16)


def _row_tile(m, pref):
    return pref if m % pref == 0 else m


def _dense_tail(h_in, o_nsa, o_gla, p_emb, wts, conv_p1, conv_p2, seq_rows, last_layer, tm):
    (g_nsa, w_out, g_ffn, w_up, conv_w, conv_b, w_down, w_ple_proj, g_ple, w_ple_gate, g_final) = wts
    h = _out_proj(o_nsa, o_gla, h_in, g_nsa, w_out, tm, D_MODEL // 2)
    h, tails = _ffn(h, g_ffn, w_up, conv_w, conv_b, w_down, conv_p1, conv_p2, tm, 512, seq_rows)
    y = _ple(h, p_emb, w_ple_gate, w_ple_proj, g_ple, g_final, tm, last_layer)
    return y, tails


def kernel(x_prompt, x_sample, p_prompt, p_sample, cache_cmp_kv, cache_sel_kv, cache_win_kv, state_gla, state_ffn_conv, page_table, g_attn, w_in, w_cmp_k1, w_cmp_k2, pe_cmp_k, w_cmp_v1, w_cmp_v2, pe_cmp_v, w_gla_a2, b_gla_a, g_nsa_out, g_gla_out, w_out, g_ffn, w_up, conv_w, conv_b, w_down, w_ple_proj, g_ple, w_ple_gate, g_final):
    depth = w_in.shape[0]
    bp, tp, _ = x_prompt.shape
    bs, ts, _ = x_sample.shape
    n_pages = page_table.shape[1]
    past_len = n_pages * PAGE_SIZE
    n_pool = cache_cmp_kv.shape[1]
    assert tp % TQ == 0 and tp >= WINDOW and ts <= SUBLANES and ts <= SEL_BLOCK and ts >= CONV_W - 1
    assert n_pages % CMP_PAGES == 0 and tp // SEL_BLOCK >= N_SELECT
    slopes = _alibi_slopes()
    pt_flat = page_table.reshape(-1).astype(jnp.int32)
    kv_shape = (NSA_KV_HEADS, 2, HEAD_DIM)

    hp = x_prompt.reshape(bp * tp, D_MODEL)
    hs = x_sample.reshape(bs * ts, D_MODEL)
    new_p = [[] for _ in range(5)]
    new_s = [[] for _ in range(5)]
    tm_p = _row_tile(bp * tp, 512)
    tm_s = bs * ts
    for i in range(depth):
        last = i == depth - 1
        w_in_p = _permute_w_in(w_in[i])
        g_a = g_attn[i].reshape(1, D_MODEL)
        w1c, w2c, pec = _cmp_weights(w_cmp_k1[i], w_cmp_k2[i], pe_cmp_k[i], w_cmp_v1[i], w_cmp_v2[i], pe_cmp_v[i])
        b_a = b_gla_a[i].reshape(1, -1)
        g_go = g_gla_out[i].reshape(1, GLA_DV)
        wts = (g_nsa_out[i].reshape(1, -1), w_out[i].astype(BF16), g_ffn[i].reshape(1, -1), w_up[i].astype(BF16),
               conv_w[i], conv_b[i].reshape(1, -1), w_down[i].astype(BF16), w_ple_proj[i].astype(BF16),
               g_ple[i].reshape(1, -1), w_ple_gate[i].astype(BF16), g_final.reshape(1, -1))

        proj = _norm_matmul(hp, g_a, w_in_p, tm_p, D_IN_PAD // 5)
        kcv = _cmp_prompt(proj, bp, tp, w1c, w2c, pec)
        o_nsa = _nsa_prompt(proj, kcv, slopes, bp, tp)
        s0 = jnp.zeros((bp, GLA_HEADS, GLA_DK, GLA_DV), F32)
        o_gla, s_new = _gla(proj, w_gla_a2[i], b_a, g_go, s0, bp, tp, 64, 16, 64)
        zbuf = jnp.zeros((bp, CONV_W - 1, D_FF), F32)
        hp, tails = _dense_tail(hp, o_nsa, o_gla, p_prompt[i].reshape(bp * tp, PLE_DIM), wts, zbuf, zbuf, tp, last, tm_p)
        proj3 = proj.reshape(bp, tp, D_IN_PAD)
        new_p[0].append(proj3[:, :, COL_KVC:COL_KVC + KV_WIDTH].reshape((bp, tp) + kv_shape))
        new_p[1].append(proj3[:, :, COL_KVS:COL_KVS + KV_WIDTH].reshape((bp, tp) + kv_shape))
        new_p[2].append(proj3[:, tp - WINDOW:, COL_KVW:COL_KVW + KV_WIDTH].reshape((bp, WINDOW) + kv_shape))
        new_p[3].append(s_new)
        tiles_per_seq = tp // tm_p
        new_p[4].append(tails.reshape(bp, tiles_per_seq, SUBLANES, D_FF)[:, -1, SUBLANES - (CONV_W - 1):, :])

        proj_s = _norm_matmul(hs, g_a, w_in_p, tm_s, D_IN_PAD // 5)
        proj8 = jnp.pad(proj_s.reshape(bs, ts, D_IN_PAD), ((0, 0), (0, SUBLANES - ts), (0, 0))).reshape(bs * SUBLANES, D_IN_PAD)
        cache_c = jnp.transpose(cache_cmp_kv[i], (0, 2, 3, 4, 1)).reshape(n_pool, KV_WIDTH, PAGE_SIZE)
        cache_s = jnp.transpose(cache_sel_kv[i], (0, 2, 3, 4, 1)).reshape(n_pool, NSA_KV_HEADS, KV_SLAB, PAGE_SIZE)
        cache_w = jnp.transpose(cache_win_kv[i], (0, 2, 3, 4, 1)).reshape(bs, NSA_KV_HEADS, KV_SLAB, -1)
        kcv_s = _cmp_sample(cache_c, pt_flat, w1c, w2c, pec, bs, n_pages)
        o_c, idx = _sel_sample(proj8, kcv_s, slopes, bs, past_len, ts)
        o_nsa_s = _nsa_sample(proj8, cache_s, cache_w, o_c, idx.reshape(-1), pt_flat, slopes, bs, ts, past_len)
        o_nsa_s = o_nsa_s.reshape(bs, SUBLANES, NSA_WIDTH)[:, :ts].reshape(bs * ts, NSA_WIDTH)
        o_gla_s, s_new_s = _gla(proj8, w_gla_a2[i], b_a, g_go, state_gla[i].astype(F32), bs, SUBLANES, SUBLANES, SUBLANES, ts)
        o_gla_s = o_gla_s.reshape(bs, SUBLANES, GLA_WIDTH)[:, :ts].reshape(bs * ts, GLA_WIDTH)
        buf = state_ffn_conv[i]
        zrow = jnp.zeros((bs, ts - 1, D_FF), F32)
        p1 = jnp.concatenate([buf[:, 1:2], zrow], axis=1).reshape(bs * ts, D_FF)
        p2 = jnp.concatenate([buf, jnp.zeros((bs, ts - 2, D_FF), F32)], axis=1).reshape(bs * ts, D_FF)
        hs, tails_s = _dense_tail(hs, o_nsa_s, o_gla_s, p_sample[i].reshape(bs * ts, PLE_DIM), wts, p1, p2, ts, last, tm_s)
        ps3 = proj_s.reshape(bs, ts, D_IN_PAD)
        new_s[0].append(ps3[:, :, COL_KVC:COL_KVC + KV_WIDTH].reshape((bs, ts) + kv_shape))
        new_s[1].append(ps3[:, :, COL_KVS:COL_KVS + KV_WIDTH].reshape((bs, ts) + kv_shape))
        new_s[2].append(ps3[:, :, COL_KVW:COL_KVW + KV_WIDTH].reshape((bs, ts) + kv_shape))
        new_s[3].append(s_new_s)
        new_s[4].append(tails_s.reshape(bs, ts, D_FF)[:, ts - (CONV_W - 1):, :])

    y_prompt = hp.reshape(bp, tp, D_MODEL)
    y_sample = hs.reshape(bs, ts, D_MODEL)
    cmp_p, sel_p, win_p, gla_p, conv_p = [jnp.stack(l) for l in new_p]
    cmp_s, sel_s, win_s, gla_s, conv_s = [jnp.stack(l) for l in new_s]
    return (y_prompt, y_sample, cmp_p, sel_p, win_p, gla_p, conv_p, cmp_s, sel_s, win_s, gla_s, conv_s)
```

```python
import functools

import numpy as np
import jax
import jax.numpy as jnp
from jax import lax
from jax.experimental import pallas as pl
from jax.experimental.pallas import tpu as pltpu

F32 = jnp.float32
BF16 = jnp.bfloat16
HIGHEST = lax.Precision.HIGHEST

D_MODEL = 2048
PAGE_SIZE = 128
NSA_HEADS = 16
NSA_KV_HEADS = 4
NSA_GROUP = NSA_HEADS // NSA_KV_HEADS
HEAD_DIM = 64
CMP_BLOCK = 32
CMP_STRIDE = 16
CMP_HIDDEN = 2 * HEAD_DIM
SEL_BLOCK = 64
N_SELECT = 16
WINDOW = 512
TQ = 256
GLA_HEADS = 4
GLA_DK = 128
GLA_DV = 256
GLA_RANK = 16
GLA_TAU = 16.0
D_FF = 5632
CONV_W = 3
PLE_DIM = 256
EPS = 1e-6
NEG_INF = -1e30
FORCE_SCORE = 1e9
REMOVED_SCORE = -3e38
SCALE = HEAD_DIM ** -0.5
LOG2E = np.float32(1.4426950408889634)

NSA_WIDTH = NSA_HEADS * HEAD_DIM
GLA_WIDTH = GLA_HEADS * GLA_DV
KV_WIDTH = 2 * NSA_KV_HEADS * HEAD_DIM
KV_SLAB = 2 * HEAD_DIM
Q_SLAB = NSA_GROUP * HEAD_DIM
GATE_W = 3 * NSA_HEADS

LANES = 128
SUBLANES = 8
VMEM_LIMIT = 56 * 1024 * 1024

COL_Q = 0
COL_GV = COL_Q + NSA_WIDTH
COL_GR = COL_GV + GLA_WIDTH
COL_KVC = COL_GR + GLA_WIDTH
COL_KVS = COL_KVC + KV_WIDTH
COL_KVW = COL_KVS + KV_WIDTH
COL_GQ = COL_KVW + KV_WIDTH
COL_GK = COL_GQ + GLA_HEADS * GLA_DK
COL_MISC = COL_GK + GLA_HEADS * GLA_DK
D_IN_PAD = COL_MISC + LANES
MISC_GA = GATE_W

DENSE_ROWS = 512
IN_PROJ_ROWS = 512
IN_PROJ_COLS = D_IN_PAD // 5
FFN_COLS = 512


def _cparams(sem):
    return pltpu.CompilerParams(dimension_semantics=sem, vmem_limit_bytes=VMEM_LIMIT)


def _rms(x, g):
    return x * lax.rsqrt(jnp.mean(x * x, axis=-1, keepdims=True) + EPS) * g


def _gelu_tanh(x):
    return 0.5 * x * (1.0 + jnp.tanh(np.float32(np.sqrt(2.0 / np.pi)) * (x + 0.044715 * (x * x * x))))


def _sigmoid(x):
    return 1.0 / (1.0 + jnp.exp(-x))


def _dot(a, b):
    return jnp.dot(a, b, preferred_element_type=F32)


def _dot_nt(a, b):
    return lax.dot_general(a, b, (((1,), (1,)), ((), ())), preferred_element_type=F32)


def _dot_tn(a, b):
    return lax.dot_general(a, b, (((0,), (0,)), ((), ())), preferred_element_type=F32)


def _norm_matmul_kernel(x_ref, g_ref, w_ref, o_ref, xn_ref):
    @pl.when(pl.program_id(1) == 0)
    def _():
        xn_ref[...] = _rms(x_ref[...], g_ref[...]).astype(BF16)

    o_ref[...] = _dot(xn_ref[...], w_ref[...])


def _norm_matmul(x, g, w, tm, tn):
    m, d = x.shape
    n = w.shape[1]
    return pl.pallas_call(
        _norm_matmul_kernel,
        grid=(m // tm, n // tn),
        in_specs=[pl.BlockSpec((tm, d), lambda i, j: (i, 0)),
                  pl.BlockSpec((1, d), lambda i, j: (0, 0)),
                  pl.BlockSpec((d, tn), lambda i, j: (0, j))],
        out_specs=pl.BlockSpec((tm, tn), lambda i, j: (i, j)),
        out_shape=jax.ShapeDtypeStruct((m, n), F32),
        scratch_shapes=[pltpu.VMEM((tm, d), BF16)],
        compiler_params=_cparams(("parallel", "arbitrary")),
        name="in_proj",
    )(x, g, w)


def _cmp_pe_hidden(pe_ref, w1_ref):
    span = CMP_BLOCK // CMP_STRIDE
    acc = jnp.zeros((SUBLANES, 2 * CMP_HIDDEN), F32)
    for j in range(span):
        for s in range(CMP_STRIDE):
            row = jnp.broadcast_to(pe_ref[pl.ds(j * CMP_STRIDE + s, 1), :], (SUBLANES, KV_SLAB)).astype(BF16)
            acc = acc + _dot(row, w1_ref[s][:, j * 2 * CMP_HIDDEN:(j + 1) * 2 * CMP_HIDDEN])
    return acc[0:1, :]


def _cmp_prompt_kernel(x_ref, w1_ref, w2_ref, pe_ref, o_ref, *, n16):
    acc = jnp.zeros((n16, 4 * CMP_HIDDEN), F32)
    for s in range(CMP_STRIDE):
        xs = x_ref[pl.ds(s, n16, stride=CMP_STRIDE), :].astype(BF16)
        acc = acc + _dot(xs, w1_ref[s])
    a = acc[:, :2 * CMP_HIDDEN]
    b_next = pltpu.roll(acc[:, 2 * CMP_HIDDEN:], n16 - 1, axis=0)
    hid = a + b_next + _cmp_pe_hidden(pe_ref, w1_ref)
    o_ref[...] = _dot(_gelu_tanh(hid).astype(BF16), w2_ref[...])


def _cmp_weights(w_k1, w_k2, pe_k, w_v1, w_v2, pe_v):
    z = jnp.zeros((CMP_BLOCK, HEAD_DIM, CMP_HIDDEN), F32)
    wl = jnp.concatenate([jnp.concatenate([w_k1, z], axis=2), jnp.concatenate([z, w_v1], axis=2)], axis=1)
    w1 = jnp.concatenate([wl[:CMP_STRIDE], wl[CMP_STRIDE:]], axis=2).astype(BF16)
    z2 = jnp.zeros((CMP_HIDDEN, HEAD_DIM), F32)
    w2 = jnp.concatenate([jnp.concatenate([w_k2, z2], axis=1), jnp.concatenate([z2, w_v2], axis=1)], axis=0)
    pe = jnp.concatenate([pe_k, pe_v], axis=1)
    return w1, w2.astype(BF16), pe


def _cmp_prompt(proj, b, t, w1, w2, pe):
    n16 = t // CMP_STRIDE
    col0 = COL_KVC // KV_SLAB
    return pl.pallas_call(
        functools.partial(_cmp_prompt_kernel, n16=n16),
        grid=(b, NSA_KV_HEADS),
        in_specs=[pl.BlockSpec((t, KV_SLAB), lambda i, n: (i, col0 + n)),
                  pl.BlockSpec(w1.shape, lambda i, n: (0, 0, 0)),
                  pl.BlockSpec(w2.shape, lambda i, n: (0, 0)),
                  pl.BlockSpec(pe.shape, lambda i, n: (0, 0))],
        out_specs=pl.BlockSpec((None, None, n16, KV_SLAB), lambda i, n: (i, n, 0, 0)),
        out_shape=jax.ShapeDtypeStruct((b, NSA_KV_HEADS, n16, KV_SLAB), F32),
        compiler_params=_cparams(("parallel", "parallel")),
        name="cmp_prompt",
    )(proj, w1, w2, pe)


def _overlap_counts(c, j):
    per_sel = SEL_BLOCK // CMP_STRIDE
    ov = sum(((c + k >= per_sel * j) & (c + k < per_sel * (j + 1))) for k in range(CMP_BLOCK // CMP_STRIDE))
    return np.where(c >= 0, ov, 0).astype(np.float32)


def _top_blocks(imp, n_pick):
    rows, nsel = imp.shape
    j = lax.broadcasted_iota(jnp.int32, (rows, nsel), 1).astype(F32)
    mask = jnp.zeros((rows, nsel), F32)
    picks = []
    for _ in range(n_pick):
        m = jnp.max(imp, axis=-1, keepdims=True)
        jmin = jnp.min(jnp.where(imp == m, j, float(nsel)), axis=-1, keepdims=True)
        hit = j == jmin
        mask = jnp.where(hit, 1.0, mask)
        imp = jnp.where(hit, REMOVED_SCORE, imp)
        picks.append(jmin.astype(jnp.int32))
    return mask, picks


def _slope_col(slopes_ref, n, rows_per_head, rows):
    g = lax.broadcasted_iota(jnp.int32, (rows, 1), 0) // rows_per_head
    col = jnp.zeros((rows, 1), F32)
    for gg in range(NSA_GROUP):
        col = jnp.where(g == gg, slopes_ref[n * NSA_GROUP + gg], col)
    return col


def _pick_head_gates(gl, n):
    out = jnp.zeros((gl.shape[0], 3 * NSA_GROUP), F32)
    for nn in range(NSA_KV_HEADS):
        out = jnp.where(n == nn, gl[:, nn * 3 * NSA_GROUP:(nn + 1) * 3 * NSA_GROUP], out)
    return _sigmoid(out)


MASK_BIAS = -131072.0
M_FLOOR = -65536.0
POS_HI, POS_LO = HEAD_DIM, HEAD_DIM + 3
KEY_TILE = TQ


def _slope_features():
    h = np.arange(1, NSA_HEADS + 1, dtype=np.float32)
    slopes = (2.0 ** (-8.0 * h / NSA_HEADS)).astype(np.float32)
    tab = np.zeros((NSA_HEADS, LANES), np.float32)
    rest = slopes * LOG2E
    for c in range(3):
        piece = rest.astype(BF16).astype(np.float32)
        tab[:, POS_HI + c] = piece
        tab[:, POS_LO + c] = piece
        rest = rest - piece
    assert not rest.any()
    return jnp.asarray(tab)


def _kv_prep_kernel(kvs_ref, kvw_ref, ks_ref, vs_ref, kw_ref, vw_ref, *, t):
    lane = lax.broadcasted_iota(jnp.int32, (t, LANES), 1)
    pos = lax.broadcasted_iota(jnp.int32, (t, LANES), 0)
    hi = ((pos // SEL_BLOCK) * SEL_BLOCK).astype(F32)
    lo = (pos % SEL_BLOCK).astype(F32)
    feat = jnp.where((lane >= POS_HI) & (lane < POS_HI + 3), hi,
                     jnp.where((lane >= POS_LO) & (lane < POS_LO + 3), lo, 0.0))
    ones_col = jnp.where(lane == HEAD_DIM, 1.0, 0.0)
    for src, k_out, v_out in ((kvs_ref, ks_ref, vs_ref), (kvw_ref, kw_ref, vw_ref)):
        x = src[...]
        k_out[:, 0:LANES] = jnp.where(lane < HEAD_DIM, x, feat).astype(BF16)
        v_out[...] = jnp.where(lane < HEAD_DIM, pltpu.roll(x, HEAD_DIM, axis=1), ones_col).astype(BF16)
    ks_ref[:, LANES:2 * LANES] = jnp.where(lane == pos // SEL_BLOCK, 1.0, 0.0).astype(BF16)


def _kv_prep(proj, b, t):
    spec_in = lambda col: pl.BlockSpec((t, KV_SLAB), lambda i, n: (i, col // KV_SLAB + n))
    spec_out = lambda w: pl.BlockSpec((None, None, t, w), lambda i, n: (i, n, 0, 0))
    shape = lambda w: jax.ShapeDtypeStruct((b, NSA_KV_HEADS, t, w), BF16)
    return pl.pallas_call(
        functools.partial(_kv_prep_kernel, t=t),
        grid=(b, NSA_KV_HEADS),
        in_specs=[spec_in(COL_KVS), spec_in(COL_KVW)],
        out_specs=[spec_out(2 * LANES), spec_out(LANES), spec_out(LANES), spec_out(LANES)],
        out_shape=[shape(2 * LANES), shape(LANES), shape(LANES), shape(LANES)],
        compiler_params=_cparams(("parallel", "parallel")),
        name="kv_prep",
    )(proj, proj)


def _nsa_prompt_kernel(slopes_ref, q_ref, kcv_ref, ks_ref, vs_ref, kw_ref, vw_ref, misc_ref, sfeat_ref, ovt_ref, o_ref,
                       *, n16, n_sel):
    n = pl.program_id(1)
    qb = pl.program_id(2)
    rows = NSA_GROUP * TQ
    qf = q_ref[...] * SCALE
    q = jnp.concatenate([qf[:, g * HEAD_DIM:(g + 1) * HEAD_DIM] for g in range(NSA_GROUP)], axis=0).astype(BF16)
    tok = lax.broadcasted_iota(jnp.int32, (rows, 1), 0) % TQ
    qpos = qb * TQ + tok
    slope = _slope_col(slopes_ref, n, TQ, rows)

    kcv = kcv_ref[...]
    kc = kcv[:, :HEAD_DIM].astype(BF16)
    vc = kcv[:, HEAD_DIM:].astype(BF16)
    s = _dot_nt(q, kc)
    end = lax.broadcasted_iota(jnp.int32, (1, n16), 1) * CMP_STRIDE + (CMP_BLOCK - 1)
    dist = (qpos - end).astype(F32)
    valid = (dist >= 0) & (end < n16 * CMP_STRIDE)
    s = jnp.where(valid, s - slope * dist, NEG_INF)
    e = jnp.exp(s - jnp.max(s, axis=-1, keepdims=True))
    p = jnp.where(valid, e / jnp.sum(e, axis=-1, keepdims=True), 0.0)
    o_c = _dot(p.astype(BF16), vc)
    p_grp = p[0:TQ]
    for g in range(1, NSA_GROUP):
        p_grp = p_grp + p[g * TQ:(g + 1) * TQ]
    imp = lax.dot_general(ovt_ref[...], p_grp, (((1,), (1,)), ((), ())),
                          preferred_element_type=F32, precision=HIGHEST)

    jj = lax.broadcasted_iota(jnp.int32, (n_sel, TQ), 0)
    cur = (qb * TQ + lax.broadcasted_iota(jnp.int32, (1, TQ), 1)) // SEL_BLOCK
    forced = (jj == 0) | (jj == cur) | (jj == cur - 1)
    imp = jnp.where(forced, FORCE_SCORE, jnp.where(jj > cur, -FORCE_SCORE, imp))
    beaten = jnp.zeros((n_sel, TQ), F32)
    for jp in range(n_sel):
        other = imp[jp:jp + 1, :]
        beats = (other > imp) | ((other == imp) & (jj > jp))
        beaten = beaten + jnp.where(beats, 1.0, 0.0)
    not_picked = jnp.where(beaten < min(N_SELECT, n_sel), 0.0, 1.0)
    if n_sel < LANES:
        not_picked = jnp.concatenate([not_picked, jnp.zeros((LANES - n_sel, TQ), F32)], axis=0)
    q_bias = jnp.transpose(not_picked) * MASK_BIAS

    qf2 = qf * LOG2E
    q_main = jnp.concatenate(
        [jnp.concatenate([qf2[:, g * HEAD_DIM:(g + 1) * HEAD_DIM],
                          jnp.broadcast_to(sfeat_ref[pl.ds(n * NSA_GROUP + g, 1), HEAD_DIM:LANES],
                                           (TQ, LANES - HEAD_DIM))], axis=1)
         for g in range(NSA_GROUP)], axis=0)
    groups = [slice(g * TQ, (g + 1) * TQ) for g in range(NSA_GROUP)]
    q_win = [q_main[r].astype(BF16) for r in groups]
    q_sel = [jnp.concatenate([q_main[r], q_bias], axis=1).astype(BF16) for r in groups]
    tpos = qb * TQ + lax.broadcasted_iota(jnp.int32, (TQ, 1), 0)

    def flash_step(carry, scores, v):
        out = []
        for (m, acc), sc in zip(carry, scores):
            m_new = jnp.maximum(m, jnp.max(sc, axis=-1, keepdims=True))
            pr = jnp.exp2(sc - m_new).astype(BF16)
            out.append((m_new, jnp.exp2(m - m_new) * acc + _dot(pr, v)))
        return tuple(out)

    def finish(carry):
        return [(acc[:, :HEAD_DIM], acc[:, HEAD_DIM:HEAD_DIM + 1]) for _, acc in carry]

    init = tuple((jnp.full((TQ, 1), M_FLOOR, F32), jnp.zeros((TQ, LANES), F32)) for _ in groups)

    def sel_scores(r, keep=None):
        k = ks_ref[r, :]
        sc = [_dot_nt(qg, k) for qg in q_sel]
        return sc if keep is None else [jnp.where(keep, s_, NEG_INF) for s_ in sc]

    def key_rows(kt):
        return pl.ds(pl.multiple_of(kt * KEY_TILE, KEY_TILE), KEY_TILE)

    def sel_body(kt, carry):
        return flash_step(carry, sel_scores(key_rows(kt)), vs_ref[key_rows(kt), :])

    n_full = (qb * TQ) // KEY_TILE
    carry = lax.fori_loop(0, n_full, sel_body, init)
    sc = sel_scores(key_rows(n_full))
    kpos = n_full * KEY_TILE + lax.broadcasted_iota(jnp.int32, (1, KEY_TILE), 1)
    sc = [jnp.where(kpos <= tpos, s_, NEG_INF) for s_ in sc]
    o_s = finish(flash_step(carry, sc, vs_ref[key_rows(n_full), :]))

    n_win = WINDOW // TQ + 1
    lane_q = lax.broadcasted_iota(jnp.int32, (1, TQ), 1)
    win_sc = [[] for _ in groups]
    win_v = []
    for rel in range(n_win):
        kt = qb - (n_win - 1) + rel
        rows_k = pl.ds(pl.multiple_of(jnp.maximum(kt, 0) * TQ, TQ), TQ)
        k = kw_ref[rows_k, :]
        win_v.append(vw_ref[rows_k, :])
        dist = tpos - (kt * TQ + lane_q)
        keep = jnp.full((1, TQ), kt, jnp.int32) >= 0
        if rel == 0:
            keep = keep & (dist < WINDOW)
        if rel == n_win - 1:
            keep = keep & (dist >= 0)
        for g, qg in enumerate(q_win):
            win_sc[g].append(jnp.where(keep, _dot_nt(qg, k), NEG_INF))
    o_w = finish(flash_step(init, [jnp.concatenate(s_, axis=1) for s_ in win_sc], jnp.concatenate(win_v, axis=0)))

    gates = _pick_head_gates(misc_ref[:, 0:GATE_W], n)
    outs = []
    for g, r in enumerate(groups):
        (u_s, l_s), (u_w, l_w) = o_s[g], o_w[g]
        outs.append(gates[:, 3 * g:3 * g + 1] * o_c[r] + (gates[:, 3 * g + 1:3 * g + 2] / l_s) * u_s
                    + (gates[:, 3 * g + 2:3 * g + 3] / l_w) * u_w)
    o_ref[...] = jnp.concatenate(outs, axis=1)


def _nsa_prompt(proj, kcv, slopes, b, t):
    n16 = t // CMP_STRIDE
    n_sel = t // SEL_BLOCK
    nqb = t // TQ
    assert n_sel <= LANES and t % KEY_TILE == 0
    k_sel, v_sel, k_win, v_win = _kv_prep(proj, b, t)
    seq = lambda w: pl.BlockSpec((None, None, t, w), lambda i, n, qb, sl: (i, n, 0, 0))
    grid_spec = pltpu.PrefetchScalarGridSpec(
        num_scalar_prefetch=1,
        grid=(b, NSA_KV_HEADS, nqb),
        in_specs=[pl.BlockSpec((TQ, Q_SLAB), lambda i, n, qb, sl: (i * nqb + qb, COL_Q // Q_SLAB + n)),
                  pl.BlockSpec((None, None, n16, KV_SLAB), lambda i, n, qb, sl: (i, n, 0, 0)),
                  seq(2 * LANES), seq(LANES), seq(LANES), seq(LANES),
                  pl.BlockSpec((TQ, LANES), lambda i, n, qb, sl: (i * nqb + qb, COL_MISC // LANES)),
                  pl.BlockSpec((NSA_HEADS, LANES), lambda i, n, qb, sl: (0, 0)),
                  pl.BlockSpec((n_sel, n16), lambda i, n, qb, sl: (0, 0))],
        out_specs=pl.BlockSpec((TQ, Q_SLAB), lambda i, n, qb, sl: (i * nqb + qb, n)),
    )
    overlap_t = jnp.asarray(_overlap_counts(np.arange(n16)[None, :], np.arange(n_sel)[:, None]))
    return pl.pallas_call(
        functools.partial(_nsa_prompt_kernel, n16=n16, n_sel=n_sel),
        grid_spec=grid_spec,
        out_shape=jax.ShapeDtypeStruct((b * t, NSA_WIDTH), F32),
        compiler_params=_cparams(("parallel", "parallel", "arbitrary")),
        name="nsa_prompt",
    )(slopes, proj, kcv, k_sel, v_sel, k_win, v_win, proj, _slope_features(), overlap_t)


def _gla_kernel(q_ref, k_ref, v_ref, r_ref, misc_ref, wa_ref, ba_ref, gout_ref, s0_ref, o_ref, sfin_ref, state_ref,
                *, chunk, sub, valid_rows):
    c = pl.program_id(1)

    @pl.when(c == 0)
    def _():
        state_ref[...] = s0_ref[...]

    ga = misc_ref[:, MISC_GA:MISC_GA + GLA_RANK]
    x = jnp.dot(ga, wa_ref[...], preferred_element_type=F32, precision=HIGHEST) + ba_ref[...]
    lg = (jnp.minimum(x, 0.0) - jnp.log1p(jnp.exp(-jnp.abs(x)))) / GLA_TAU
    row = lax.broadcasted_iota(jnp.int32, (chunk, 1), 0)
    if valid_rows < chunk:
        lg = jnp.where(row < valid_rows, lg, 0.0)
    tri = (lax.broadcasted_iota(jnp.int32, (chunk, chunk), 0)
           >= lax.broadcasted_iota(jnp.int32, (chunk, chunk), 1)).astype(F32)
    cum_all = jnp.dot(tri, lg, preferred_element_type=F32, precision=HIGHEST)
    for h in range(GLA_HEADS):
        kcols = slice(h * GLA_DK, (h + 1) * GLA_DK)
        vcols = slice(h * GLA_DV, (h + 1) * GLA_DV)
        o, new_state = _gla_head_chunk(q_ref[:, kcols] * (GLA_DK ** -0.5), k_ref[:, kcols], v_ref[:, vcols],
                                       cum_all[:, kcols], state_ref[h], chunk, sub)
        state_ref[h] = new_state
        rg = r_ref[:, vcols]
        o_ref[:, vcols] = _rms(o, gout_ref[...]) * (rg * _sigmoid(rg))

    @pl.when(c == pl.num_programs(1) - 1)
    def _():
        sfin_ref[...] = state_ref[...]


def _gla_head_chunk(q, k, v, cum, state, chunk, sub):
    vb = v.astype(BF16)
    inter = _dot((q * jnp.exp(cum)).astype(BF16), state.astype(BF16))

    outs = []
    for i in range(chunk // sub):
        r0 = i * sub
        qi, ki, ci, vi = q[r0:r0 + sub], k[r0:r0 + sub], cum[r0:r0 + sub], v[r0:r0 + sub]
        o_i = inter[r0:r0 + sub]
        if i > 0:
            anchor = cum[r0:r0 + 1]
            qd = (qi * jnp.exp(ci - anchor)).astype(BF16)
            kd = (k[0:r0] * jnp.exp(anchor - cum[0:r0])).astype(BF16)
            o_i = o_i + _dot(_dot_nt(qd, kd).astype(BF16), vb[0:r0])
        trow = lax.broadcasted_iota(jnp.int32, (sub, 1), 0)
        for s_ in range(sub):
            w = jnp.sum(qi * (ki[s_:s_ + 1] * jnp.exp(jnp.minimum(ci - ci[s_:s_ + 1], 0.0))), axis=-1, keepdims=True)
            o_i = o_i + jnp.where(trow >= s_, w, 0.0) * vi[s_:s_ + 1]
        outs.append(o_i)
    o = jnp.concatenate(outs, axis=0) if len(outs) > 1 else outs[0]

    last = cum[chunk - 1:chunk]
    kdec = (k * jnp.exp(last - cum)).astype(BF16)
    decay_col = jnp.transpose(jnp.broadcast_to(jnp.exp(last), (SUBLANES, GLA_DK)))[:, 0:1]
    return o, decay_col * state + _dot_tn(kdec, vb)


def _gla(proj, w_a2, b_a, g_out, s0, b, t, chunk, sub, valid_rows):
    nck = t // chunk
    kw = GLA_HEADS * GLA_DK
    state_spec = pl.BlockSpec((None, GLA_HEADS, GLA_DK, GLA_DV), lambda i, c: (i, 0, 0, 0))
    return pl.pallas_call(
        functools.partial(_gla_kernel, chunk=chunk, sub=sub, valid_rows=valid_rows),
        grid=(b, nck),
        in_specs=[pl.BlockSpec((chunk, kw), lambda i, c: (i * nck + c, COL_GQ // kw)),
                  pl.BlockSpec((chunk, kw), lambda i, c: (i * nck + c, COL_GK // kw)),
                  pl.BlockSpec((chunk, GLA_WIDTH), lambda i, c: (i * nck + c, COL_GV // GLA_WIDTH)),
                  pl.BlockSpec((chunk, GLA_WIDTH), lambda i, c: (i * nck + c, COL_GR // GLA_WIDTH)),
                  pl.BlockSpec((chunk, LANES), lambda i, c: (i * nck + c, COL_MISC // LANES)),
                  pl.BlockSpec((GLA_RANK, kw), lambda i, c: (0, 0)),
                  pl.BlockSpec((1, kw), lambda i, c: (0, 0)),
                  pl.BlockSpec((1, GLA_DV), lambda i, c: (0, 0)),
                  state_spec],
        out_specs=[pl.BlockSpec((chunk, GLA_WIDTH), lambda i, c: (i * nck + c, 0)), state_spec],
        out_shape=[jax.ShapeDtypeStruct((b * t, GLA_WIDTH), F32),
                   jax.ShapeDtypeStruct((b, GLA_HEADS, GLA_DK, GLA_DV), F32)],
        scratch_shapes=[pltpu.VMEM((GLA_HEADS, GLA_DK, GLA_DV), F32)],
        compiler_params=_cparams(("parallel", "arbitrary")),
        name="gla",
    )(proj, proj, proj, proj, proj, w_a2, b_a, g_out, s0)


def _out_proj_kernel(on_ref, og_ref, x_ref, g_ref, w_ref, o_ref, a_ref):
    @pl.when(pl.program_id(1) == 0)
    def _():
        a_ref[:, :NSA_WIDTH] = _rms(on_ref[...], g_ref[...]).astype(BF16)
        a_ref[:, NSA_WIDTH:] = og_ref[...].astype(BF16)

    o_ref[...] = x_ref[...] + _dot(a_ref[...], w_ref[...])


def _out_proj(o_nsa, o_gla, x, g_nsa, w_out, tm, tn):
    m = x.shape[0]
    return pl.pallas_call(
        _out_proj_kernel,
        grid=(m // tm, D_MODEL // tn),
        in_specs=[pl.BlockSpec((tm, NSA_WIDTH), lambda i, j: (i, 0)),
                  pl.BlockSpec((tm, GLA_WIDTH), lambda i, j: (i, 0)),
                  pl.BlockSpec((tm, tn), lambda i, j: (i, j)),
                  pl.BlockSpec((1, NSA_WIDTH), lambda i, j: (0, 0)),
                  pl.BlockSpec((NSA_WIDTH + GLA_WIDTH, tn), lambda i, j: (0, j))],
        out_specs=pl.BlockSpec((tm, tn), lambda i, j: (i, j)),
        out_shape=jax.ShapeDtypeStruct((m, D_MODEL), F32),
        scratch_shapes=[pltpu.VMEM((tm, NSA_WIDTH + GLA_WIDTH), BF16)],
        compiler_params=_cparams(("parallel", "arbitrary")),
        name="out_proj",
    )(o_nsa, o_gla, x, g_nsa, w_out)


def _ffn_kernel(h_ref, g_ref, wa_ref, wg_ref, cw_ref, cb_ref, wd_ref, p1_ref, p2_ref, o_ref, tail_ref,
                n2_ref, acc_ref, carry_ref, *, tm, tf, seq_rows):
    i = pl.program_id(0)
    j = pl.program_id(1)

    @pl.when(j == 0)
    def _():
        n2_ref[...] = _rms(h_ref[...], g_ref[...]).astype(BF16)
        acc_ref[...] = jnp.zeros_like(acc_ref)

    n2 = n2_ref[...]
    a = _dot(n2, wa_ref[...])
    gate = _dot(n2, wg_ref[...])
    row = lax.broadcasted_iota(jnp.int32, (tm, 1), 0)
    r1 = pltpu.roll(a, 1, axis=0)
    r2 = pltpu.roll(a, 2, axis=0)
    if seq_rows >= tm:
        cols = pl.ds(pl.multiple_of(j * tf, tf), tf)
        first = (i % (seq_rows // tm)) == 0
        prev = jnp.where(first, p2_ref[...], carry_ref[:, cols])
        a1 = jnp.where(row == 0, prev[1:2], r1)
        a2 = jnp.where(row == 0, prev[0:1], jnp.where(row == 1, prev[1:2], r2))
        carry_ref[:, cols] = a[tm - 2:tm]
    else:
        t = row % seq_rows
        a1 = jnp.where(t == 0, p1_ref[...], r1)
        a2 = jnp.where(t < 2, p2_ref[...], r2)
    cw = cw_ref[...]
    conv = cb_ref[...] + a2 * cw[0:1] + a1 * cw[1:2] + a * cw[2:3]
    y = (_gelu_tanh(conv) * gate).astype(BF16)
    acc_ref[...] += _dot(y, wd_ref[...])
    tail_ref[...] = a[tm - tail_ref.shape[0]:tm]

    @pl.when(j == pl.num_programs(1) - 1)
    def _():
        o_ref[...] = h_ref[...] + acc_ref[...]


def _ffn(h, g_ffn, w_up, conv_w, conv_b, w_down, p1, p2, tm, tf, seq_rows):
    m = h.shape[0]
    nj = D_FF // tf
    if seq_rows >= tm:
        tiles_per_seq = seq_rows // tm
        p1_spec = pl.BlockSpec((None, CONV_W - 1, tf), lambda i, j: (i // tiles_per_seq, 0, j))
        p2_spec = pl.BlockSpec((None, CONV_W - 1, tf), lambda i, j: (i // tiles_per_seq, 0, j))
    else:
        p1_spec = pl.BlockSpec((tm, tf), lambda i, j: (i, j))
        p2_spec = pl.BlockSpec((tm, tf), lambda i, j: (i, j))
    if seq_rows >= tm:
        tail_spec = pl.BlockSpec((None, SUBLANES, tf), lambda i, j: (i, 0, j))
        tail_shape = jax.ShapeDtypeStruct((m // tm, SUBLANES, D_FF), F32)
    else:
        tail_spec = pl.BlockSpec((tm, tf), lambda i, j: (i, j))
        tail_shape = jax.ShapeDtypeStruct((m, D_FF), F32)
    return pl.pallas_call(
        functools.partial(_ffn_kernel, tm=tm, tf=tf, seq_rows=seq_rows),
        grid=(m // tm, nj),
        in_specs=[pl.BlockSpec((tm, D_MODEL), lambda i, j: (i, 0)),
                  pl.BlockSpec((1, D_MODEL), lambda i, j: (0, 0)),
                  pl.BlockSpec((D_MODEL, tf), lambda i, j: (0, j)),
                  pl.BlockSpec((D_MODEL, tf), lambda i, j: (0, nj + j)),
                  pl.BlockSpec((CONV_W, tf), lambda i, j: (0, j)),
                  pl.BlockSpec((1, tf), lambda i, j: (0, j)),
                  pl.BlockSpec((tf, D_MODEL), lambda i, j: (j, 0)),
                  p1_spec, p2_spec],
        out_specs=[pl.BlockSpec((tm, D_MODEL), lambda i, j: (i, 0)),
                   tail_spec],
        out_shape=[jax.ShapeDtypeStruct((m, D_MODEL), F32), tail_shape],
        scratch_shapes=[pltpu.VMEM((tm, D_MODEL), BF16), pltpu.VMEM((tm, D_MODEL), F32),
                        pltpu.VMEM((CONV_W - 1, D_FF), F32)],
        compiler_params=_cparams(("arbitrary", "arbitrary")),
        name="conv_ffn",
    )(h, g_ffn, w_up, w_up, conv_w, conv_b, w_down, p1, p2)


def _ple_kernel(h_ref, p_ref, wg_ref, wp_ref, gp_ref, gf_ref, o_ref, *, final_norm):
    h = h_ref[...]
    gate = _sigmoid(_dot(h.astype(BF16), wg_ref[...]))
    pe = _rms(_dot(p_ref[...].astype(BF16), wp_ref[...]), gp_ref[...])
    h = h + gate * pe
    o_ref[...] = _rms(h, gf_ref[...]) if final_norm else h


def _ple(h, p, w_gate, w_proj, g_ple, g_final, tm, final_norm):
    m = h.shape[0]
    return pl.pallas_call(
        functools.partial(_ple_kernel, final_norm=final_norm),
        grid=(m // tm,),
        in_specs=[pl.BlockSpec((tm, D_MODEL), lambda i: (i, 0)),
                  pl.BlockSpec((tm, PLE_DIM), lambda i: (i, 0)),
                  pl.BlockSpec((D_MODEL, D_MODEL), lambda i: (0, 0)),
                  pl.BlockSpec((PLE_DIM, D_MODEL), lambda i: (0, 0)),
                  pl.BlockSpec((1, D_MODEL), lambda i: (0, 0)),
                  pl.BlockSpec((1, D_MODEL), lambda i: (0, 0))],
        out_specs=pl.BlockSpec((tm, D_MODEL), lambda i: (i, 0)),
        out_shape=jax.ShapeDtypeStruct((m, D_MODEL), F32),
        compiler_params=_cparams(("parallel",)),
        name="ple_norm",
    )(h, p, w_gate, w_proj, g_ple, g_final)


CMP_PAGES = 16
CHUNKS_PER_PAGE = PAGE_SIZE // CMP_STRIDE


def _cmp_sample_kernel(pt_ref, *refs):
    page_refs = refs[:CMP_PAGES]
    perm_ref, w1_ref, w1p_ref, w2_ref, pe_ref, o_ref, carry_ref, x_ref = refs[CMP_PAGES:]
    grp = pl.program_id(1)
    nck = CMP_PAGES * CHUNKS_PER_PAGE

    @pl.when(grp == 0)
    def _():
        carry_ref[...] = jnp.zeros_like(carry_ref)

    perm = perm_ref[...]
    for k, r in enumerate(page_refs):
        xp = _dot_nt(perm, r[...].astype(BF16))
        for s in range(CMP_STRIDE):
            for n in range(NSA_KV_HEADS):
                x_ref[s, n, k * CHUNKS_PER_PAGE:(k + 1) * CHUNKS_PER_PAGE, :] = (
                    xp[s * CHUNKS_PER_PAGE:(s + 1) * CHUNKS_PER_PAGE, n * KV_SLAB:(n + 1) * KV_SLAB])
    acc = jnp.zeros((NSA_KV_HEADS * nck, 4 * CMP_HIDDEN), F32)
    for sp in range(CMP_STRIDE // 2):
        xs = jnp.concatenate([x_ref[2 * sp].reshape(NSA_KV_HEADS * nck, KV_SLAB),
                              x_ref[2 * sp + 1].reshape(NSA_KV_HEADS * nck, KV_SLAB)], axis=1)
        acc = acc + _dot(xs.astype(BF16), w1p_ref[sp])
    a = acc[:, :2 * CMP_HIDDEN]
    b = acc[:, 2 * CMP_HIDDEN:]
    a_prev = pltpu.roll(a, 1, axis=0)
    row = lax.broadcasted_iota(jnp.int32, (NSA_KV_HEADS * nck, 1), 0)
    for n in range(NSA_KV_HEADS):
        a_prev = jnp.where(row == n * nck, carry_ref[n:n + 1, :], a_prev)
        carry_ref[n:n + 1, :] = a[(n + 1) * nck - 1:(n + 1) * nck]
    hid = a_prev + b + _cmp_pe_hidden(pe_ref, w1_ref)
    res = _dot(_gelu_tanh(hid).astype(BF16), w2_ref[...])
    for n in range(NSA_KV_HEADS):
        o_ref[n] = res[n * nck:(n + 1) * nck]


def _cmp_sample(cache_c, page_table_flat, w1, w2, pe, b, n_pages):
    n_grp = n_pages // CMP_PAGES
    nck = CMP_PAGES * CHUNKS_PER_PAGE

    def page_spec(k):
        return pl.BlockSpec((None, KV_WIDTH, PAGE_SIZE),
                            lambda i, g, pt: (pt[i * n_pages + g * CMP_PAGES + k], 0, 0))

    r = np.arange(PAGE_SIZE)
    perm_np = np.zeros((PAGE_SIZE, PAGE_SIZE), np.float32)
    perm_np[(r % CMP_STRIDE) * CHUNKS_PER_PAGE + r // CMP_STRIDE, r] = 1.0
    perm = jnp.asarray(perm_np, dtype=BF16)
    w1p = w1.reshape(CMP_STRIDE // 2, 2 * KV_SLAB, 4 * CMP_HIDDEN)

    grid_spec = pltpu.PrefetchScalarGridSpec(
        num_scalar_prefetch=1,
        grid=(b, n_grp),
        in_specs=[page_spec(k) for k in range(CMP_PAGES)]
        + [pl.BlockSpec(perm.shape, lambda i, g, pt: (0, 0)),
           pl.BlockSpec(w1.shape, lambda i, g, pt: (0, 0, 0)),
           pl.BlockSpec(w1p.shape, lambda i, g, pt: (0, 0, 0)),
           pl.BlockSpec(w2.shape, lambda i, g, pt: (0, 0)),
           pl.BlockSpec(pe.shape, lambda i, g, pt: (0, 0))],
        out_specs=pl.BlockSpec((None, NSA_KV_HEADS, nck, KV_SLAB), lambda i, g, pt: (i, 0, g, 0)),
        scratch_shapes=[pltpu.VMEM((SUBLANES, 2 * CMP_HIDDEN), F32),
                        pltpu.VMEM((CMP_STRIDE, NSA_KV_HEADS, nck, KV_SLAB), F32)],
    )
    return pl.pallas_call(
        _cmp_sample_kernel,
        grid_spec=grid_spec,
        out_shape=jax.ShapeDtypeStruct((b, NSA_KV_HEADS, n_pages * CHUNKS_PER_PAGE, KV_SLAB), F32),
        compiler_params=_cparams(("parallel", "arbitrary")),
        name="cmp_sample",
    )(page_table_flat, *([cache_c] * CMP_PAGES), perm, w1, w1p, w2, pe)


def _sel_sample_kernel(slopes_ref, q_ref, kcv_ref, ov_ref, oc_ref, idx_ref, *, past_len, n_sel_pad, t_valid):
    for n in range(NSA_KV_HEADS):
        oc, idx = _sel_sample_head(n, slopes_ref, q_ref[:, n * Q_SLAB:(n + 1) * Q_SLAB] * SCALE, kcv_ref[n],
                                   ov_ref[...], past_len, n_sel_pad, t_valid)
        oc_ref[n] = oc
        idx_ref[n] = idx


def _sel_sample_head(n, slopes_ref, qf, kcv, overlap, past_len, n_sel_pad, t_valid):
    tp = SUBLANES
    rows = NSA_GROUP * tp
    n_rows_c = past_len // CMP_STRIDE
    nb_past = past_len // SEL_BLOCK
    q = jnp.concatenate([qf[:, g * HEAD_DIM:(g + 1) * HEAD_DIM] for g in range(NSA_GROUP)], axis=0).astype(BF16)
    tok = lax.broadcasted_iota(jnp.int32, (rows, 1), 0) % tp
    qpos = past_len + tok
    slope = _slope_col(slopes_ref, n, tp, rows)
    kc = kcv[:, :HEAD_DIM].astype(BF16)
    vc = kcv[:, HEAD_DIM:].astype(BF16)
    s = _dot_nt(q, kc)
    cp = lax.broadcasted_iota(jnp.int32, (1, n_rows_c), 1)
    end = (cp - 1) * CMP_STRIDE + (CMP_BLOCK - 1)
    dist = (qpos - end).astype(F32)
    valid = (cp >= 1) & (dist >= 0)
    s = jnp.where(valid, s - slope * dist, NEG_INF)
    e = jnp.exp(s - jnp.max(s, axis=-1, keepdims=True))
    p = jnp.where(valid, e / jnp.sum(e, axis=-1, keepdims=True), 0.0)
    o_c = _dot(p.astype(BF16), vc)
    oc = jnp.concatenate([o_c[g * tp:(g + 1) * tp] for g in range(NSA_GROUP)], axis=1)
    p_grp = p[0:tp]
    for g in range(1, NSA_GROUP):
        p_grp = p_grp + p[g * tp:(g + 1) * tp]
    imp = jnp.dot(p_grp, overlap, preferred_element_type=F32, precision=HIGHEST)
    n_tail = -(-t_valid // SEL_BLOCK)
    n_sel = nb_past + n_tail
    j = lax.broadcasted_iota(jnp.int32, (tp, n_sel_pad), 1)
    cur = qpos[0:tp] // SEL_BLOCK
    forced = (j == 0) | (j == cur) | (j == cur - 1)
    imp = jnp.where(forced, FORCE_SCORE, jnp.where(j > cur, -FORCE_SCORE, imp))
    imp = jnp.where(j < n_sel, imp, REMOVED_SCORE)
    _, picks = _top_blocks(imp, min(N_SELECT, n_sel))
    kcol = lax.broadcasted_iota(jnp.int32, (tp, N_SELECT), 1)
    idx = jnp.zeros((tp, N_SELECT), jnp.int32)
    for kk, pk in enumerate(picks):
        idx = jnp.where(kcol == kk, pk, idx)
    return oc, idx


def _sel_sample(proj8, kcv, slopes, b, past_len, t_valid):
    n_rows_c = past_len // CMP_STRIDE
    nb_past = past_len // SEL_BLOCK
    n_sel_pad = -(-(nb_past + 1) // LANES) * LANES
    overlap = jnp.asarray(_overlap_counts(np.arange(n_rows_c)[:, None] - 1, np.arange(n_sel_pad)[None, :]))
    grid_spec = pltpu.PrefetchScalarGridSpec(
        num_scalar_prefetch=1,
        grid=(b,),
        in_specs=[pl.BlockSpec((SUBLANES, NSA_WIDTH), lambda i, sl: (i, COL_Q // NSA_WIDTH)),
                  pl.BlockSpec((None, NSA_KV_HEADS, n_rows_c, KV_SLAB), lambda i, sl: (i, 0, 0, 0)),
                  pl.BlockSpec(overlap.shape, lambda i, sl: (0, 0))],
        out_specs=[pl.BlockSpec((None, NSA_KV_HEADS, SUBLANES, Q_SLAB), lambda i, sl: (i, 0, 0, 0)),
                   pl.BlockSpec((None, NSA_KV_HEADS, SUBLANES, N_SELECT), lambda i, sl: (i, 0, 0, 0))],
    )
    return pl.pallas_call(
        functools.partial(_sel_sample_kernel, past_len=past_len, n_sel_pad=n_sel_pad, t_valid=t_valid),
        grid_spec=grid_spec,
        out_shape=[jax.ShapeDtypeStruct((b, NSA_KV_HEADS, SUBLANES, Q_SLAB), F32),
                   jax.ShapeDtypeStruct((b, NSA_KV_HEADS, SUBLANES, N_SELECT), jnp.int32)],
        compiler_params=_cparams(("parallel",)),
        name="sel_sample",
    )(slopes, proj8, kcv, overlap)


def _nsa_sample_kernel(idx_ref, pt_ref, slopes_ref, cache_ref, q_ref, tail_ref, wnew_ref, wcache_ref, oc_ref,
                       misc_ref, o_ref, pages_a, pages_b, sem_ref, *, past_len, t_valid, n_pages):
    n_blk = t_valid * N_SELECT
    nb_past = past_len // SEL_BLOCK
    per_page = PAGE_SIZE // SEL_BLOCK
    pairs = NSA_KV_HEADS // 2
    step = pl.program_id(0) * pairs + pl.program_id(1)
    n_steps = pl.num_programs(0) * pairs
    head_a = step * 2
    head_b = head_a + 1
    next_a = ((step + 1) % n_steps) * 2

    def page_copy(bh, j, buf, sem_i):
        t, kk = j // N_SELECT, j % N_SELECT
        jb = jnp.minimum(idx_ref[(bh * SUBLANES + t) * N_SELECT + kk], nb_past - 1)
        page = pt_ref[(bh // NSA_KV_HEADS) * n_pages + jb // per_page]
        return pltpu.make_async_copy(cache_ref.at[page, bh % NSA_KV_HEADS], buf.at[j], sem_ref.at[sem_i])

    def wait_all(bh, buf, sem_i):
        def body(j, c):
            page_copy(bh, j, buf, sem_i).wait()
            return c
        lax.fori_loop(0, n_blk, body, 0)

    def compute(bh, h2, buf):
        n = bh % NSA_KV_HEADS
        slope = _slope_col(slopes_ref, n, 1, SUBLANES)
        gates = _pick_head_gates(misc_ref[:, 0:GATE_W], n)
        tl = tail_ref[:, h2 * KV_SLAB:(h2 + 1) * KV_SLAB]
        wn = wnew_ref[:, h2 * KV_SLAB:(h2 + 1) * KV_SLAB]
        win_k = wcache_ref[h2, 0:HEAD_DIM, :].astype(BF16)
        win_v = wcache_ref[h2, HEAD_DIM:KV_SLAB, :].astype(BF16)
        rows_out = []
        for t in range(t_valid):
            picked = [buf.at[t * N_SELECT + kk] for kk in range(N_SELECT)]
            jbs = [idx_ref[(bh * SUBLANES + t) * N_SELECT + kk] for kk in range(N_SELECT)]
            rows_out.append(_nsa_sample_token(t, q_ref[t:t + 1, h2 * Q_SLAB:(h2 + 1) * Q_SLAB] * SCALE, slope, picked,
                                              jbs, tl, wn, win_k, win_v, oc_ref[h2, t:t + 1, :], gates[t:t + 1, :],
                                              past_len))
        rows_out.append(jnp.zeros((SUBLANES - t_valid, Q_SLAB), F32))
        o_ref[:, h2 * Q_SLAB:(h2 + 1) * Q_SLAB] = jnp.concatenate(rows_out, axis=0)

    @pl.when(step == 0)
    def _():
        for j in range(n_blk):
            page_copy(head_a, j, pages_a, 0).start()

    for j in range(n_blk):
        page_copy(head_b, j, pages_b, 1).start()
    wait_all(head_a, pages_a, 0)
    compute(head_a, 0, pages_a)
    for j in range(n_blk):
        page_copy(next_a, j, pages_a, 0).start()
    wait_all(head_b, pages_b, 1)
    compute(head_b, 1, pages_b)

    @pl.when(step == n_steps - 1)
    def _():
        wait_all(next_a, pages_a, 0)


def _nsa_sample_token(t, qrow, slope, picked, jbs, tl, wn, win_k, win_v, oc_row, gates, past_len):
    nb_past = past_len // SEL_BLOCK
    g8 = SUBLANES
    q = jnp.concatenate([qrow[:, g * HEAD_DIM:(g + 1) * HEAD_DIM] for g in range(NSA_GROUP)]
                        + [jnp.zeros((g8 - NSA_GROUP, HEAD_DIM), F32)], axis=0).astype(BF16)
    qpos = past_len + t

    per_page = PAGE_SIZE // SEL_BLOCK
    k_all = jnp.concatenate([r[0:HEAD_DIM, :] for r in picked], axis=1).astype(BF16)
    v_all = jnp.concatenate([r[HEAD_DIM:KV_SLAB, :] for r in picked], axis=1).astype(BF16)
    jb_row = jnp.concatenate([jnp.full((1, PAGE_SIZE), jb, jnp.int32) for jb in jbs], axis=1)
    tail_count = jnp.zeros((), jnp.int32)
    for jb in jbs:
        tail_count = tail_count + (jb >= nb_past).astype(jnp.int32)
    lane = lax.broadcasted_iota(jnp.int32, (1, N_SELECT * PAGE_SIZE), 1) % PAGE_SIZE
    kpos = (jb_row // per_page) * PAGE_SIZE + lane
    dist = (qpos - kpos).astype(F32)
    ok = (dist >= 0) & (kpos // SEL_BLOCK == jb_row) & (jb_row < nb_past)
    sc = _dot(q, k_all) - slope * dist
    parts = [(jnp.where(ok, sc, NEG_INF), ok, v_all, True)]
    lane8 = lax.broadcasted_iota(jnp.int32, (1, SUBLANES), 1)
    dist = (t - lane8).astype(F32)
    ok = (dist >= 0) & (jnp.full((1, SUBLANES), tail_count, jnp.int32) > 0)
    sc = _dot_nt(q, tl[:, :HEAD_DIM].astype(BF16)) - slope * dist
    parts.append((jnp.where(ok, sc, NEG_INF), ok, tl[:, HEAD_DIM:].astype(BF16), False))

    def softmax_av(parts):
        m = parts[0][0].max(axis=-1, keepdims=True)
        for sc, _, _, _ in parts[1:]:
            m = jnp.maximum(m, sc.max(axis=-1, keepdims=True))
        l = jnp.zeros((g8, 1), F32)
        acc = jnp.zeros((g8, HEAD_DIM), F32)
        for sc, ok, v, v_transposed in parts:
            pr = jnp.where(ok, jnp.exp(sc - m), 0.0)
            l = l + pr.sum(axis=-1, keepdims=True)
            acc = acc + (_dot_nt(pr.astype(BF16), v) if v_transposed else _dot(pr.astype(BF16), v))
        return acc / l

    o_s = softmax_av(parts)

    buf_len = win_k.shape[1]
    lane_w = lax.broadcasted_iota(jnp.int32, (1, buf_len), 1)
    dist_c = (qpos - (past_len - buf_len + lane_w)).astype(F32)
    ok_c = (dist_c >= 0) & (dist_c < WINDOW)
    sc_c = jnp.where(ok_c, _dot(q, win_k) - slope * dist_c, NEG_INF)
    dist_n = (t - lane8).astype(F32)
    ok_n = (dist_n >= 0) & (dist_n < WINDOW)
    sc_n = jnp.where(ok_n, _dot_nt(q, wn[:, :HEAD_DIM].astype(BF16)) - slope * dist_n, NEG_INF)
    o_w = softmax_av([(sc_c, ok_c, win_v, True), (sc_n, ok_n, wn[:, HEAD_DIM:].astype(BF16), False)])

    outs = []
    for g in range(NSA_GROUP):
        outs.append(gates[:, 3 * g:3 * g + 1] * oc_row[:, g * HEAD_DIM:(g + 1) * HEAD_DIM]
                    + gates[:, 3 * g + 1:3 * g + 2] * o_s[g:g + 1]
                    + gates[:, 3 * g + 2:3 * g + 3] * o_w[g:g + 1])
    return jnp.concatenate(outs, axis=1)


def _nsa_sample(proj8, cache_s, cache_w, o_c, idx_flat, page_table_flat, slopes, b, t_valid, past_len):
    n_pages = past_len // PAGE_SIZE
    buf_len = cache_w.shape[-1]
    pairs = NSA_KV_HEADS // 2
    n_blk = t_valid * N_SELECT
    pages = pltpu.VMEM((n_blk, KV_SLAB, PAGE_SIZE), F32)
    grid_spec = pltpu.PrefetchScalarGridSpec(
        num_scalar_prefetch=3,
        grid=(b, pairs),
        in_specs=[pl.BlockSpec(memory_space=pl.ANY),
                  pl.BlockSpec((SUBLANES, 2 * Q_SLAB), lambda i, p, *_: (i, COL_Q // (2 * Q_SLAB) + p)),
                  pl.BlockSpec((SUBLANES, 2 * KV_SLAB), lambda i, p, *_: (i, COL_KVS // (2 * KV_SLAB) + p)),
                  pl.BlockSpec((SUBLANES, 2 * KV_SLAB), lambda i, p, *_: (i, COL_KVW // (2 * KV_SLAB) + p)),
                  pl.BlockSpec((None, 2, KV_SLAB, buf_len), lambda i, p, *_: (i, p, 0, 0)),
                  pl.BlockSpec((None, 2, SUBLANES, Q_SLAB), lambda i, p, *_: (i, p, 0, 0)),
                  pl.BlockSpec((SUBLANES, LANES), lambda i, p, *_: (i, COL_MISC // LANES))],
        out_specs=pl.BlockSpec((SUBLANES, 2 * Q_SLAB), lambda i, p, *_: (i, p)),
        scratch_shapes=[pages, pages, pltpu.SemaphoreType.DMA((2,))],
    )
    return pl.pallas_call(
        functools.partial(_nsa_sample_kernel, past_len=past_len, t_valid=t_valid, n_pages=n_pages),
        grid_spec=grid_spec,
        out_shape=jax.ShapeDtypeStruct((b * SUBLANES, NSA_WIDTH), F32),
        compiler_params=_cparams(("arbitrary", "arbitrary")),
        name="nsa_sample",
    )(idx_flat, page_table_flat, slopes, cache_s, proj8, proj8, proj8, cache_w, o_c, proj8)


def _alibi_slopes():
    h = np.arange(1, NSA_HEADS + 1, dtype=np.float32)
    return jnp.asarray(2.0 ** (-8.0 * h / NSA_HEADS), dtype=F32)


def _permute_w_in(w_in):
    offs = np.cumsum([0, NSA_WIDTH, KV_WIDTH, KV_WIDTH, KV_WIDTH, GATE_W, GLA_HEADS * GLA_DK, GLA_HEADS * GLA_DK,
                      GLA_WIDTH, GLA_WIDTH, GLA_RANK])
    piece = [w_in[:, offs[k]:offs[k + 1]] for k in range(10)]
    q, kvc, kvs, kvw, gl, gq, gk, gv, gr, ga = piece
    pad = jnp.zeros((w_in.shape[0], LANES - GATE_W - GLA_RANK), w_in.dtype)
    return jnp.concatenate([q, gv, gr, kvc, kvs, kvw, gq, gk, gl, ga, pad], axis=1).astype(BF16)


def _row_tile(m, pref):
    return pref if m % pref == 0 else m


def _dense_tail(h_in, o_nsa, o_gla, p_emb, wts, conv_p1, conv_p2, seq_rows, last_layer, tm):
    (g_nsa, w_out, g_ffn, w_up, conv_w, conv_b, w_down, w_ple_proj, g_ple, w_ple_gate, g_final) = wts
    h = _out_proj(o_nsa, o_gla, h_in, g_nsa, w_out, tm, D_MODEL // 2)
    h, tails = _ffn(h, g_ffn, w_up, conv_w, conv_b, w_down, conv_p1, conv_p2, tm, FFN_COLS, seq_rows)
    y = _ple(h, p_emb, w_ple_gate, w_ple_proj, g_ple, g_final, tm, last_layer)
    return y, tails


def kernel(x_prompt, x_sample, p_prompt, p_sample, cache_cmp_kv, cache_sel_kv, cache_win_kv, state_gla, state_ffn_conv, page_table, g_attn, w_in, w_cmp_k1, w_cmp_k2, pe_cmp_k, w_cmp_v1, w_cmp_v2, pe_cmp_v, w_gla_a2, b_gla_a, g_nsa_out, g_gla_out, w_out, g_ffn, w_up, conv_w, conv_b, w_down, w_ple_proj, g_ple, w_ple_gate, g_final):
    depth = w_in.shape[0]
    bp, tp, _ = x_prompt.shape
    bs, ts, _ = x_sample.shape
    n_pages = page_table.shape[1]
    past_len = n_pages * PAGE_SIZE
    n_pool = cache_cmp_kv.shape[1]
    assert tp % TQ == 0 and tp >= WINDOW and ts <= SUBLANES and ts <= SEL_BLOCK and ts >= CONV_W - 1
    assert n_pages % CMP_PAGES == 0 and tp // SEL_BLOCK >= N_SELECT
    slopes = _alibi_slopes()
    pt_flat = page_table.reshape(-1).astype(jnp.int32)
    kv_shape = (NSA_KV_HEADS, 2, HEAD_DIM)

    hp = x_prompt.reshape(bp * tp, D_MODEL)
    hs = x_sample.reshape(bs * ts, D_MODEL)
    new_p = [[] for _ in range(5)]
    new_s = [[] for _ in range(5)]
    tm_p = _row_tile(bp * tp, DENSE_ROWS)
    tm_s = bs * ts
    for i in range(depth):
        last = i == depth - 1
        w_in_p = _permute_w_in(w_in[i])
        g_a = g_attn[i].reshape(1, D_MODEL)
        w1c, w2c, pec = _cmp_weights(w_cmp_k1[i], w_cmp_k2[i], pe_cmp_k[i], w_cmp_v1[i], w_cmp_v2[i], pe_cmp_v[i])
        b_a = b_gla_a[i].reshape(1, -1)
        g_go = g_gla_out[i].reshape(1, GLA_DV)
        wts = (g_nsa_out[i].reshape(1, -1), w_out[i].astype(BF16), g_ffn[i].reshape(1, -1), w_up[i].astype(BF16),
               conv_w[i], conv_b[i].reshape(1, -1), w_down[i].astype(BF16), w_ple_proj[i].astype(BF16),
               g_ple[i].reshape(1, -1), w_ple_gate[i].astype(BF16), g_final.reshape(1, -1))

        proj = _norm_matmul(hp, g_a, w_in_p, _row_tile(bp * tp, IN_PROJ_ROWS), IN_PROJ_COLS)
        kcv = _cmp_prompt(proj, bp, tp, w1c, w2c, pec)
        o_nsa = _nsa_prompt(proj, kcv, slopes, bp, tp)
        s0 = jnp.zeros((bp, GLA_HEADS, GLA_DK, GLA_DV), F32)
        o_gla, s_new = _gla(proj, w_gla_a2[i], b_a, g_go, s0, bp, tp, 64, 16, 64)
        zbuf = jnp.zeros((bp, CONV_W - 1, D_FF), F32)
        hp, tails = _dense_tail(hp, o_nsa, o_gla, p_prompt[i].reshape(bp * tp, PLE_DIM), wts, zbuf, zbuf, tp, last, tm_p)
        proj3 = proj.reshape(bp, tp, D_IN_PAD)
        new_p[0].append(proj3[:, :, COL_KVC:COL_KVC + KV_WIDTH].reshape((bp, tp) + kv_shape))
        new_p[1].append(proj3[:, :, COL_KVS:COL_KVS + KV_WIDTH].reshape((bp, tp) + kv_shape))
        new_p[2].append(proj3[:, tp - WINDOW:, COL_KVW:COL_KVW + KV_WIDTH].reshape((bp, WINDOW) + kv_shape))
        new_p[3].append(s_new)
        tiles_per_seq = tp // tm_p
        new_p[4].append(tails.reshape(bp, tiles_per_seq, SUBLANES, D_FF)[:, -1, SUBLANES - (CONV_W - 1):, :])

        proj_s = _norm_matmul(hs, g_a, w_in_p, tm_s, IN_PROJ_COLS)
        proj8 = jnp.pad(proj_s.reshape(bs, ts, D_IN_PAD), ((0, 0), (0, SUBLANES - ts), (0, 0))).reshape(bs * SUBLANES, D_IN_PAD)
        cache_c = jnp.transpose(cache_cmp_kv[i], (0, 2, 3, 4, 1)).reshape(n_pool, KV_WIDTH, PAGE_SIZE)
        cache_s = jnp.transpose(cache_sel_kv[i], (0, 2, 3, 4, 1)).reshape(n_pool, NSA_KV_HEADS, KV_SLAB, PAGE_SIZE)
        cache_w = jnp.transpose(cache_win_kv[i], (0, 2, 3, 4, 1)).reshape(bs, NSA_KV_HEADS, KV_SLAB, -1)
        kcv_s = _cmp_sample(cache_c, pt_flat, w1c, w2c, pec, bs, n_pages)
        o_c, idx = _sel_sample(proj8, kcv_s, slopes, bs, past_len, ts)
        o_nsa_s = _nsa_sample(proj8, cache_s, cache_w, o_c, idx.reshape(-1), pt_flat, slopes, bs, ts, past_len)
        o_nsa_s = o_nsa_s.reshape(bs, SUBLANES, NSA_WIDTH)[:, :ts].reshape(bs * ts, NSA_WIDTH)
        o_gla_s, s_new_s = _gla(proj8, w_gla_a2[i], b_a, g_go, state_gla[i].astype(F32), bs, SUBLANES, SUBLANES, SUBLANES, ts)
        o_gla_s = o_gla_s.reshape(bs, SUBLANES, GLA_WIDTH)[:, :ts].reshape(bs * ts, GLA_WIDTH)
        buf = state_ffn_conv[i]
        zrow = jnp.zeros((bs, ts - 1, D_FF), F32)
        p1 = jnp.concatenate([buf[:, 1:2], zrow], axis=1).reshape(bs * ts, D_FF)
        p2 = jnp.concatenate([buf, jnp.zeros((bs, ts - 2, D_FF), F32)], axis=1).reshape(bs * ts, D_FF)
        hs, tails_s = _dense_tail(hs, o_nsa_s, o_gla_s, p_sample[i].reshape(bs * ts, PLE_DIM), wts, p1, p2, ts, last, tm_s)
        ps3 = proj_s.reshape(bs, ts, D_IN_PAD)
        new_s[0].append(ps3[:, :, COL_KVC:COL_KVC + KV_WIDTH].reshape((bs, ts) + kv_shape))
        new_s[1].append(ps3[:, :, COL_KVS:COL_KVS + KV_WIDTH].reshape((bs, ts) + kv_shape))
        new_s[2].append(ps3[:, :, COL_KVW:COL_KVW + KV_WIDTH].reshape((bs, ts) + kv_shape))
        new_s[3].append(s_new_s)
        new_s[4].append(tails_s.reshape(bs, ts, D_FF)[:, ts - (CONV_W - 1):, :])

    y_prompt = hp.reshape(bp, tp, D_MODEL)
    y_sample = hs.reshape(bs, ts, D_MODEL)
    cmp_p, sel_p, win_p, gla_p, conv_p = [jnp.stack(l) for l in new_p]
    cmp_s, sel_s, win_s, gla_s, conv_s = [jnp.stack(l) for l in new_s]
    return (y_prompt, y_sample, cmp_p, sel_p, win_p, gla_p, conv_p, cmp_s, sel_s, win_s, gla_s, conv_s)
```

```python
import functools

import numpy as np
import jax
import jax.numpy as jnp
from jax import lax
from jax.experimental import pallas as pl
from jax.experimental.pallas import tpu as pltpu

F32 = jnp.float32
BF16 = jnp.bfloat16
HIGHEST = lax.Precision.HIGHEST

D_MODEL = 2048
PAGE_SIZE = 128
NSA_HEADS = 16
NSA_KV_HEADS = 4
NSA_GROUP = NSA_HEADS // NSA_KV_HEADS
HEAD_DIM = 64
CMP_BLOCK = 32
CMP_STRIDE = 16
CMP_HIDDEN = 2 * HEAD_DIM
SEL_BLOCK = 64
N_SELECT = 16
WINDOW = 512
TQ = 256
GLA_HEADS = 4
GLA_DK = 128
GLA_DV = 256
GLA_RANK = 16
GLA_TAU = 16.0
D_FF = 5632
CONV_W = 3
PLE_DIM = 256
EPS = 1e-6
NEG_INF = -1e30
FORCE_SCORE = 1e9
REMOVED_SCORE = -3e38
SCALE = HEAD_DIM ** -0.5
LOG2E = np.float32(1.4426950408889634)

NSA_WIDTH = NSA_HEADS * HEAD_DIM
GLA_WIDTH = GLA_HEADS * GLA_DV
KV_WIDTH = 2 * NSA_KV_HEADS * HEAD_DIM
KV_SLAB = 2 * HEAD_DIM
Q_SLAB = NSA_GROUP * HEAD_DIM
GATE_W = 3 * NSA_HEADS

LANES = 128
SUBLANES = 8
VMEM_LIMIT = 56 * 1024 * 1024

COL_Q = 0
COL_GV = COL_Q + NSA_WIDTH
COL_GR = COL_GV + GLA_WIDTH
COL_KVC = COL_GR + GLA_WIDTH
COL_KVS = COL_KVC + KV_WIDTH
COL_KVW = COL_KVS + KV_WIDTH
COL_GQ = COL_KVW + KV_WIDTH
COL_GK = COL_GQ + GLA_HEADS * GLA_DK
COL_MISC = COL_GK + GLA_HEADS * GLA_DK
D_IN_PAD = COL_MISC + LANES
MISC_GA = GATE_W

DENSE_ROWS = 512
IN_PROJ_ROWS = 512
IN_PROJ_COLS = D_IN_PAD // 5
FFN_COLS = 512


def _cparams(sem):
    return pltpu.CompilerParams(dimension_semantics=sem, vmem_limit_bytes=VMEM_LIMIT)


def _rms(x, g):
    return x * lax.rsqrt(jnp.mean(x * x, axis=-1, keepdims=True) + EPS) * g


def _gelu_tanh(x):
    return 0.5 * x * (1.0 + jnp.tanh(np.float32(np.sqrt(2.0 / np.pi)) * (x + 0.044715 * (x * x * x))))


def _sigmoid(x):
    return 1.0 / (1.0 + jnp.exp(-x))


def _dot(a, b):
    return jnp.dot(a, b, preferred_element_type=F32)


def _dot_nt(a, b):
    return lax.dot_general(a, b, (((1,), (1,)), ((), ())), preferred_element_type=F32)


def _dot_tn(a, b):
    return lax.dot_general(a, b, (((0,), (0,)), ((), ())), preferred_element_type=F32)


def _norm_matmul_kernel(x_ref, g_ref, w_ref, o_ref, xn_ref):
    @pl.when(pl.program_id(1) == 0)
    def _():
        xn_ref[...] = _rms(x_ref[...], g_ref[...]).astype(BF16)

    o_ref[...] = _dot(xn_ref[...], w_ref[...])


def _norm_matmul(x, g, w, tm, tn):
    m, d = x.shape
    n = w.shape[1]
    return pl.pallas_call(
        _norm_matmul_kernel,
        grid=(m // tm, n // tn),
        in_specs=[pl.BlockSpec((tm, d), lambda i, j: (i, 0)),
                  pl.BlockSpec((1, d), lambda i, j: (0, 0)),
                  pl.BlockSpec((d, tn), lambda i, j: (0, j))],
        out_specs=pl.BlockSpec((tm, tn), lambda i, j: (i, j)),
        out_shape=jax.ShapeDtypeStruct((m, n), F32),
        scratch_shapes=[pltpu.VMEM((tm, d), BF16)],
        compiler_params=_cparams(("parallel", "arbitrary")),
        name="in_proj",
    )(x, g, w)


def _cmp_pe_hidden(pe_ref, w1_ref):
    span = CMP_BLOCK // CMP_STRIDE
    acc = jnp.zeros((SUBLANES, 2 * CMP_HIDDEN), F32)
    for j in range(span):
        for s in range(CMP_STRIDE):
            row = jnp.broadcast_to(pe_ref[pl.ds(j * CMP_STRIDE + s, 1), :], (SUBLANES, KV_SLAB)).astype(BF16)
            acc = acc + _dot(row, w1_ref[s][:, j * 2 * CMP_HIDDEN:(j + 1) * 2 * CMP_HIDDEN])
    return acc[0:1, :]


def _cmp_prompt_kernel(x_ref, w1_ref, w2_ref, pe_ref, o_ref, *, n16):
    acc = jnp.zeros((n16, 4 * CMP_HIDDEN), F32)
    for s in range(CMP_STRIDE):
        xs = x_ref[pl.ds(s, n16, stride=CMP_STRIDE), :].astype(BF16)
        acc = acc + _dot(xs, w1_ref[s])
    a = acc[:, :2 * CMP_HIDDEN]
    b_next = pltpu.roll(acc[:, 2 * CMP_HIDDEN:], n16 - 1, axis=0)
    hid = a + b_next + _cmp_pe_hidden(pe_ref, w1_ref)
    o_ref[...] = _dot(_gelu_tanh(hid).astype(BF16), w2_ref[...])


def _cmp_weights(w_k1, w_k2, pe_k, w_v1, w_v2, pe_v):
    z = jnp.zeros((CMP_BLOCK, HEAD_DIM, CMP_HIDDEN), F32)
    wl = jnp.concatenate([jnp.concatenate([w_k1, z], axis=2), jnp.concatenate([z, w_v1], axis=2)], axis=1)
    w1 = jnp.concatenate([wl[:CMP_STRIDE], wl[CMP_STRIDE:]], axis=2).astype(BF16)
    z2 = jnp.zeros((CMP_HIDDEN, HEAD_DIM), F32)
    w2 = jnp.concatenate([jnp.concatenate([w_k2, z2], axis=1), jnp.concatenate([z2, w_v2], axis=1)], axis=0)
    pe = jnp.concatenate([pe_k, pe_v], axis=1)
    return w1, w2.astype(BF16), pe


def _cmp_prompt(proj, b, t, w1, w2, pe):
    n16 = t // CMP_STRIDE
    col0 = COL_KVC // KV_SLAB
    return pl.pallas_call(
        functools.partial(_cmp_prompt_kernel, n16=n16),
        grid=(b, NSA_KV_HEADS),
        in_specs=[pl.BlockSpec((t, KV_SLAB), lambda i, n: (i, col0 + n)),
                  pl.BlockSpec(w1.shape, lambda i, n: (0, 0, 0)),
                  pl.BlockSpec(w2.shape, lambda i, n: (0, 0)),
                  pl.BlockSpec(pe.shape, lambda i, n: (0, 0))],
        out_specs=pl.BlockSpec((None, None, n16, KV_SLAB), lambda i, n: (i, n, 0, 0)),
        out_shape=jax.ShapeDtypeStruct((b, NSA_KV_HEADS, n16, KV_SLAB), F32),
        compiler_params=_cparams(("parallel", "parallel")),
        name="cmp_prompt",
    )(proj, w1, w2, pe)


def _overlap_counts(c, j):
    per_sel = SEL_BLOCK // CMP_STRIDE
    ov = sum(((c + k >= per_sel * j) & (c + k < per_sel * (j + 1))) for k in range(CMP_BLOCK // CMP_STRIDE))
    return np.where(c >= 0, ov, 0).astype(np.float32)


def _top_blocks(imp, n_pick):
    rows, nsel = imp.shape
    j = lax.broadcasted_iota(jnp.int32, (rows, nsel), 1).astype(F32)
    mask = jnp.zeros((rows, nsel), F32)
    picks = []
    for _ in range(n_pick):
        m = jnp.max(imp, axis=-1, keepdims=True)
        jmin = jnp.min(jnp.where(imp == m, j, float(nsel)), axis=-1, keepdims=True)
        hit = j == jmin
        mask = jnp.where(hit, 1.0, mask)
        imp = jnp.where(hit, REMOVED_SCORE, imp)
        picks.append(jmin.astype(jnp.int32))
    return mask, picks


def _slope_col(slopes_ref, n, rows_per_head, rows):
    g = lax.broadcasted_iota(jnp.int32, (rows, 1), 0) // rows_per_head
    col = jnp.zeros((rows, 1), F32)
    for gg in range(NSA_GROUP):
        col = jnp.where(g == gg, slopes_ref[n * NSA_GROUP + gg], col)
    return col


def _pick_head_gates(gl, n):
    out = jnp.zeros((gl.shape[0], 3 * NSA_GROUP), F32)
    for nn in range(NSA_KV_HEADS):
        out = jnp.where(n == nn, gl[:, nn * 3 * NSA_GROUP:(nn + 1) * 3 * NSA_GROUP], out)
    return _sigmoid(out)


MASK_BIAS = -131072.0
M_FLOOR = -65536.0
POS_HI, POS_LO = HEAD_DIM, HEAD_DIM + 3
KEY_TILE = 2 * TQ


def _slope_features():
    h = np.arange(1, NSA_HEADS + 1, dtype=np.float32)
    slopes = (2.0 ** (-8.0 * h / NSA_HEADS)).astype(np.float32)
    tab = np.zeros((NSA_HEADS, LANES), np.float32)
    rest = slopes * LOG2E
    for c in range(3):
        piece = rest.astype(BF16).astype(np.float32)
        tab[:, POS_HI + c] = piece
        tab[:, POS_LO + c] = piece
        rest = rest - piece
    assert not rest.any()
    return jnp.asarray(tab)


def _kv_prep_kernel(kvs_ref, kvw_ref, ks_ref, vs_ref, kw_ref, vw_ref, *, t):
    lane = lax.broadcasted_iota(jnp.int32, (t, LANES), 1)
    pos = lax.broadcasted_iota(jnp.int32, (t, LANES), 0)
    hi = ((pos // SEL_BLOCK) * SEL_BLOCK).astype(F32)
    lo = (pos % SEL_BLOCK).astype(F32)
    feat = jnp.where((lane >= POS_HI) & (lane < POS_HI + 3), hi,
                     jnp.where((lane >= POS_LO) & (lane < POS_LO + 3), lo, 0.0))
    ones_col = jnp.where(lane == HEAD_DIM, 1.0, 0.0)
    for src, k_out, v_out in ((kvs_ref, ks_ref, vs_ref), (kvw_ref, kw_ref, vw_ref)):
        x = src[...]
        k_out[:, 0:LANES] = jnp.where(lane < HEAD_DIM, x, feat).astype(BF16)
        v_out[...] = jnp.where(lane < HEAD_DIM, pltpu.roll(x, HEAD_DIM, axis=1), ones_col).astype(BF16)
    ks_ref[:, LANES:2 * LANES] = jnp.where(lane == pos // SEL_BLOCK, 1.0, 0.0).astype(BF16)


def _kv_prep(proj, b, t):
    spec_in = lambda col: pl.BlockSpec((t, KV_SLAB), lambda i, n: (i, col // KV_SLAB + n))
    spec_out = lambda w: pl.BlockSpec((None, None, t, w), lambda i, n: (i, n, 0, 0))
    shape = lambda w: jax.ShapeDtypeStruct((b, NSA_KV_HEADS, t, w), BF16)
    return pl.pallas_call(
        functools.partial(_kv_prep_kernel, t=t),
        grid=(b, NSA_KV_HEADS),
        in_specs=[spec_in(COL_KVS), spec_in(COL_KVW)],
        out_specs=[spec_out(2 * LANES), spec_out(LANES), spec_out(LANES), spec_out(LANES)],
        out_shape=[shape(2 * LANES), shape(LANES), shape(LANES), shape(LANES)],
        compiler_params=_cparams(("parallel", "parallel")),
        name="kv_prep",
    )(proj, proj)


def _nsa_prompt_kernel(slopes_ref, q_ref, kcv_ref, ks_ref, vs_ref, kw_ref, vw_ref, misc_ref, sfeat_ref, ovt_ref, o_ref,
                       *, n16, n_sel):
    n = pl.program_id(1)
    qb = pl.program_id(2)
    rows = NSA_GROUP * TQ
    qf = q_ref[...] * SCALE
    q = jnp.concatenate([qf[:, g * HEAD_DIM:(g + 1) * HEAD_DIM] for g in range(NSA_GROUP)], axis=0).astype(BF16)
    tok = lax.broadcasted_iota(jnp.int32, (rows, 1), 0) % TQ
    qpos = qb * TQ + tok
    slope = _slope_col(slopes_ref, n, TQ, rows)

    kcv = kcv_ref[...]
    kc = kcv[:, :HEAD_DIM].astype(BF16)
    vc = kcv[:, HEAD_DIM:].astype(BF16)
    s = _dot_nt(q, kc)
    end = lax.broadcasted_iota(jnp.int32, (1, n16), 1) * CMP_STRIDE + (CMP_BLOCK - 1)
    dist = (qpos - end).astype(F32)
    valid = (dist >= 0) & (end < n16 * CMP_STRIDE)
    s = jnp.where(valid, s - slope * dist, NEG_INF)
    e = jnp.exp(s - jnp.max(s, axis=-1, keepdims=True))
    p = jnp.where(valid, e / jnp.sum(e, axis=-1, keepdims=True), 0.0)
    o_c = _dot(p.astype(BF16), vc)
    p_grp = p[0:TQ]
    for g in range(1, NSA_GROUP):
        p_grp = p_grp + p[g * TQ:(g + 1) * TQ]
    imp = lax.dot_general(ovt_ref[...], p_grp, (((1,), (1,)), ((), ())),
                          preferred_element_type=F32, precision=HIGHEST)

    jj = lax.broadcasted_iota(jnp.int32, (n_sel, TQ), 0)
    cur = (qb * TQ + lax.broadcasted_iota(jnp.int32, (1, TQ), 1)) // SEL_BLOCK
    forced = (jj == 0) | (jj == cur) | (jj == cur - 1)
    imp = jnp.where(forced, FORCE_SCORE, jnp.where(jj > cur, -FORCE_SCORE, imp))
    beaten = jnp.zeros((n_sel, TQ), F32)
    for jp in range(n_sel):
        other = imp[jp:jp + 1, :]
        beats = (other > imp) | ((other == imp) & (jj > jp))
        beaten = beaten + jnp.where(beats, 1.0, 0.0)
    not_picked = jnp.where(beaten < min(N_SELECT, n_sel), 0.0, 1.0)
    if n_sel < LANES:
        not_picked = jnp.concatenate([not_picked, jnp.zeros((LANES - n_sel, TQ), F32)], axis=0)
    q_bias = jnp.transpose(not_picked) * MASK_BIAS

    qf2 = qf * LOG2E
    q_main = jnp.concatenate(
        [jnp.concatenate([qf2[:, g * HEAD_DIM:(g + 1) * HEAD_DIM],
                          jnp.broadcast_to(sfeat_ref[pl.ds(n * NSA_GROUP + g, 1), HEAD_DIM:LANES],
                                           (TQ, LANES - HEAD_DIM))], axis=1)
         for g in range(NSA_GROUP)], axis=0)
    groups = [slice(g * TQ, (g + 1) * TQ) for g in range(NSA_GROUP)]
    q_win = [q_main[r].astype(BF16) for r in groups]
    q_sel = [jnp.concatenate([q_main[r], q_bias], axis=1).astype(BF16) for r in groups]
    tpos = qb * TQ + lax.broadcasted_iota(jnp.int32, (TQ, 1), 0)

    def flash_step(carry, scores, v):
        out = []
        for (m, acc), sc in zip(carry, scores):
            m_new = jnp.maximum(m, jnp.max(sc, axis=-1, keepdims=True))
            pr = jnp.exp2(sc - m_new).astype(BF16)
            out.append((m_new, jnp.exp2(m - m_new) * acc + _dot(pr, v)))
        return tuple(out)

    def finish(carry):
        return [(acc[:, :HEAD_DIM], acc[:, HEAD_DIM:HEAD_DIM + 1]) for _, acc in carry]

    init = tuple((jnp.full((TQ, 1), M_FLOOR, F32), jnp.zeros((TQ, LANES), F32)) for _ in groups)

    def sel_scores(r, keep=None):
        k = ks_ref[r, :]
        sc = [_dot_nt(qg, k) for qg in q_sel]
        return sc if keep is None else [jnp.where(keep, s_, NEG_INF) for s_ in sc]

    def key_rows(kt):
        return pl.ds(pl.multiple_of(kt * KEY_TILE, KEY_TILE), KEY_TILE)

    def sel_body(kt, carry):
        return flash_step(carry, sel_scores(key_rows(kt)), vs_ref[key_rows(kt), :])

    n_full = (qb * TQ) // KEY_TILE
    carry = lax.fori_loop(0, n_full, sel_body, init)
    sc = sel_scores(key_rows(n_full))
    kpos = n_full * KEY_TILE + lax.broadcasted_iota(jnp.int32, (1, KEY_TILE), 1)
    sc = [jnp.where(kpos <= tpos, s_, NEG_INF) for s_ in sc]
    o_s = finish(flash_step(carry, sc, vs_ref[key_rows(n_full), :]))

    n_win = WINDOW // TQ + 1
    lane_q = lax.broadcasted_iota(jnp.int32, (1, TQ), 1)
    win_sc = [[] for _ in groups]
    win_v = []
    for rel in range(n_win):
        kt = qb - (n_win - 1) + rel
        rows_k = pl.ds(pl.multiple_of(jnp.maximum(kt, 0) * TQ, TQ), TQ)
        k = kw_ref[rows_k, :]
        win_v.append(vw_ref[rows_k, :])
        dist = tpos - (kt * TQ + lane_q)
        keep = jnp.full((1, TQ), kt, jnp.int32) >= 0
        if rel == 0:
            keep = keep & (dist < WINDOW)
        if rel == n_win - 1:
            keep = keep & (dist >= 0)
        for g, qg in enumerate(q_win):
            win_sc[g].append(jnp.where(keep, _dot_nt(qg, k), NEG_INF))
    o_w = finish(flash_step(init, [jnp.concatenate(s_, axis=1) for s_ in win_sc], jnp.concatenate(win_v, axis=0)))

    gates = _pick_head_gates(misc_ref[:, 0:GATE_W], n)
    outs = []
    for g, r in enumerate(groups):
        (u_s, l_s), (u_w, l_w) = o_s[g], o_w[g]
        outs.append(gates[:, 3 * g:3 * g + 1] * o_c[r] + (gates[:, 3 * g + 1:3 * g + 2] / l_s) * u_s
                    + (gates[:, 3 * g + 2:3 * g + 3] / l_w) * u_w)
    o_ref[...] = jnp.concatenate(outs, axis=1)


def _nsa_prompt(proj, kcv, slopes, b, t):
    n16 = t // CMP_STRIDE
    n_sel = t // SEL_BLOCK
    nqb = t // TQ
    assert n_sel <= LANES and t % KEY_TILE == 0
    k_sel, v_sel, k_win, v_win = _kv_prep(proj, b, t)
    seq = lambda w: pl.BlockSpec((None, None, t, w), lambda i, n, qb, sl: (i, n, 0, 0))
    grid_spec = pltpu.PrefetchScalarGridSpec(
        num_scalar_prefetch=1,
        grid=(b, NSA_KV_HEADS, nqb),
        in_specs=[pl.BlockSpec((TQ, Q_SLAB), lambda i, n, qb, sl: (i * nqb + qb, COL_Q // Q_SLAB + n)),
                  pl.BlockSpec((None, None, n16, KV_SLAB), lambda i, n, qb, sl: (i, n, 0, 0)),
                  seq(2 * LANES), seq(LANES), seq(LANES), seq(LANES),
                  pl.BlockSpec((TQ, LANES), lambda i, n, qb, sl: (i * nqb + qb, COL_MISC // LANES)),
                  pl.BlockSpec((NSA_HEADS, LANES), lambda i, n, qb, sl: (0, 0)),
                  pl.BlockSpec((n_sel, n16), lambda i, n, qb, sl: (0, 0))],
        out_specs=pl.BlockSpec((TQ, Q_SLAB), lambda i, n, qb, sl: (i * nqb + qb, n)),
    )
    overlap_t = jnp.asarray(_overlap_counts(np.arange(n16)[None, :], np.arange(n_sel)[:, None]))
    return pl.pallas_call(
        functools.partial(_nsa_prompt_kernel, n16=n16, n_sel=n_sel),
        grid_spec=grid_spec,
        out_shape=jax.ShapeDtypeStruct((b * t, NSA_WIDTH), F32),
        compiler_params=_cparams(("parallel", "parallel", "arbitrary")),
        name="nsa_prompt",
    )(slopes, proj, kcv, k_sel, v_sel, k_win, v_win, proj, _slope_features(), overlap_t)


def _gla_kernel(q_ref, k_ref, v_ref, r_ref, misc_ref, wa_ref, ba_ref, gout_ref, s0_ref, o_ref, sfin_ref, state_ref,
                *, chunk, sub, valid_rows):
    c = pl.program_id(1)

    @pl.when(c == 0)
    def _():
        state_ref[...] = s0_ref[...]

    ga = misc_ref[:, MISC_GA:MISC_GA + GLA_RANK]
    x = jnp.dot(ga, wa_ref[...], preferred_element_type=F32, precision=HIGHEST) + ba_ref[...]
    lg = (jnp.minimum(x, 0.0) - jnp.log1p(jnp.exp(-jnp.abs(x)))) / GLA_TAU
    row = lax.broadcasted_iota(jnp.int32, (chunk, 1), 0)
    if valid_rows < chunk:
        lg = jnp.where(row < valid_rows, lg, 0.0)
    tri = (lax.broadcasted_iota(jnp.int32, (chunk, chunk), 0)
           >= lax.broadcasted_iota(jnp.int32, (chunk, chunk), 1)).astype(F32)
    cum_all = jnp.dot(tri, lg, preferred_element_type=F32, precision=HIGHEST)
    for h in range(GLA_HEADS):
        kcols = slice(h * GLA_DK, (h + 1) * GLA_DK)
        vcols = slice(h * GLA_DV, (h + 1) * GLA_DV)
        o, new_state = _gla_head_chunk(q_ref[:, kcols] * (GLA_DK ** -0.5), k_ref[:, kcols], v_ref[:, vcols],
                                       cum_all[:, kcols], state_ref[h], chunk, sub)
        state_ref[h] = new_state
        rg = r_ref[:, vcols]
        o_ref[:, vcols] = _rms(o, gout_ref[...]) * (rg * _sigmoid(rg))

    @pl.when(c == pl.num_programs(1) - 1)
    def _():
        sfin_ref[...] = state_ref[...]


def _gla_head_chunk(q, k, v, cum, state, chunk, sub):
    vb = v.astype(BF16)
    inter = _dot((q * jnp.exp(cum)).astype(BF16), state.astype(BF16))

    outs = []
    for i in range(chunk // sub):
        r0 = i * sub
        qi, ki, ci, vi = q[r0:r0 + sub], k[r0:r0 + sub], cum[r0:r0 + sub], v[r0:r0 + sub]
        o_i = inter[r0:r0 + sub]
        if i > 0:
            anchor = cum[r0:r0 + 1]
            qd = (qi * jnp.exp(ci - anchor)).astype(BF16)
            kd = (k[0:r0] * jnp.exp(anchor - cum[0:r0])).astype(BF16)
            o_i = o_i + _dot(_dot_nt(qd, kd).astype(BF16), vb[0:r0])
        trow = lax.broadcasted_iota(jnp.int32, (sub, 1), 0)
        for s_ in range(sub):
            w = jnp.sum(qi * (ki[s_:s_ + 1] * jnp.exp(jnp.minimum(ci - ci[s_:s_ + 1], 0.0))), axis=-1, keepdims=True)
            o_i = o_i + jnp.where(trow >= s_, w, 0.0) * vi[s_:s_ + 1]
        outs.append(o_i)
    o = jnp.concatenate(outs, axis=0) if len(outs) > 1 else outs[0]

    last = cum[chunk - 1:chunk]
    kdec = (k * jnp.exp(last - cum)).astype(BF16)
    decay_col = jnp.transpose(jnp.broadcast_to(jnp.exp(last), (SUBLANES, GLA_DK)))[:, 0:1]
    return o, decay_col * state + _dot_tn(kdec, vb)


def _gla(proj, w_a2, b_a, g_out, s0, b, t, chunk, sub, valid_rows):
    nck = t // chunk
    kw = GLA_HEADS * GLA_DK
    state_spec = pl.BlockSpec((None, GLA_HEADS, GLA_DK, GLA_DV), lambda i, c: (i, 0, 0, 0))
    return pl.pallas_call(
        functools.partial(_gla_kernel, chunk=chunk, sub=sub, valid_rows=valid_rows),
        grid=(b, nck),
        in_specs=[pl.BlockSpec((chunk, kw), lambda i, c: (i * nck + c, COL_GQ // kw)),
                  pl.BlockSpec((chunk, kw), lambda i, c: (i * nck + c, COL_GK // kw)),
                  pl.BlockSpec((chunk, GLA_WIDTH), lambda i, c: (i * nck + c, COL_GV // GLA_WIDTH)),
                  pl.BlockSpec((chunk, GLA_WIDTH), lambda i, c: (i * nck + c, COL_GR // GLA_WIDTH)),
                  pl.BlockSpec((chunk, LANES), lambda i, c: (i * nck + c, COL_MISC // LANES)),
                  pl.BlockSpec((GLA_RANK, kw), lambda i, c: (0, 0)),
                  pl.BlockSpec((1, kw), lambda i, c: (0, 0)),
                  pl.BlockSpec((1, GLA_DV), lambda i, c: (0, 0)),
                  state_spec],
        out_specs=[pl.BlockSpec((chunk, GLA_WIDTH), lambda i, c: (i * nck + c, 0)), state_spec],
        out_shape=[jax.ShapeDtypeStruct((b * t, GLA_WIDTH), F32),
                   jax.ShapeDtypeStruct((b, GLA_HEADS, GLA_DK, GLA_DV), F32)],
        scratch_shapes=[pltpu.VMEM((GLA_HEADS, GLA_DK, GLA_DV), F32)],
        compiler_params=_cparams(("parallel", "arbitrary")),
        name="gla",
    )(proj, proj, proj, proj, proj, w_a2, b_a, g_out, s0)


def _out_proj_kernel(on_ref, og_ref, x_ref, g_ref, w_ref, o_ref, a_ref):
    @pl.when(pl.program_id(1) == 0)
    def _():
        a_ref[:, :NSA_WIDTH] = _rms(on_ref[...], g_ref[...]).astype(BF16)
        a_ref[:, NSA_WIDTH:] = og_ref[...].astype(BF16)

    o_ref[...] = x_ref[...] + _dot(a_ref[...], w_ref[...])


def _out_proj(o_nsa, o_gla, x, g_nsa, w_out, tm, tn):
    m = x.shape[0]
    return pl.pallas_call(
        _out_proj_kernel,
        grid=(m // tm, D_MODEL // tn),
        in_specs=[pl.BlockSpec((tm, NSA_WIDTH), lambda i, j: (i, 0)),
                  pl.BlockSpec((tm, GLA_WIDTH), lambda i, j: (i, 0)),
                  pl.BlockSpec((tm, tn), lambda i, j: (i, j)),
                  pl.BlockSpec((1, NSA_WIDTH), lambda i, j: (0, 0)),
                  pl.BlockSpec((NSA_WIDTH + GLA_WIDTH, tn), lambda i, j: (0, j))],
        out_specs=pl.BlockSpec((tm, tn), lambda i, j: (i, j)),
        out_shape=jax.ShapeDtypeStruct((m, D_MODEL), F32),
        scratch_shapes=[pltpu.VMEM((tm, NSA_WIDTH + GLA_WIDTH), BF16)],
        compiler_params=_cparams(("parallel", "arbitrary")),
        name="out_proj",
    )(o_nsa, o_gla, x, g_nsa, w_out)


def _ffn_kernel(h_ref, g_ref, wa_ref, wg_ref, cw_ref, cb_ref, wd_ref, p1_ref, p2_ref, o_ref, tail_ref,
                n2_ref, acc_ref, carry_ref, *, tm, tf, seq_rows):
    i = pl.program_id(0)
    j = pl.program_id(1)

    @pl.when(j == 0)
    def _():
        n2_ref[...] = _rms(h_ref[...], g_ref[...]).astype(BF16)
        acc_ref[...] = jnp.zeros_like(acc_ref)

    n2 = n2_ref[...]
    a = _dot(n2, wa_ref[...])
    gate = _dot(n2, wg_ref[...])
    row = lax.broadcasted_iota(jnp.int32, (tm, 1), 0)
    r1 = pltpu.roll(a, 1, axis=0)
    r2 = pltpu.roll(a, 2, axis=0)
    if seq_rows >= tm:
        cols = pl.ds(pl.multiple_of(j * tf, tf), tf)
        first = (i % (seq_rows // tm)) == 0
        prev = jnp.where(first, p2_ref[...], carry_ref[:, cols])
        a1 = jnp.where(row == 0, prev[1:2], r1)
        a2 = jnp.where(row == 0, prev[0:1], jnp.where(row == 1, prev[1:2], r2))
        carry_ref[:, cols] = a[tm - 2:tm]
    else:
        t = row % seq_rows
        a1 = jnp.where(t == 0, p1_ref[...], r1)
        a2 = jnp.where(t < 2, p2_ref[...], r2)
    cw = cw_ref[...]
    conv = cb_ref[...] + a2 * cw[0:1] + a1 * cw[1:2] + a * cw[2:3]
    y = (_gelu_tanh(conv) * gate).astype(BF16)
    acc_ref[...] += _dot(y, wd_ref[...])
    tail_ref[...] = a[tm - tail_ref.shape[0]:tm]

    @pl.when(j == pl.num_programs(1) - 1)
    def _():
        o_ref[...] = h_ref[...] + acc_ref[...]


def _ffn(h, g_ffn, w_up, conv_w, conv_b, w_down, p1, p2, tm, tf, seq_rows):
    m = h.shape[0]
    nj = D_FF // tf
    if seq_rows >= tm:
        tiles_per_seq = seq_rows // tm
        p1_spec = pl.BlockSpec((None, CONV_W - 1, tf), lambda i, j: (i // tiles_per_seq, 0, j))
        p2_spec = pl.BlockSpec((None, CONV_W - 1, tf), lambda i, j: (i // tiles_per_seq, 0, j))
    else:
        p1_spec = pl.BlockSpec((tm, tf), lambda i, j: (i, j))
        p2_spec = pl.BlockSpec((tm, tf), lambda i, j: (i, j))
    if seq_rows >= tm:
        tail_spec = pl.BlockSpec((None, SUBLANES, tf), lambda i, j: (i, 0, j))
        tail_shape = jax.ShapeDtypeStruct((m // tm, SUBLANES, D_FF), F32)
    else:
        tail_spec = pl.BlockSpec((tm, tf), lambda i, j: (i, j))
        tail_shape = jax.ShapeDtypeStruct((m, D_FF), F32)
    return pl.pallas_call(
        functools.partial(_ffn_kernel, tm=tm, tf=tf, seq_rows=seq_rows),
        grid=(m // tm, nj),
        in_specs=[pl.BlockSpec((tm, D_MODEL), lambda i, j: (i, 0)),
                  pl.BlockSpec((1, D_MODEL), lambda i, j: (0, 0)),
                  pl.BlockSpec((D_MODEL, tf), lambda i, j: (0, j)),
                  pl.BlockSpec((D_MODEL, tf), lambda i, j: (0, nj + j)),
                  pl.BlockSpec((CONV_W, tf), lambda i, j: (0, j)),
                  pl.BlockSpec((1, tf), lambda i, j: (0, j)),
                  pl.BlockSpec((tf, D_MODEL), lambda i, j: (j, 0)),
                  p1_spec, p2_spec],
        out_specs=[pl.BlockSpec((tm, D_MODEL), lambda i, j: (i, 0)),
                   tail_spec],
        out_shape=[jax.ShapeDtypeStruct((m, D_MODEL), F32), tail_shape],
        scratch_shapes=[pltpu.VMEM((tm, D_MODEL), BF16), pltpu.VMEM((tm, D_MODEL), F32),
                        pltpu.VMEM((CONV_W - 1, D_FF), F32)],
        compiler_params=_cparams(("arbitrary", "arbitrary")),
        name="conv_ffn",
    )(h, g_ffn, w_up, w_up, conv_w, conv_b, w_down, p1, p2)


def _ple_kernel(h_ref, p_ref, wg_ref, wp_ref, gp_ref, gf_ref, o_ref, *, final_norm):
    h = h_ref[...]
    gate = _sigmoid(_dot(h.astype(BF16), wg_ref[...]))
    pe = _rms(_dot(p_ref[...].astype(BF16), wp_ref[...]), gp_ref[...])
    h = h + gate * pe
    o_ref[...] = _rms(h, gf_ref[...]) if final_norm else h


def _ple(h, p, w_gate, w_proj, g_ple, g_final, tm, final_norm):
    m = h.shape[0]
    return pl.pallas_call(
        functools.partial(_ple_kernel, final_norm=final_norm),
        grid=(m // tm,),
        in_specs=[pl.BlockSpec((tm, D_MODEL), lambda i: (i, 0)),
                  pl.BlockSpec((tm, PLE_DIM), lambda i: (i, 0)),
                  pl.BlockSpec((D_MODEL, D_MODEL), lambda i: (0, 0)),
                  pl.BlockSpec((PLE_DIM, D_MODEL), lambda i: (0, 0)),
                  pl.BlockSpec((1, D_MODEL), lambda i: (0, 0)),
                  pl.BlockSpec((1, D_MODEL), lambda i: (0, 0))],
        out_specs=pl.BlockSpec((tm, D_MODEL), lambda i: (i, 0)),
        out_shape=jax.ShapeDtypeStruct((m, D_MODEL), F32),
        compiler_params=_cparams(("parallel",)),
        name="ple_norm",
    )(h, p, w_gate, w_proj, g_ple, g_final)


CMP_PAGES = 16
CHUNKS_PER_PAGE = PAGE_SIZE // CMP_STRIDE


def _cmp_sample_kernel(pt_ref, *refs):
    page_refs = refs[:CMP_PAGES]
    perm_ref, w1_ref, w1p_ref, w2_ref, pe_ref, o_ref, carry_ref, x_ref = refs[CMP_PAGES:]
    grp = pl.program_id(1)
    nck = CMP_PAGES * CHUNKS_PER_PAGE

    @pl.when(grp == 0)
    def _():
        carry_ref[...] = jnp.zeros_like(carry_ref)

    perm = perm_ref[...]
    for k, r in enumerate(page_refs):
        xp = _dot_nt(perm, r[...].astype(BF16))
        for s in range(CMP_STRIDE):
            for n in range(NSA_KV_HEADS):
                x_ref[s, n, k * CHUNKS_PER_PAGE:(k + 1) * CHUNKS_PER_PAGE, :] = (
                    xp[s * CHUNKS_PER_PAGE:(s + 1) * CHUNKS_PER_PAGE, n * KV_SLAB:(n + 1) * KV_SLAB])
    acc = jnp.zeros((NSA_KV_HEADS * nck, 4 * CMP_HIDDEN), F32)
    for sp in range(CMP_STRIDE // 2):
        xs = jnp.concatenate([x_ref[2 * sp].reshape(NSA_KV_HEADS * nck, KV_SLAB),
                              x_ref[2 * sp + 1].reshape(NSA_KV_HEADS * nck, KV_SLAB)], axis=1)
        acc = acc + _dot(xs.astype(BF16), w1p_ref[sp])
    a = acc[:, :2 * CMP_HIDDEN]
    b = acc[:, 2 * CMP_HIDDEN:]
    a_prev = pltpu.roll(a, 1, axis=0)
    row = lax.broadcasted_iota(jnp.int32, (NSA_KV_HEADS * nck, 1), 0)
    for n in range(NSA_KV_HEADS):
        a_prev = jnp.where(row == n * nck, carry_ref[n:n + 1, :], a_prev)
        carry_ref[n:n + 1, :] = a[(n + 1) * nck - 1:(n + 1) * nck]
    hid = a_prev + b + _cmp_pe_hidden(pe_ref, w1_ref)
    res = _dot(_gelu_tanh(hid).astype(BF16), w2_ref[...])
    for n in range(NSA_KV_HEADS):
        o_ref[n] = res[n * nck:(n + 1) * nck]


def _cmp_sample(cache_c, page_table_flat, w1, w2, pe, b, n_pages):
    n_grp = n_pages // CMP_PAGES
    nck = CMP_PAGES * CHUNKS_PER_PAGE

    def page_spec(k):
        return pl.BlockSpec((None, KV_WIDTH, PAGE_SIZE),
                            lambda i, g, pt: (pt[i * n_pages + g * CMP_PAGES + k], 0, 0))

    r = np.arange(PAGE_SIZE)
    perm_np = np.zeros((PAGE_SIZE, PAGE_SIZE), np.float32)
    perm_np[(r % CMP_STRIDE) * CHUNKS_PER_PAGE + r // CMP_STRIDE, r] = 1.0
    perm = jnp.asarray(perm_np, dtype=BF16)
    w1p = w1.reshape(CMP_STRIDE // 2, 2 * KV_SLAB, 4 * CMP_HIDDEN)

    grid_spec = pltpu.PrefetchScalarGridSpec(
        num_scalar_prefetch=1,
        grid=(b, n_grp),
        in_specs=[page_spec(k) for k in range(CMP_PAGES)]
        + [pl.BlockSpec(perm.shape, lambda i, g, pt: (0, 0)),
           pl.BlockSpec(w1.shape, lambda i, g, pt: (0, 0, 0)),
           pl.BlockSpec(w1p.shape, lambda i, g, pt: (0, 0, 0)),
           pl.BlockSpec(w2.shape, lambda i, g, pt: (0, 0)),
           pl.BlockSpec(pe.shape, lambda i, g, pt: (0, 0))],
        out_specs=pl.BlockSpec((None, NSA_KV_HEADS, nck, KV_SLAB), lambda i, g, pt: (i, 0, g, 0)),
        scratch_shapes=[pltpu.VMEM((SUBLANES, 2 * CMP_HIDDEN), F32),
                        pltpu.VMEM((CMP_STRIDE, NSA_KV_HEADS, nck, KV_SLAB), F32)],
    )
    return pl.pallas_call(
        _cmp_sample_kernel,
        grid_spec=grid_spec,
        out_shape=jax.ShapeDtypeStruct((b, NSA_KV_HEADS, n_pages * CHUNKS_PER_PAGE, KV_SLAB), F32),
        compiler_params=_cparams(("parallel", "arbitrary")),
        name="cmp_sample",
    )(page_table_flat, *([cache_c] * CMP_PAGES), perm, w1, w1p, w2, pe)


def _sel_sample_kernel(slopes_ref, q_ref, kcv_ref, ov_ref, oc_ref, idx_ref, *, past_len, n_sel_pad, t_valid):
    tp = SUBLANES
    p_grps = []
    for n in range(NSA_KV_HEADS):
        oc, p_grp = _sel_sample_attend(n, slopes_ref, q_ref[:, n * Q_SLAB:(n + 1) * Q_SLAB] * SCALE, kcv_ref[n],
                                       past_len)
        oc_ref[n] = oc
        p_grps.append(p_grp)
    imp_all = jnp.dot(jnp.concatenate(p_grps, axis=0), ov_ref[...], preferred_element_type=F32, precision=HIGHEST)
    for n in range(NSA_KV_HEADS):
        idx_ref[n] = _sel_sample_pick(imp_all[n * tp:(n + 1) * tp], past_len, n_sel_pad, t_valid)


def _sel_sample_attend(n, slopes_ref, qf, kcv, past_len):
    tp = SUBLANES
    rows = NSA_GROUP * tp
    n_rows_c = past_len // CMP_STRIDE
    q = jnp.concatenate([qf[:, g * HEAD_DIM:(g + 1) * HEAD_DIM] for g in range(NSA_GROUP)], axis=0).astype(BF16)
    tok = lax.broadcasted_iota(jnp.int32, (rows, 1), 0) % tp
    qpos = past_len + tok
    slope = _slope_col(slopes_ref, n, tp, rows)
    kc = kcv[:, :HEAD_DIM].astype(BF16)
    vc = kcv[:, HEAD_DIM:].astype(BF16)
    s = _dot_nt(q, kc)
    cp = lax.broadcasted_iota(jnp.int32, (1, n_rows_c), 1)
    end = (cp - 1) * CMP_STRIDE + (CMP_BLOCK - 1)
    dist = (qpos - end).astype(F32)
    valid = (cp >= 1) & (dist >= 0)
    s = jnp.where(valid, s - slope * dist, NEG_INF)
    e = jnp.exp(s - jnp.max(s, axis=-1, keepdims=True))
    p = jnp.where(valid, e / jnp.sum(e, axis=-1, keepdims=True), 0.0)
    o_c = _dot(p.astype(BF16), vc)
    oc = jnp.concatenate([o_c[g * tp:(g + 1) * tp] for g in range(NSA_GROUP)], axis=1)
    p_grp = p[0:tp]
    for g in range(1, NSA_GROUP):
        p_grp = p_grp + p[g * tp:(g + 1) * tp]
    return oc, p_grp


def _sel_sample_pick(imp, past_len, n_sel_pad, t_valid):
    tp = SUBLANES
    nb_past = past_len // SEL_BLOCK
    n_tail = -(-t_valid // SEL_BLOCK)
    n_sel = nb_past + n_tail
    j = lax.broadcasted_iota(jnp.int32, (tp, n_sel_pad), 1)
    cur = (past_len + lax.broadcasted_iota(jnp.int32, (tp, 1), 0)) // SEL_BLOCK
    forced = (j == 0) | (j == cur) | (j == cur - 1)
    imp = jnp.where(forced, FORCE_SCORE, jnp.where(j > cur, -FORCE_SCORE, imp))
    imp = jnp.where(j < n_sel, imp, REMOVED_SCORE)
    _, picks = _top_blocks(imp, min(N_SELECT, n_sel))
    kcol = lax.broadcasted_iota(jnp.int32, (tp, N_SELECT), 1)
    idx = jnp.zeros((tp, N_SELECT), jnp.int32)
    for kk, pk in enumerate(picks):
        idx = jnp.where(kcol == kk, pk, idx)
    return idx


def _sel_sample(proj8, kcv, slopes, b, past_len, t_valid):
    n_rows_c = past_len // CMP_STRIDE
    nb_past = past_len // SEL_BLOCK
    n_sel_pad = -(-(nb_past + 1) // LANES) * LANES
    overlap = jnp.asarray(_overlap_counts(np.arange(n_rows_c)[:, None] - 1, np.arange(n_sel_pad)[None, :]))
    grid_spec = pltpu.PrefetchScalarGridSpec(
        num_scalar_prefetch=1,
        grid=(b,),
        in_specs=[pl.BlockSpec((SUBLANES, NSA_WIDTH), lambda i, sl: (i, COL_Q // NSA_WIDTH)),
                  pl.BlockSpec((None, NSA_KV_HEADS, n_rows_c, KV_SLAB), lambda i, sl: (i, 0, 0, 0)),
                  pl.BlockSpec(overlap.shape, lambda i, sl: (0, 0))],
        out_specs=[pl.BlockSpec((None, NSA_KV_HEADS, SUBLANES, Q_SLAB), lambda i, sl: (i, 0, 0, 0)),
                   pl.BlockSpec((None, NSA_KV_HEADS, SUBLANES, N_SELECT), lambda i, sl: (i, 0, 0, 0))],
    )
    return pl.pallas_call(
        functools.partial(_sel_sample_kernel, past_len=past_len, n_sel_pad=n_sel_pad, t_valid=t_valid),
        grid_spec=grid_spec,
        out_shape=[jax.ShapeDtypeStruct((b, NSA_KV_HEADS, SUBLANES, Q_SLAB), F32),
                   jax.ShapeDtypeStruct((b, NSA_KV_HEADS, SUBLANES, N_SELECT), jnp.int32)],
        compiler_params=_cparams(("parallel",)),
        name="sel_sample",
    )(slopes, proj8, kcv, overlap)


def _nsa_sample_kernel(idx_ref, pt_ref, slopes_ref, cache_ref, q_ref, tail_ref, wnew_ref, wcache_ref, oc_ref,
                       misc_ref, o_ref, pages_a, pages_b, sem_ref, *, past_len, t_valid, n_pages):
    n_blk = t_valid * N_SELECT
    nb_past = past_len // SEL_BLOCK
    per_page = PAGE_SIZE // SEL_BLOCK
    pairs = NSA_KV_HEADS // 2
    step = pl.program_id(0) * pairs + pl.program_id(1)
    n_steps = pl.num_programs(0) * pairs
    head_a = step * 2
    head_b = head_a + 1
    next_a = ((step + 1) % n_steps) * 2

    def page_copy(bh, j, buf, sem_i):
        t, kk = j // N_SELECT, j % N_SELECT
        jb = jnp.minimum(idx_ref[(bh * SUBLANES + t) * N_SELECT + kk], nb_past - 1)
        page = pt_ref[(bh // NSA_KV_HEADS) * n_pages + jb // per_page]
        return pltpu.make_async_copy(cache_ref.at[page, bh % NSA_KV_HEADS], buf.at[j], sem_ref.at[sem_i])

    def wait_all(bh, buf, sem_i):
        def body(j, c):
            page_copy(bh, j, buf, sem_i).wait()
            return c
        lax.fori_loop(0, n_blk, body, 0)

    def compute(bh, h2, buf):
        n = bh % NSA_KV_HEADS
        slope = _slope_col(slopes_ref, n, 1, SUBLANES)
        gates = _pick_head_gates(misc_ref[:, 0:GATE_W], n)
        tl = tail_ref[:, h2 * KV_SLAB:(h2 + 1) * KV_SLAB]
        wn = wnew_ref[:, h2 * KV_SLAB:(h2 + 1) * KV_SLAB]
        win_k = wcache_ref[h2, 0:HEAD_DIM, :].astype(BF16)
        win_v = wcache_ref[h2, HEAD_DIM:KV_SLAB, :].astype(BF16)
        rows_out = []
        for t in range(t_valid):
            picked = [buf.at[t * N_SELECT + kk] for kk in range(N_SELECT)]
            jbs = [idx_ref[(bh * SUBLANES + t) * N_SELECT + kk] for kk in range(N_SELECT)]
            rows_out.append(_nsa_sample_token(t, q_ref[t:t + 1, h2 * Q_SLAB:(h2 + 1) * Q_SLAB] * SCALE, slope, picked,
                                              jbs, tl, wn, win_k, win_v, oc_ref[h2, t:t + 1, :], gates[t:t + 1, :],
                                              past_len))
        rows_out.append(jnp.zeros((SUBLANES - t_valid, Q_SLAB), F32))
        o_ref[:, h2 * Q_SLAB:(h2 + 1) * Q_SLAB] = jnp.concatenate(rows_out, axis=0)

    @pl.when(step == 0)
    def _():
        for j in range(n_blk):
            page_copy(head_a, j, pages_a, 0).start()

    for j in range(n_blk):
        page_copy(head_b, j, pages_b, 1).start()
    wait_all(head_a, pages_a, 0)
    compute(head_a, 0, pages_a)
    for j in range(n_blk):
        page_copy(next_a, j, pages_a, 0).start()
    wait_all(head_b, pages_b, 1)
    compute(head_b, 1, pages_b)

    @pl.when(step == n_steps - 1)
    def _():
        wait_all(next_a, pages_a, 0)


def _nsa_sample_token(t, qrow, slope, picked, jbs, tl, wn, win_k, win_v, oc_row, gates, past_len):
    nb_past = past_len // SEL_BLOCK
    g8 = SUBLANES
    q = jnp.concatenate([qrow[:, g * HEAD_DIM:(g + 1) * HEAD_DIM] for g in range(NSA_GROUP)]
                        + [jnp.zeros((g8 - NSA_GROUP, HEAD_DIM), F32)], axis=0).astype(BF16)
    qpos = past_len + t

    per_page = PAGE_SIZE // SEL_BLOCK
    k_all = jnp.concatenate([r[0:HEAD_DIM, :] for r in picked], axis=1).astype(BF16)
    v_all = jnp.concatenate([r[HEAD_DIM:KV_SLAB, :] for r in picked], axis=1).astype(BF16)
    jb_row = jnp.concatenate([jnp.full((1, PAGE_SIZE), jb, jnp.int32) for jb in jbs], axis=1)
    tail_count = jnp.zeros((), jnp.int32)
    for jb in jbs:
        tail_count = tail_count + (jb >= nb_past).astype(jnp.int32)
    lane = lax.broadcasted_iota(jnp.int32, (1, N_SELECT * PAGE_SIZE), 1) % PAGE_SIZE
    kpos = (jb_row // per_page) * PAGE_SIZE + lane
    dist = (qpos - kpos).astype(F32)
    ok = (dist >= 0) & (kpos // SEL_BLOCK == jb_row) & (jb_row < nb_past)
    sc = _dot(q, k_all) - slope * dist
    parts = [(jnp.where(ok, sc, NEG_INF), ok, v_all, True)]
    lane8 = lax.broadcasted_iota(jnp.int32, (1, SUBLANES), 1)
    dist = (t - lane8).astype(F32)
    ok = (dist >= 0) & (jnp.full((1, SUBLANES), tail_count, jnp.int32) > 0)
    sc = _dot_nt(q, tl[:, :HEAD_DIM].astype(BF16)) - slope * dist
    parts.append((jnp.where(ok, sc, NEG_INF), ok, tl[:, HEAD_DIM:].astype(BF16), False))

    def softmax_av(parts):
        m = parts[0][0].max(axis=-1, keepdims=True)
        for sc, _, _, _ in parts[1:]:
            m = jnp.maximum(m, sc.max(axis=-1, keepdims=True))
        l = jnp.zeros((g8, 1), F32)
        acc = jnp.zeros((g8, HEAD_DIM), F32)
        for sc, ok, v, v_transposed in parts:
            pr = jnp.where(ok, jnp.exp(sc - m), 0.0)
            l = l + pr.sum(axis=-1, keepdims=True)
            acc = acc + (_dot_nt(pr.astype(BF16), v) if v_transposed else _dot(pr.astype(BF16), v))
        return acc / l

    o_s = softmax_av(parts)

    buf_len = win_k.shape[1]
    lane_w = lax.broadcasted_iota(jnp.int32, (1, buf_len), 1)
    dist_c = (qpos - (past_len - buf_len + lane_w)).astype(F32)
    ok_c = (dist_c >= 0) & (dist_c < WINDOW)
    sc_c = jnp.where(ok_c, _dot(q, win_k) - slope * dist_c, NEG_INF)
    dist_n = (t - lane8).astype(F32)
    ok_n = (dist_n >= 0) & (dist_n < WINDOW)
    sc_n = jnp.where(ok_n, _dot_nt(q, wn[:, :HEAD_DIM].astype(BF16)) - slope * dist_n, NEG_INF)
    o_w = softmax_av([(sc_c, ok_c, win_v, True), (sc_n, ok_n, wn[:, HEAD_DIM:].astype(BF16), False)])

    outs = []
    for g in range(NSA_GROUP):
        outs.append(gates[:, 3 * g:3 * g + 1] * oc_row[:, g * HEAD_DIM:(g + 1) * HEAD_DIM]
                    + gates[:, 3 * g + 1:3 * g + 2] * o_s[g:g + 1]
                    + gates[:, 3 * g + 2:3 * g + 3] * o_w[g:g + 1])
    return jnp.concatenate(outs, axis=1)


def _nsa_sample(proj8, cache_s, cache_w, o_c, idx_flat, page_table_flat, slopes, b, t_valid, past_len):
    n_pages = past_len // PAGE_SIZE
    buf_len = cache_w.shape[-1]
    pairs = NSA_KV_HEADS // 2
    n_blk = t_valid * N_SELECT
    pages = pltpu.VMEM((n_blk, KV_SLAB, PAGE_SIZE), F32)
    grid_spec = pltpu.PrefetchScalarGridSpec(
        num_scalar_prefetch=3,
        grid=(b, pairs),
        in_specs=[pl.BlockSpec(memory_space=pl.ANY),
                  pl.BlockSpec((SUBLANES, 2 * Q_SLAB), lambda i, p, *_: (i, COL_Q // (2 * Q_SLAB) + p)),
                  pl.BlockSpec((SUBLANES, 2 * KV_SLAB), lambda i, p, *_: (i, COL_KVS // (2 * KV_SLAB) + p)),
                  pl.BlockSpec((SUBLANES, 2 * KV_SLAB), lambda i, p, *_: (i, COL_KVW // (2 * KV_SLAB) + p)),
                  pl.BlockSpec((None, 2, KV_SLAB, buf_len), lambda i, p, *_: (i, p, 0, 0)),
                  pl.BlockSpec((None, 2, SUBLANES, Q_SLAB), lambda i, p, *_: (i, p, 0, 0)),
                  pl.BlockSpec((SUBLANES, LANES), lambda i, p, *_: (i, COL_MISC // LANES))],
        out_specs=pl.BlockSpec((SUBLANES, 2 * Q_SLAB), lambda i, p, *_: (i, p)),
        scratch_shapes=[pages, pages, pltpu.SemaphoreType.DMA((2,))],
    )
    return pl.pallas_call(
        functools.partial(_nsa_sample_kernel, past_len=past_len, t_valid=t_valid, n_pages=n_pages),
        grid_spec=grid_spec,
        out_shape=jax.ShapeDtypeStruct((b * SUBLANES, NSA_WIDTH), F32),
        compiler_params=_cparams(("arbitrary", "arbitrary")),
        name="nsa_sample",
    )(idx_flat, page_table_flat, slopes, cache_s, proj8, proj8, proj8, cache_w, o_c, proj8)


def _alibi_slopes():
    h = np.arange(1, NSA_HEADS + 1, dtype=np.float32)
    return jnp.asarray(2.0 ** (-8.0 * h / NSA_HEADS), dtype=F32)


def _permute_w_in(w_in):
    offs = np.cumsum([0, NSA_WIDTH, KV_WIDTH, KV_WIDTH, KV_WIDTH, GATE_W, GLA_HEADS * GLA_DK, GLA_HEADS * GLA_DK,
                      GLA_WIDTH, GLA_WIDTH, GLA_RANK])
    w_bf = w_in.astype(BF16)
    piece = [w_bf[:, offs[k]:offs[k + 1]] for k in range(10)]
    q, kvc, kvs, kvw, gl, gq, gk, gv, gr, ga = piece
    pad = jnp.zeros((w_in.shape[0], LANES - GATE_W - GLA_RANK), BF16)
    return jnp.concatenate([q, gv, gr, kvc, kvs, kvw, gq, gk, gl, ga, pad], axis=1)


def _row_tile(m, pref):
    return pref if m % pref == 0 else m


def _dense_tail(h_in, o_nsa, o_gla, p_emb, wts, conv_p1, conv_p2, seq_rows, last_layer, tm):
    (g_nsa, w_out, g_ffn, w_up, conv_w, conv_b, w_down, w_ple_proj, g_ple, w_ple_gate, g_final) = wts
    h = _out_proj(o_nsa, o_gla, h_in, g_nsa, w_out, tm, D_MODEL // 2)
    h, tails = _ffn(h, g_ffn, w_up, conv_w, conv_b, w_down, conv_p1, conv_p2, tm, FFN_COLS, seq_rows)
    y = _ple(h, p_emb, w_ple_gate, w_ple_proj, g_ple, g_final, tm, last_layer)
    return y, tails


def kernel(x_prompt, x_sample, p_prompt, p_sample, cache_cmp_kv, cache_sel_kv, cache_win_kv, state_gla, state_ffn_conv, page_table, g_attn, w_in, w_cmp_k1, w_cmp_k2, pe_cmp_k, w_cmp_v1, w_cmp_v2, pe_cmp_v, w_gla_a2, b_gla_a, g_nsa_out, g_gla_out, w_out, g_ffn, w_up, conv_w, conv_b, w_down, w_ple_proj, g_ple, w_ple_gate, g_final):
    depth = w_in.shape[0]
    bp, tp, _ = x_prompt.shape
    bs, ts, _ = x_sample.shape
    n_pages = page_table.shape[1]
    past_len = n_pages * PAGE_SIZE
    n_pool = cache_cmp_kv.shape[1]
    assert tp % TQ == 0 and tp >= WINDOW and ts <= SUBLANES and ts <= SEL_BLOCK and ts >= CONV_W - 1
    assert n_pages % CMP_PAGES == 0 and tp // SEL_BLOCK >= N_SELECT
    slopes = _alibi_slopes()
    pt_flat = page_table.reshape(-1).astype(jnp.int32)
    kv_shape = (NSA_KV_HEADS, 2, HEAD_DIM)

    hp = x_prompt.reshape(bp * tp, D_MODEL)
    hs = x_sample.reshape(bs * ts, D_MODEL)
    new_p = [[] for _ in range(5)]
    new_s = [[] for _ in range(5)]
    tm_p = _row_tile(bp * tp, DENSE_ROWS)
    tm_s = bs * ts
    for i in range(depth):
        last = i == depth - 1
        w_in_p = _permute_w_in(w_in[i])
        g_a = g_attn[i].reshape(1, D_MODEL)
        w1c, w2c, pec = _cmp_weights(w_cmp_k1[i], w_cmp_k2[i], pe_cmp_k[i], w_cmp_v1[i], w_cmp_v2[i], pe_cmp_v[i])
        b_a = b_gla_a[i].reshape(1, -1)
        g_go = g_gla_out[i].reshape(1, GLA_DV)
        wts = (g_nsa_out[i].reshape(1, -1), w_out[i].astype(BF16), g_ffn[i].reshape(1, -1), w_up[i].astype(BF16),
               conv_w[i], conv_b[i].reshape(1, -1), w_down[i].astype(BF16), w_ple_proj[i].astype(BF16),
               g_ple[i].reshape(1, -1), w_ple_gate[i].astype(BF16), g_final.reshape(1, -1))

        proj = _norm_matmul(hp, g_a, w_in_p, _row_tile(bp * tp, IN_PROJ_ROWS), IN_PROJ_COLS)
        kcv = _cmp_prompt(proj, bp, tp, w1c, w2c, pec)
        o_nsa = _nsa_prompt(proj, kcv, slopes, bp, tp)
        s0 = jnp.zeros((bp, GLA_HEADS, GLA_DK, GLA_DV), F32)
        o_gla, s_new = _gla(proj, w_gla_a2[i], b_a, g_go, s0, bp, tp, 64, 16, 64)
        zbuf = jnp.zeros((bp, CONV_W - 1, D_FF), F32)
        hp, tails = _dense_tail(hp, o_nsa, o_gla, p_prompt[i].reshape(bp * tp, PLE_DIM), wts, zbuf, zbuf, tp, last, tm_p)
        proj3 = proj.reshape(bp, tp, D_IN_PAD)
        new_p[0].append(proj3[:, :, COL_KVC:COL_KVC + KV_WIDTH].reshape((bp, tp) + kv_shape))
        new_p[1].append(proj3[:, :, COL_KVS:COL_KVS + KV_WIDTH].reshape((bp, tp) + kv_shape))
        new_p[2].append(proj3[:, tp - WINDOW:, COL_KVW:COL_KVW + KV_WIDTH].reshape((bp, WINDOW) + kv_shape))
        new_p[3].append(s_new)
        tiles_per_seq = tp // tm_p
        new_p[4].append(tails.reshape(bp, tiles_per_seq, SUBLANES, D_FF)[:, -1, SUBLANES - (CONV_W - 1):, :])

        proj_s = _norm_matmul(hs, g_a, w_in_p, tm_s, IN_PROJ_COLS)
        proj8 = jnp.pad(proj_s.reshape(bs, ts, D_IN_PAD), ((0, 0), (0, SUBLANES - ts), (0, 0))).reshape(bs * SUBLANES, D_IN_PAD)
        cache_c = jnp.transpose(cache_cmp_kv[i], (0, 2, 3, 4, 1)).reshape(n_pool, KV_WIDTH, PAGE_SIZE)
        cache_s = jnp.transpose(cache_sel_kv[i], (0, 2, 3, 4, 1)).reshape(n_pool, NSA_KV_HEADS, KV_SLAB, PAGE_SIZE)
        cache_w = jnp.transpose(cache_win_kv[i], (0, 2, 3, 4, 1)).reshape(bs, NSA_KV_HEADS, KV_SLAB, -1)
        kcv_s = _cmp_sample(cache_c, pt_flat, w1c, w2c, pec, bs, n_pages)
        o_c, idx = _sel_sample(proj8, kcv_s, slopes, bs, past_len, ts)
        o_nsa_s = _nsa_sample(proj8, cache_s, cache_w, o_c, idx.reshape(-1), pt_flat, slopes, bs, ts, past_len)
        o_nsa_s = o_nsa_s.reshape(bs, SUBLANES, NSA_WIDTH)[:, :ts].reshape(bs * ts, NSA_WIDTH)
        o_gla_s, s_new_s = _gla(proj8, w_gla_a2[i], b_a, g_go, state_gla[i].astype(F32), bs, SUBLANES, SUBLANES, SUBLANES, ts)
        o_gla_s = o_gla_s.reshape(bs, SUBLANES, GLA_WIDTH)[:, :ts].reshape(bs * ts, GLA_WIDTH)
        buf = state_ffn_conv[i]
        zrow = jnp.zeros((bs, ts - 1, D_FF), F32)
        p1 = jnp.concatenate([buf[:, 1:2], zrow], axis=1).reshape(bs * ts, D_FF)
        p2 = jnp.concatenate([buf, jnp.zeros((bs, ts - 2, D_FF), F32)], axis=1).reshape(bs * ts, D_FF)
        hs, tails_s = _dense_tail(hs, o_nsa_s, o_gla_s, p_sample[i].reshape(bs * ts, PLE_DIM), wts, p1, p2, ts, last, tm_s)
        ps3 = proj_s.reshape(bs, ts, D_IN_PAD)
        new_s[0].append(ps3[:, :, COL_KVC:COL_KVC + KV_WIDTH].reshape((bs, ts) + kv_shape))
        new_s[1].append(ps3[:, :, COL_KVS:COL_KVS + KV_WIDTH].reshape((bs, ts) + kv_shape))
        new_s[2].append(ps3[:, :, COL_KVW:COL_KVW + KV_WIDTH].reshape((bs, ts) + kv_shape))
        new_s[3].append(s_new_s)
        new_s[4].append(tails_s.reshape(bs, ts, D_FF)[:, ts - (CONV_W - 1):, :])

    y_prompt = hp.reshape(bp, tp, D_MODEL)
    y_sample = hs.reshape(bs, ts, D_MODEL)
    cmp_p, sel_p, win_p, gla_p, conv_p = [jnp.stack(l) for l in new_p]
    cmp_s, sel_s, win_s, gla_s, conv_s = [jnp.stack(l) for l in new_s]
    return (y_prompt, y_sample, cmp_p, sel_p, win_p, gla_p, conv_p, cmp_s, sel_s, win_s, gla_s, conv_s)
```

```python
import functools

import numpy as np
import jax
import jax.numpy as jnp
from jax import lax
from jax.experimental import pallas as pl
from jax.experimental.pallas import tpu as pltpu

F32 = jnp.float32
BF16 = jnp.bfloat16
HIGHEST = lax.Precision.HIGHEST

D_MODEL = 2048
PAGE_SIZE = 128
NSA_HEADS = 16
NSA_KV_HEADS = 4
NSA_GROUP = NSA_HEADS // NSA_KV_HEADS
HEAD_DIM = 64
CMP_BLOCK = 32
CMP_STRIDE = 16
CMP_HIDDEN = 2 * HEAD_DIM
SEL_BLOCK = 64
N_SELECT = 16
WINDOW = 512
TQ = 256
GLA_HEADS = 4
GLA_DK = 128
GLA_DV = 256
GLA_RANK = 16
GLA_TAU = 16.0
D_FF = 5632
CONV_W = 3
PLE_DIM = 256
EPS = 1e-6
NEG_INF = -1e30
FORCE_SCORE = 1e9
REMOVED_SCORE = -3e38
SCALE = HEAD_DIM ** -0.5
LOG2E = np.float32(1.4426950408889634)

NSA_WIDTH = NSA_HEADS * HEAD_DIM
GLA_WIDTH = GLA_HEADS * GLA_DV
KV_WIDTH = 2 * NSA_KV_HEADS * HEAD_DIM
KV_SLAB = 2 * HEAD_DIM
Q_SLAB = NSA_GROUP * HEAD_DIM
GATE_W = 3 * NSA_HEADS

LANES = 128
SUBLANES = 8
VMEM_LIMIT = 56 * 1024 * 1024

COL_Q = 0
COL_GV = COL_Q + NSA_WIDTH
COL_GR = COL_GV + GLA_WIDTH
COL_KVC = COL_GR + GLA_WIDTH
COL_KVS = COL_KVC + KV_WIDTH
COL_KVW = COL_KVS + KV_WIDTH
COL_GQ = COL_KVW + KV_WIDTH
COL_GK = COL_GQ + GLA_HEADS * GLA_DK
COL_MISC = COL_GK + GLA_HEADS * GLA_DK
D_IN_PAD = COL_MISC + LANES
MISC_GA = GATE_W

DENSE_ROWS = 512
IN_PROJ_ROWS = 512
IN_PROJ_COLS = D_IN_PAD // 5
FFN_COLS = 512


def _cparams(sem):
    return pltpu.CompilerParams(dimension_semantics=sem, vmem_limit_bytes=VMEM_LIMIT)


def _rms(x, g):
    return x * lax.rsqrt(jnp.mean(x * x, axis=-1, keepdims=True) + EPS) * g


def _gelu_tanh(x):
    return 0.5 * x * (1.0 + jnp.tanh(np.float32(np.sqrt(2.0 / np.pi)) * (x + 0.044715 * (x * x * x))))


def _sigmoid(x):
    return 1.0 / (1.0 + jnp.exp(-x))


def _dot(a, b):
    return jnp.dot(a, b, preferred_element_type=F32)


def _dot_nt(a, b):
    return lax.dot_general(a, b, (((1,), (1,)), ((), ())), preferred_element_type=F32)


def _dot_tn(a, b):
    return lax.dot_general(a, b, (((0,), (0,)), ((), ())), preferred_element_type=F32)


def _norm_matmul_kernel(x_ref, g_ref, w_ref, o_ref, xn_ref):
    @pl.when(pl.program_id(1) == 0)
    def _():
        xn_ref[...] = _rms(x_ref[...], g_ref[...]).astype(BF16)

    o_ref[...] = _dot(xn_ref[...], w_ref[...])


def _norm_matmul(x, g, w, tm, tn):
    m, d = x.shape
    n = w.shape[1]
    return pl.pallas_call(
        _norm_matmul_kernel,
        grid=(m // tm, n // tn),
        in_specs=[pl.BlockSpec((tm, d), lambda i, j: (i, 0)),
                  pl.BlockSpec((1, d), lambda i, j: (0, 0)),
                  pl.BlockSpec((d, tn), lambda i, j: (0, j))],
        out_specs=pl.BlockSpec((tm, tn), lambda i, j: (i, j)),
        out_shape=jax.ShapeDtypeStruct((m, n), F32),
        scratch_shapes=[pltpu.VMEM((tm, d), BF16)],
        compiler_params=_cparams(("parallel", "arbitrary")),
        name="in_proj",
    )(x, g, w)


def _cmp_pe_hidden(pe_ref, w1_ref):
    span = CMP_BLOCK // CMP_STRIDE
    acc = jnp.zeros((SUBLANES, 2 * CMP_HIDDEN), F32)
    for j in range(span):
        for s in range(CMP_STRIDE):
            row = jnp.broadcast_to(pe_ref[pl.ds(j * CMP_STRIDE + s, 1), :], (SUBLANES, KV_SLAB)).astype(BF16)
            acc = acc + _dot(row, w1_ref[s][:, j * 2 * CMP_HIDDEN:(j + 1) * 2 * CMP_HIDDEN])
    return acc[0:1, :]


def _cmp_prompt_kernel(x_ref, w1_ref, w2_ref, pe_ref, o_ref, *, n16):
    acc = jnp.zeros((n16, 4 * CMP_HIDDEN), F32)
    for s in range(CMP_STRIDE):
        xs = x_ref[pl.ds(s, n16, stride=CMP_STRIDE), :].astype(BF16)
        acc = acc + _dot(xs, w1_ref[s])
    a = acc[:, :2 * CMP_HIDDEN]
    b_next = pltpu.roll(acc[:, 2 * CMP_HIDDEN:], n16 - 1, axis=0)
    hid = a + b_next + _cmp_pe_hidden(pe_ref, w1_ref)
    o_ref[...] = _dot(_gelu_tanh(hid).astype(BF16), w2_ref[...])


def _cmp_weights(w_k1, w_k2, pe_k, w_v1, w_v2, pe_v):
    z = jnp.zeros((CMP_BLOCK, HEAD_DIM, CMP_HIDDEN), F32)
    wl = jnp.concatenate([jnp.concatenate([w_k1, z], axis=2), jnp.concatenate([z, w_v1], axis=2)], axis=1)
    w1 = jnp.concatenate([wl[:CMP_STRIDE], wl[CMP_STRIDE:]], axis=2).astype(BF16)
    z2 = jnp.zeros((CMP_HIDDEN, HEAD_DIM), F32)
    w2 = jnp.concatenate([jnp.concatenate([w_k2, z2], axis=1), jnp.concatenate([z2, w_v2], axis=1)], axis=0)
    pe = jnp.concatenate([pe_k, pe_v], axis=1)
    return w1, w2.astype(BF16), pe


def _cmp_prompt(proj, b, t, w1, w2, pe):
    n16 = t // CMP_STRIDE
    col0 = COL_KVC // KV_SLAB
    return pl.pallas_call(
        functools.partial(_cmp_prompt_kernel, n16=n16),
        grid=(b, NSA_KV_HEADS),
        in_specs=[pl.BlockSpec((t, KV_SLAB), lambda i, n: (i, col0 + n)),
                  pl.BlockSpec(w1.shape, lambda i, n: (0, 0, 0)),
                  pl.BlockSpec(w2.shape, lambda i, n: (0, 0)),
                  pl.BlockSpec(pe.shape, lambda i, n: (0, 0))],
        out_specs=pl.BlockSpec((None, None, n16, KV_SLAB), lambda i, n: (i, n, 0, 0)),
        out_shape=jax.ShapeDtypeStruct((b, NSA_KV_HEADS, n16, KV_SLAB), F32),
        compiler_params=_cparams(("parallel", "parallel")),
        name="cmp_prompt",
    )(proj, w1, w2, pe)


def _overlap_counts(c, j):
    per_sel = SEL_BLOCK // CMP_STRIDE
    ov = sum(((c + k >= per_sel * j) & (c + k < per_sel * (j + 1))) for k in range(CMP_BLOCK // CMP_STRIDE))
    return np.where(c >= 0, ov, 0).astype(np.float32)


def _top_blocks(imp, n_pick):
    rows, nsel = imp.shape
    j = lax.broadcasted_iota(jnp.int32, (rows, nsel), 1).astype(F32)
    mask = jnp.zeros((rows, nsel), F32)
    picks = []
    for _ in range(n_pick):
        m = jnp.max(imp, axis=-1, keepdims=True)
        jmin = jnp.min(jnp.where(imp == m, j, float(nsel)), axis=-1, keepdims=True)
        hit = j == jmin
        mask = jnp.where(hit, 1.0, mask)
        imp = jnp.where(hit, REMOVED_SCORE, imp)
        picks.append(jmin.astype(jnp.int32))
    return mask, picks


def _slope_col(slopes_ref, n, rows_per_head, rows):
    g = lax.broadcasted_iota(jnp.int32, (rows, 1), 0) // rows_per_head
    col = jnp.zeros((rows, 1), F32)
    for gg in range(NSA_GROUP):
        col = jnp.where(g == gg, slopes_ref[n * NSA_GROUP + gg], col)
    return col


def _pick_head_gates(gl, n):
    out = jnp.zeros((gl.shape[0], 3 * NSA_GROUP), F32)
    for nn in range(NSA_KV_HEADS):
        out = jnp.where(n == nn, gl[:, nn * 3 * NSA_GROUP:(nn + 1) * 3 * NSA_GROUP], out)
    return _sigmoid(out)


MASK_BIAS = -131072.0
M_FLOOR = -65536.0
POS_HI, POS_LO = HEAD_DIM, HEAD_DIM + 3
KEY_TILE = 2 * TQ


def _slope_features():
    h = np.arange(1, NSA_HEADS + 1, dtype=np.float32)
    slopes = (2.0 ** (-8.0 * h / NSA_HEADS)).astype(np.float32)
    tab = np.zeros((NSA_HEADS, LANES), np.float32)
    rest = slopes * LOG2E
    for c in range(3):
        piece = rest.astype(BF16).astype(np.float32)
        tab[:, POS_HI + c] = piece
        tab[:, POS_LO + c] = piece
        rest = rest - piece
    assert not rest.any()
    return jnp.asarray(tab)


def _kv_prep_kernel(kvs_ref, kvw_ref, ks_ref, vs_ref, kw_ref, vw_ref, *, t):
    lane = lax.broadcasted_iota(jnp.int32, (t, LANES), 1)
    pos = lax.broadcasted_iota(jnp.int32, (t, LANES), 0)
    hi = ((pos // SEL_BLOCK) * SEL_BLOCK).astype(F32)
    lo = (pos % SEL_BLOCK).astype(F32)
    feat = jnp.where((lane >= POS_HI) & (lane < POS_HI + 3), hi,
                     jnp.where((lane >= POS_LO) & (lane < POS_LO + 3), lo, 0.0))
    ones_col = jnp.where(lane == HEAD_DIM, 1.0, 0.0)
    for src, k_out, v_out in ((kvs_ref, ks_ref, vs_ref), (kvw_ref, kw_ref, vw_ref)):
        x = src[...]
        k_out[:, 0:LANES] = jnp.where(lane < HEAD_DIM, x, feat).astype(BF16)
        v_out[...] = jnp.where(lane < HEAD_DIM, pltpu.roll(x, HEAD_DIM, axis=1), ones_col).astype(BF16)
    ks_ref[:, LANES:2 * LANES] = jnp.where(lane == pos // SEL_BLOCK, 1.0, 0.0).astype(BF16)


def _kv_prep(proj, b, t):
    spec_in = lambda col: pl.BlockSpec((t, KV_SLAB), lambda i, n: (i, col // KV_SLAB + n))
    spec_out = lambda w: pl.BlockSpec((None, None, t, w), lambda i, n: (i, n, 0, 0))
    shape = lambda w: jax.ShapeDtypeStruct((b, NSA_KV_HEADS, t, w), BF16)
    return pl.pallas_call(
        functools.partial(_kv_prep_kernel, t=t),
        grid=(b, NSA_KV_HEADS),
        in_specs=[spec_in(COL_KVS), spec_in(COL_KVW)],
        out_specs=[spec_out(2 * LANES), spec_out(LANES), spec_out(LANES), spec_out(LANES)],
        out_shape=[shape(2 * LANES), shape(LANES), shape(LANES), shape(LANES)],
        compiler_params=_cparams(("parallel", "parallel")),
        name="kv_prep",
    )(proj, proj)


def _nsa_prompt_kernel(slopes_ref, q_ref, kcv_ref, ks_ref, vs_ref, kw_ref, vw_ref, misc_ref, sfeat_ref, ovt_ref, o_ref,
                       *, n16, n_sel):
    n = pl.program_id(1)
    qb = pl.program_id(2)
    rows = NSA_GROUP * TQ
    qf = q_ref[...] * SCALE
    q = jnp.concatenate([qf[:, g * HEAD_DIM:(g + 1) * HEAD_DIM] for g in range(NSA_GROUP)], axis=0).astype(BF16)
    tok = lax.broadcasted_iota(jnp.int32, (rows, 1), 0) % TQ
    qpos = qb * TQ + tok
    slope = _slope_col(slopes_ref, n, TQ, rows)

    kcv = kcv_ref[...]
    kc = kcv[:, :HEAD_DIM].astype(BF16)
    vc = kcv[:, HEAD_DIM:].astype(BF16)
    s = _dot_nt(q, kc)
    end = lax.broadcasted_iota(jnp.int32, (1, n16), 1) * CMP_STRIDE + (CMP_BLOCK - 1)
    dist = (qpos - end).astype(F32)
    valid = (dist >= 0) & (end < n16 * CMP_STRIDE)
    s = jnp.where(valid, s - slope * dist, NEG_INF)
    e = jnp.exp(s - jnp.max(s, axis=-1, keepdims=True))
    p = jnp.where(valid, e / jnp.sum(e, axis=-1, keepdims=True), 0.0)
    o_c = _dot(p.astype(BF16), vc)
    p_grp = p[0:TQ]
    for g in range(1, NSA_GROUP):
        p_grp = p_grp + p[g * TQ:(g + 1) * TQ]
    imp = lax.dot_general(ovt_ref[...], p_grp, (((1,), (1,)), ((), ())),
                          preferred_element_type=F32, precision=HIGHEST)

    jj = lax.broadcasted_iota(jnp.int32, (n_sel, TQ), 0)
    cur = (qb * TQ + lax.broadcasted_iota(jnp.int32, (1, TQ), 1)) // SEL_BLOCK
    forced = (jj == 0) | (jj == cur) | (jj == cur - 1)
    imp = jnp.where(forced, FORCE_SCORE, jnp.where(jj > cur, -FORCE_SCORE, imp))
    beaten = jnp.zeros((n_sel, TQ), F32)
    for jp in range(n_sel):
        other = imp[jp:jp + 1, :]
        beats = (other > imp) | ((other == imp) & (jj > jp))
        beaten = beaten + jnp.where(beats, 1.0, 0.0)
    not_picked = jnp.where(beaten < min(N_SELECT, n_sel), 0.0, 1.0)
    if n_sel < LANES:
        not_picked = jnp.concatenate([not_picked, jnp.zeros((LANES - n_sel, TQ), F32)], axis=0)
    q_bias = jnp.transpose(not_picked) * MASK_BIAS

    qf2 = qf * LOG2E
    q_main = jnp.concatenate(
        [jnp.concatenate([qf2[:, g * HEAD_DIM:(g + 1) * HEAD_DIM],
                          jnp.broadcast_to(sfeat_ref[pl.ds(n * NSA_GROUP + g, 1), HEAD_DIM:LANES],
                                           (TQ, LANES - HEAD_DIM))], axis=1)
         for g in range(NSA_GROUP)], axis=0)
    groups = [slice(g * TQ, (g + 1) * TQ) for g in range(NSA_GROUP)]
    q_win = [q_main[r].astype(BF16) for r in groups]
    q_sel = [jnp.concatenate([q_main[r], q_bias], axis=1).astype(BF16) for r in groups]
    tpos = qb * TQ + lax.broadcasted_iota(jnp.int32, (TQ, 1), 0)

    def flash_step(carry, scores, v):
        out = []
        for (m, acc), sc in zip(carry, scores):
            m_new = jnp.maximum(m, jnp.max(sc, axis=-1, keepdims=True))
            pr = jnp.exp2(sc - m_new).astype(BF16)
            out.append((m_new, jnp.exp2(m - m_new) * acc + _dot(pr, v)))
        return tuple(out)

    def finish(carry):
        return [(acc[:, :HEAD_DIM], acc[:, HEAD_DIM:HEAD_DIM + 1]) for _, acc in carry]

    init = tuple((jnp.full((TQ, 1), M_FLOOR, F32), jnp.zeros((TQ, LANES), F32)) for _ in groups)

    def sel_scores(r, keep=None):
        k = ks_ref[r, :]
        sc = [_dot_nt(qg, k) for qg in q_sel]
        return sc if keep is None else [jnp.where(keep, s_, NEG_INF) for s_ in sc]

    def key_rows(kt):
        return pl.ds(pl.multiple_of(kt * KEY_TILE, KEY_TILE), KEY_TILE)

    def sel_body(kt, carry):
        return flash_step(carry, sel_scores(key_rows(kt)), vs_ref[key_rows(kt), :])

    n_full = (qb * TQ) // KEY_TILE
    carry = lax.fori_loop(0, n_full, sel_body, init)
    sc = sel_scores(key_rows(n_full))
    kpos = n_full * KEY_TILE + lax.broadcasted_iota(jnp.int32, (1, KEY_TILE), 1)
    sc = [jnp.where(kpos <= tpos, s_, NEG_INF) for s_ in sc]
    o_s = finish(flash_step(carry, sc, vs_ref[key_rows(n_full), :]))

    n_win = WINDOW // TQ + 1
    lane_q = lax.broadcasted_iota(jnp.int32, (1, TQ), 1)
    win_sc = [[] for _ in groups]
    win_v = []
    for rel in range(n_win):
        kt = qb - (n_win - 1) + rel
        rows_k = pl.ds(pl.multiple_of(jnp.maximum(kt, 0) * TQ, TQ), TQ)
        k = kw_ref[rows_k, :]
        win_v.append(vw_ref[rows_k, :])
        dist = tpos - (kt * TQ + lane_q)
        keep = jnp.full((1, TQ), kt, jnp.int32) >= 0
        if rel == 0:
            keep = keep & (dist < WINDOW)
        if rel == n_win - 1:
            keep = keep & (dist >= 0)
        for g, qg in enumerate(q_win):
            win_sc[g].append(jnp.where(keep, _dot_nt(qg, k), NEG_INF))
    o_w = finish(flash_step(init, [jnp.concatenate(s_, axis=1) for s_ in win_sc], jnp.concatenate(win_v, axis=0)))

    gates = _pick_head_gates(misc_ref[:, 0:GATE_W], n)
    outs = []
    for g, r in enumerate(groups):
        (u_s, l_s), (u_w, l_w) = o_s[g], o_w[g]
        outs.append(gates[:, 3 * g:3 * g + 1] * o_c[r] + (gates[:, 3 * g + 1:3 * g + 2] / l_s) * u_s
                    + (gates[:, 3 * g + 2:3 * g + 3] / l_w) * u_w)
    o_ref[...] = jnp.concatenate(outs, axis=1)


def _nsa_prompt(proj, kcv, slopes, b, t):
    n16 = t // CMP_STRIDE
    n_sel = t // SEL_BLOCK
    nqb = t // TQ
    assert n_sel <= LANES and t % KEY_TILE == 0
    k_sel, v_sel, k_win, v_win = _kv_prep(proj, b, t)
    seq = lambda w: pl.BlockSpec((None, None, t, w), lambda i, n, qb, sl: (i, n, 0, 0))
    grid_spec = pltpu.PrefetchScalarGridSpec(
        num_scalar_prefetch=1,
        grid=(b, NSA_KV_HEADS, nqb),
        in_specs=[pl.BlockSpec((TQ, Q_SLAB), lambda i, n, qb, sl: (i * nqb + qb, COL_Q // Q_SLAB + n)),
                  pl.BlockSpec((None, None, n16, KV_SLAB), lambda i, n, qb, sl: (i, n, 0, 0)),
                  seq(2 * LANES), seq(LANES), seq(LANES), seq(LANES),
                  pl.BlockSpec((TQ, LANES), lambda i, n, qb, sl: (i * nqb + qb, COL_MISC // LANES)),
                  pl.BlockSpec((NSA_HEADS, LANES), lambda i, n, qb, sl: (0, 0)),
                  pl.BlockSpec((n_sel, n16), lambda i, n, qb, sl: (0, 0))],
        out_specs=pl.BlockSpec((TQ, Q_SLAB), lambda i, n, qb, sl: (i * nqb + qb, n)),
    )
    overlap_t = jnp.asarray(_overlap_counts(np.arange(n16)[None, :], np.arange(n_sel)[:, None]))
    return pl.pallas_call(
        functools.partial(_nsa_prompt_kernel, n16=n16, n_sel=n_sel),
        grid_spec=grid_spec,
        out_shape=jax.ShapeDtypeStruct((b * t, NSA_WIDTH), F32),
        compiler_params=_cparams(("parallel", "parallel", "arbitrary")),
        name="nsa_prompt",
    )(slopes, proj, kcv, k_sel, v_sel, k_win, v_win, proj, _slope_features(), overlap_t)


def _gla_kernel(q_ref, k_ref, v_ref, r_ref, misc_ref, wa_ref, ba_ref, gout_ref, s0_ref, o_ref, sfin_ref, state_ref,
                *, chunk, sub, valid_rows):
    c = pl.program_id(1)

    @pl.when(c == 0)
    def _():
        state_ref[...] = s0_ref[...]

    ga = misc_ref[:, MISC_GA:MISC_GA + GLA_RANK]
    x = jnp.dot(ga, wa_ref[...], preferred_element_type=F32, precision=HIGHEST) + ba_ref[...]
    lg = (jnp.minimum(x, 0.0) - jnp.log1p(jnp.exp(-jnp.abs(x)))) / GLA_TAU
    row = lax.broadcasted_iota(jnp.int32, (chunk, 1), 0)
    if valid_rows < chunk:
        lg = jnp.where(row < valid_rows, lg, 0.0)
    tri = (lax.broadcasted_iota(jnp.int32, (chunk, chunk), 0)
           >= lax.broadcasted_iota(jnp.int32, (chunk, chunk), 1)).astype(F32)
    cum_all = jnp.dot(tri, lg, preferred_element_type=F32, precision=HIGHEST)
    for h in range(GLA_HEADS):
        kcols = slice(h * GLA_DK, (h + 1) * GLA_DK)
        vcols = slice(h * GLA_DV, (h + 1) * GLA_DV)
        o, new_state = _gla_head_chunk(q_ref[:, kcols] * (GLA_DK ** -0.5), k_ref[:, kcols], v_ref[:, vcols],
                                       cum_all[:, kcols], state_ref[h], chunk, sub)
        state_ref[h] = new_state
        rg = r_ref[:, vcols]
        o_ref[:, vcols] = _rms(o, gout_ref[...]) * (rg * _sigmoid(rg))

    @pl.when(c == pl.num_programs(1) - 1)
    def _():
        sfin_ref[...] = state_ref[...]


def _gla_head_chunk(q, k, v, cum, state, chunk, sub):
    vb = v.astype(BF16)
    inter = _dot((q * jnp.exp(cum)).astype(BF16), state.astype(BF16))

    outs = []
    for i in range(chunk // sub):
        r0 = i * sub
        qi, ki, ci, vi = q[r0:r0 + sub], k[r0:r0 + sub], cum[r0:r0 + sub], v[r0:r0 + sub]
        o_i = inter[r0:r0 + sub]
        if i > 0:
            anchor = cum[r0:r0 + 1]
            qd = (qi * jnp.exp(ci - anchor)).astype(BF16)
            kd = (k[0:r0] * jnp.exp(anchor - cum[0:r0])).astype(BF16)
            o_i = o_i + _dot(_dot_nt(qd, kd).astype(BF16), vb[0:r0])
        trow = lax.broadcasted_iota(jnp.int32, (sub, 1), 0)
        for s_ in range(sub):
            w = jnp.sum(qi * (ki[s_:s_ + 1] * jnp.exp(jnp.minimum(ci - ci[s_:s_ + 1], 0.0))), axis=-1, keepdims=True)
            o_i = o_i + jnp.where(trow >= s_, w, 0.0) * vi[s_:s_ + 1]
        outs.append(o_i)
    o = jnp.concatenate(outs, axis=0) if len(outs) > 1 else outs[0]

    last = cum[chunk - 1:chunk]
    kdec = (k * jnp.exp(last - cum)).astype(BF16)
    decay_col = jnp.transpose(jnp.broadcast_to(jnp.exp(last), (SUBLANES, GLA_DK)))[:, 0:1]
    return o, decay_col * state + _dot_tn(kdec, vb)


def _gla(proj, w_a2, b_a, g_out, s0, b, t, chunk, sub, valid_rows):
    nck = t // chunk
    kw = GLA_HEADS * GLA_DK
    state_spec = pl.BlockSpec((None, GLA_HEADS, GLA_DK, GLA_DV), lambda i, c: (i, 0, 0, 0))
    return pl.pallas_call(
        functools.partial(_gla_kernel, chunk=chunk, sub=sub, valid_rows=valid_rows),
        grid=(b, nck),
        in_specs=[pl.BlockSpec((chunk, kw), lambda i, c: (i * nck + c, COL_GQ // kw)),
                  pl.BlockSpec((chunk, kw), lambda i, c: (i * nck + c, COL_GK // kw)),
                  pl.BlockSpec((chunk, GLA_WIDTH), lambda i, c: (i * nck + c, COL_GV // GLA_WIDTH)),
                  pl.BlockSpec((chunk, GLA_WIDTH), lambda i, c: (i * nck + c, COL_GR // GLA_WIDTH)),
                  pl.BlockSpec((chunk, LANES), lambda i, c: (i * nck + c, COL_MISC // LANES)),
                  pl.BlockSpec((GLA_RANK, kw), lambda i, c: (0, 0)),
                  pl.BlockSpec((1, kw), lambda i, c: (0, 0)),
                  pl.BlockSpec((1, GLA_DV), lambda i, c: (0, 0)),
                  state_spec],
        out_specs=[pl.BlockSpec((chunk, GLA_WIDTH), lambda i, c: (i * nck + c, 0)), state_spec],
        out_shape=[jax.ShapeDtypeStruct((b * t, GLA_WIDTH), F32),
                   jax.ShapeDtypeStruct((b, GLA_HEADS, GLA_DK, GLA_DV), F32)],
        scratch_shapes=[pltpu.VMEM((GLA_HEADS, GLA_DK, GLA_DV), F32)],
        compiler_params=_cparams(("parallel", "arbitrary")),
        name="gla",
    )(proj, proj, proj, proj, proj, w_a2, b_a, g_out, s0)


def _out_proj_kernel(on_ref, og_ref, x_ref, g_ref, w_ref, o_ref, a_ref):
    @pl.when(pl.program_id(1) == 0)
    def _():
        a_ref[:, :NSA_WIDTH] = _rms(on_ref[...], g_ref[...]).astype(BF16)
        a_ref[:, NSA_WIDTH:] = og_ref[...].astype(BF16)

    o_ref[...] = x_ref[...] + _dot(a_ref[...], w_ref[...])


def _out_proj(o_nsa, o_gla, x, g_nsa, w_out, tm, tn):
    m = x.shape[0]
    return pl.pallas_call(
        _out_proj_kernel,
        grid=(m // tm, D_MODEL // tn),
        in_specs=[pl.BlockSpec((tm, NSA_WIDTH), lambda i, j: (i, 0)),
                  pl.BlockSpec((tm, GLA_WIDTH), lambda i, j: (i, 0)),
                  pl.BlockSpec((tm, tn), lambda i, j: (i, j)),
                  pl.BlockSpec((1, NSA_WIDTH), lambda i, j: (0, 0)),
                  pl.BlockSpec((NSA_WIDTH + GLA_WIDTH, tn), lambda i, j: (0, j))],
        out_specs=pl.BlockSpec((tm, tn), lambda i, j: (i, j)),
        out_shape=jax.ShapeDtypeStruct((m, D_MODEL), F32),
        scratch_shapes=[pltpu.VMEM((tm, NSA_WIDTH + GLA_WIDTH), BF16)],
        compiler_params=_cparams(("parallel", "arbitrary")),
        name="out_proj",
    )(o_nsa, o_gla, x, g_nsa, w_out)


def _ffn_kernel(h_ref, g_ref, wa_ref, wg_ref, cw_ref, cb_ref, wd_ref, p1_ref, p2_ref, o_ref, tail_ref,
                n2_ref, acc_ref, carry_ref, *, tm, tf, seq_rows):
    i = pl.program_id(0)
    j = pl.program_id(1)

    @pl.when(j == 0)
    def _():
        n2_ref[...] = _rms(h_ref[...], g_ref[...]).astype(BF16)
        acc_ref[...] = jnp.zeros_like(acc_ref)

    n2 = n2_ref[...]
    a = _dot(n2, wa_ref[...])
    gate = _dot(n2, wg_ref[...])
    row = lax.broadcasted_iota(jnp.int32, (tm, 1), 0)
    r1 = pltpu.roll(a, 1, axis=0)
    r2 = pltpu.roll(a, 2, axis=0)
    if seq_rows >= tm:
        cols = pl.ds(pl.multiple_of(j * tf, tf), tf)
        first = (i % (seq_rows // tm)) == 0
        prev = jnp.where(first, p2_ref[...], carry_ref[:, cols])
        a1 = jnp.where(row == 0, prev[1:2], r1)
        a2 = jnp.where(row == 0, prev[0:1], jnp.where(row == 1, prev[1:2], r2))
        carry_ref[:, cols] = a[tm - 2:tm]
    else:
        t = row % seq_rows
        a1 = jnp.where(t == 0, p1_ref[...], r1)
        a2 = jnp.where(t < 2, p2_ref[...], r2)
    cw = cw_ref[...]
    conv = cb_ref[...] + a2 * cw[0:1] + a1 * cw[1:2] + a * cw[2:3]
    y = (_gelu_tanh(conv) * gate).astype(BF16)
    acc_ref[...] += _dot(y, wd_ref[...])
    tail_ref[...] = a[tm - tail_ref.shape[0]:tm]

    @pl.when(j == pl.num_programs(1) - 1)
    def _():
        o_ref[...] = h_ref[...] + acc_ref[...]


def _ffn(h, g_ffn, w_up, conv_w, conv_b, w_down, p1, p2, tm, tf, seq_rows):
    m = h.shape[0]
    nj = D_FF // tf
    if seq_rows >= tm:
        tiles_per_seq = seq_rows // tm
        p1_spec = pl.BlockSpec((None, CONV_W - 1, tf), lambda i, j: (i // tiles_per_seq, 0, j))
        p2_spec = pl.BlockSpec((None, CONV_W - 1, tf), lambda i, j: (i // tiles_per_seq, 0, j))
    else:
        p1_spec = pl.BlockSpec((tm, tf), lambda i, j: (i, j))
        p2_spec = pl.BlockSpec((tm, tf), lambda i, j: (i, j))
    if seq_rows >= tm:
        tail_spec = pl.BlockSpec((None, SUBLANES, tf), lambda i, j: (i, 0, j))
        tail_shape = jax.ShapeDtypeStruct((m // tm, SUBLANES, D_FF), F32)
    else:
        tail_spec = pl.BlockSpec((tm, tf), lambda i, j: (i, j))
        tail_shape = jax.ShapeDtypeStruct((m, D_FF), F32)
    return pl.pallas_call(
        functools.partial(_ffn_kernel, tm=tm, tf=tf, seq_rows=seq_rows),
        grid=(m // tm, nj),
        in_specs=[pl.BlockSpec((tm, D_MODEL), lambda i, j: (i, 0)),
                  pl.BlockSpec((1, D_MODEL), lambda i, j: (0, 0)),
                  pl.BlockSpec((D_MODEL, tf), lambda i, j: (0, j)),
                  pl.BlockSpec((D_MODEL, tf), lambda i, j: (0, nj + j)),
                  pl.BlockSpec((CONV_W, tf), lambda i, j: (0, j)),
                  pl.BlockSpec((1, tf), lambda i, j: (0, j)),
                  pl.BlockSpec((tf, D_MODEL), lambda i, j: (j, 0)),
                  p1_spec, p2_spec],
        out_specs=[pl.BlockSpec((tm, D_MODEL), lambda i, j: (i, 0)),
                   tail_spec],
        out_shape=[jax.ShapeDtypeStruct((m, D_MODEL), F32), tail_shape],
        scratch_shapes=[pltpu.VMEM((tm, D_MODEL), BF16), pltpu.VMEM((tm, D_MODEL), F32),
                        pltpu.VMEM((CONV_W - 1, D_FF), F32)],
        compiler_params=_cparams(("arbitrary", "arbitrary")),
        name="conv_ffn",
    )(h, g_ffn, w_up, w_up, conv_w, conv_b, w_down, p1, p2)


def _ple_kernel(h_ref, p_ref, wg_ref, wp_ref, gp_ref, gf_ref, o_ref, *, final_norm):
    h = h_ref[...]
    gate = _sigmoid(_dot(h.astype(BF16), wg_ref[...]))
    pe = _rms(_dot(p_ref[...].astype(BF16), wp_ref[...]), gp_ref[...])
    h = h + gate * pe
    o_ref[...] = _rms(h, gf_ref[...]) if final_norm else h


def _ple(h, p, w_gate, w_proj, g_ple, g_final, tm, final_norm):
    m = h.shape[0]
    return pl.pallas_call(
        functools.partial(_ple_kernel, final_norm=final_norm),
        grid=(m // tm,),
        in_specs=[pl.BlockSpec((tm, D_MODEL), lambda i: (i, 0)),
                  pl.BlockSpec((tm, PLE_DIM), lambda i: (i, 0)),
                  pl.BlockSpec((D_MODEL, D_MODEL), lambda i: (0, 0)),
                  pl.BlockSpec((PLE_DIM, D_MODEL), lambda i: (0, 0)),
                  pl.BlockSpec((1, D_MODEL), lambda i: (0, 0)),
                  pl.BlockSpec((1, D_MODEL), lambda i: (0, 0))],
        out_specs=pl.BlockSpec((tm, D_MODEL), lambda i: (i, 0)),
        out_shape=jax.ShapeDtypeStruct((m, D_MODEL), F32),
        compiler_params=_cparams(("parallel",)),
        name="ple_norm",
    )(h, p, w_gate, w_proj, g_ple, g_final)


CMP_PAGES = 16
CMP_BATCH = 2
CHUNKS_PER_PAGE = PAGE_SIZE // CMP_STRIDE


def _cmp_sample_kernel(pt_ref, *refs):
    n_in = CMP_BATCH * CMP_PAGES
    perm_ref, w1_ref, w1p_ref, w2_ref, pe_ref, o_ref, carry_ref, x_ref, pe_hid_ref = refs[n_in:]

    @pl.when((pl.program_id(0) == 0) & (pl.program_id(1) == 0))
    def _():
        pe_hid_ref[...] = jnp.broadcast_to(_cmp_pe_hidden(pe_ref, w1_ref), pe_hid_ref.shape)

    @pl.when(pl.program_id(1) == 0)
    def _():
        carry_ref[...] = jnp.zeros_like(carry_ref)

    for e in range(CMP_BATCH):
        _cmp_sample_group(refs[e * CMP_PAGES:(e + 1) * CMP_PAGES], perm_ref, w1p_ref, w2_ref, pe_hid_ref[0:1, :],
                          o_ref.at[e], carry_ref.at[e], x_ref.at[e])


def _cmp_sample_group(page_refs, perm_ref, w1p_ref, w2_ref, pe_hid, o_ref, carry_ref, x_ref):
    nck = CMP_PAGES * CHUNKS_PER_PAGE
    perm = perm_ref[...]
    for k, r in enumerate(page_refs):
        xp = _dot_nt(perm, r[...].astype(BF16))
        for s in range(CMP_STRIDE):
            for n in range(NSA_KV_HEADS):
                x_ref[s, n, k * CHUNKS_PER_PAGE:(k + 1) * CHUNKS_PER_PAGE, :] = (
                    xp[s * CHUNKS_PER_PAGE:(s + 1) * CHUNKS_PER_PAGE, n * KV_SLAB:(n + 1) * KV_SLAB])
    acc = jnp.zeros((NSA_KV_HEADS * nck, 4 * CMP_HIDDEN), F32)
    for sp in range(CMP_STRIDE // 2):
        xs = jnp.concatenate([x_ref[2 * sp].reshape(NSA_KV_HEADS * nck, KV_SLAB),
                              x_ref[2 * sp + 1].reshape(NSA_KV_HEADS * nck, KV_SLAB)], axis=1)
        acc = acc + _dot(xs.astype(BF16), w1p_ref[sp])
    a = acc[:, :2 * CMP_HIDDEN]
    b = acc[:, 2 * CMP_HIDDEN:]
    a_prev = pltpu.roll(a, 1, axis=0)
    row = lax.broadcasted_iota(jnp.int32, (NSA_KV_HEADS * nck, 1), 0)
    for n in range(NSA_KV_HEADS):
        a_prev = jnp.where(row == n * nck, carry_ref[n:n + 1, :], a_prev)
        carry_ref[n:n + 1, :] = a[(n + 1) * nck - 1:(n + 1) * nck]
    hid = a_prev + b + pe_hid
    res = _dot(_gelu_tanh(hid).astype(BF16), w2_ref[...])
    for n in range(NSA_KV_HEADS):
        o_ref[n] = res[n * nck:(n + 1) * nck]


def _cmp_sample(cache_c, page_table_flat, w1, w2, pe, b, n_pages):
    n_grp = n_pages // CMP_PAGES
    nck = CMP_PAGES * CHUNKS_PER_PAGE
    assert b % CMP_BATCH == 0

    def page_spec(e, k):
        return pl.BlockSpec((None, KV_WIDTH, PAGE_SIZE),
                            lambda i, g, pt: (pt[(i * CMP_BATCH + e) * n_pages + g * CMP_PAGES + k], 0, 0))

    r = np.arange(PAGE_SIZE)
    perm_np = np.zeros((PAGE_SIZE, PAGE_SIZE), np.float32)
    perm_np[(r % CMP_STRIDE) * CHUNKS_PER_PAGE + r // CMP_STRIDE, r] = 1.0
    perm = jnp.asarray(perm_np, dtype=BF16)
    w1p = w1.reshape(CMP_STRIDE // 2, 2 * KV_SLAB, 4 * CMP_HIDDEN)

    grid_spec = pltpu.PrefetchScalarGridSpec(
        num_scalar_prefetch=1,
        grid=(b // CMP_BATCH, n_grp),
        in_specs=[page_spec(e, k) for e in range(CMP_BATCH) for k in range(CMP_PAGES)]
        + [pl.BlockSpec(perm.shape, lambda i, g, pt: (0, 0)),
           pl.BlockSpec(w1.shape, lambda i, g, pt: (0, 0, 0)),
           pl.BlockSpec(w1p.shape, lambda i, g, pt: (0, 0, 0)),
           pl.BlockSpec(w2.shape, lambda i, g, pt: (0, 0)),
           pl.BlockSpec(pe.shape, lambda i, g, pt: (0, 0))],
        out_specs=pl.BlockSpec((CMP_BATCH, NSA_KV_HEADS, nck, KV_SLAB), lambda i, g, pt: (i, 0, g, 0)),
        scratch_shapes=[pltpu.VMEM((CMP_BATCH, SUBLANES, 2 * CMP_HIDDEN), F32),
                        pltpu.VMEM((CMP_BATCH, CMP_STRIDE, NSA_KV_HEADS, nck, KV_SLAB), F32),
                        pltpu.VMEM((SUBLANES, 2 * CMP_HIDDEN), F32)],
    )
    return pl.pallas_call(
        _cmp_sample_kernel,
        grid_spec=grid_spec,
        out_shape=jax.ShapeDtypeStruct((b, NSA_KV_HEADS, n_pages * CHUNKS_PER_PAGE, KV_SLAB), F32),
        compiler_params=_cparams(("arbitrary", "arbitrary")),
        name="cmp_sample",
    )(page_table_flat, *([cache_c] * (CMP_BATCH * CMP_PAGES)), perm, w1, w1p, w2, pe)


def _sel_sample_kernel(slopes_ref, q_ref, kcv_ref, ov_ref, oc_ref, idx_ref, *, past_len, n_sel_pad, t_valid):
    tp = SUBLANES
    p_grps = []
    for n in range(NSA_KV_HEADS):
        oc, p_grp = _sel_sample_attend(n, slopes_ref, q_ref[:, n * Q_SLAB:(n + 1) * Q_SLAB] * SCALE, kcv_ref[n],
                                       past_len)
        oc_ref[n] = oc
        p_grps.append(p_grp)
    imp_all = jnp.dot(jnp.concatenate(p_grps, axis=0), ov_ref[...], preferred_element_type=F32, precision=HIGHEST)
    for n in range(NSA_KV_HEADS):
        idx_ref[n] = _sel_sample_pick(imp_all[n * tp:(n + 1) * tp], past_len, n_sel_pad, t_valid)


def _sel_sample_attend(n, slopes_ref, qf, kcv, past_len):
    tp = SUBLANES
    rows = NSA_GROUP * tp
    n_rows_c = past_len // CMP_STRIDE
    q = jnp.concatenate([qf[:, g * HEAD_DIM:(g + 1) * HEAD_DIM] for g in range(NSA_GROUP)], axis=0).astype(BF16)
    tok = lax.broadcasted_iota(jnp.int32, (rows, 1), 0) % tp
    qpos = past_len + tok
    slope = _slope_col(slopes_ref, n, tp, rows)
    kc = kcv[:, :HEAD_DIM].astype(BF16)
    vc = kcv[:, HEAD_DIM:].astype(BF16)
    s = _dot_nt(q, kc)
    cp = lax.broadcasted_iota(jnp.int32, (1, n_rows_c), 1)
    end = (cp - 1) * CMP_STRIDE + (CMP_BLOCK - 1)
    dist = (qpos - end).astype(F32)
    valid = (cp >= 1) & (dist >= 0)
    s = jnp.where(valid, s - slope * dist, NEG_INF)
    e = jnp.exp(s - jnp.max(s, axis=-1, keepdims=True))
    p = jnp.where(valid, e / jnp.sum(e, axis=-1, keepdims=True), 0.0)
    o_c = _dot(p.astype(BF16), vc)
    oc = jnp.concatenate([o_c[g * tp:(g + 1) * tp] for g in range(NSA_GROUP)], axis=1)
    p_grp = p[0:tp]
    for g in range(1, NSA_GROUP):
        p_grp = p_grp + p[g * tp:(g + 1) * tp]
    return oc, p_grp


def _sel_sample_pick(imp, past_len, n_sel_pad, t_valid):
    tp = SUBLANES
    nb_past = past_len // SEL_BLOCK
    n_tail = -(-t_valid // SEL_BLOCK)
    n_sel = nb_past + n_tail
    j = lax.broadcasted_iota(jnp.int32, (tp, n_sel_pad), 1)
    cur = (past_len + lax.broadcasted_iota(jnp.int32, (tp, 1), 0)) // SEL_BLOCK
    forced = (j == 0) | (j == cur) | (j == cur - 1)
    imp = jnp.where(forced, FORCE_SCORE, jnp.where(j > cur, -FORCE_SCORE, imp))
    imp = jnp.where(j < n_sel, imp, REMOVED_SCORE)
    _, picks = _top_blocks(imp, min(N_SELECT, n_sel))
    kcol = lax.broadcasted_iota(jnp.int32, (tp, N_SELECT), 1)
    idx = jnp.zeros((tp, N_SELECT), jnp.int32)
    for kk, pk in enumerate(picks):
        idx = jnp.where(kcol == kk, pk, idx)
    return idx


def _sel_sample(proj8, kcv, slopes, b, past_len, t_valid):
    n_rows_c = past_len // CMP_STRIDE
    nb_past = past_len // SEL_BLOCK
    n_sel_pad = -(-(nb_past + 1) // LANES) * LANES
    overlap = jnp.asarray(_overlap_counts(np.arange(n_rows_c)[:, None] - 1, np.arange(n_sel_pad)[None, :]))
    grid_spec = pltpu.PrefetchScalarGridSpec(
        num_scalar_prefetch=1,
        grid=(b,),
        in_specs=[pl.BlockSpec((SUBLANES, NSA_WIDTH), lambda i, sl: (i, COL_Q // NSA_WIDTH)),
                  pl.BlockSpec((None, NSA_KV_HEADS, n_rows_c, KV_SLAB), lambda i, sl: (i, 0, 0, 0)),
                  pl.BlockSpec(overlap.shape, lambda i, sl: (0, 0))],
        out_specs=[pl.BlockSpec((None, NSA_KV_HEADS, SUBLANES, Q_SLAB), lambda i, sl: (i, 0, 0, 0)),
                   pl.BlockSpec((None, NSA_KV_HEADS, SUBLANES, N_SELECT), lambda i, sl: (i, 0, 0, 0))],
    )
    return pl.pallas_call(
        functools.partial(_sel_sample_kernel, past_len=past_len, n_sel_pad=n_sel_pad, t_valid=t_valid),
        grid_spec=grid_spec,
        out_shape=[jax.ShapeDtypeStruct((b, NSA_KV_HEADS, SUBLANES, Q_SLAB), F32),
                   jax.ShapeDtypeStruct((b, NSA_KV_HEADS, SUBLANES, N_SELECT), jnp.int32)],
        compiler_params=_cparams(("parallel",)),
        name="sel_sample",
    )(slopes, proj8, kcv, overlap)


def _nsa_sample_kernel(idx_ref, pt_ref, slopes_ref, cache_ref, q_ref, tail_ref, wnew_ref, wcache_ref, oc_ref,
                       misc_ref, o_ref, pages_a, pages_b, sem_ref, *, past_len, t_valid, n_pages):
    n_blk = t_valid * N_SELECT
    nb_past = past_len // SEL_BLOCK
    per_page = PAGE_SIZE // SEL_BLOCK
    pairs = NSA_KV_HEADS // 2
    step = pl.program_id(0) * pairs + pl.program_id(1)
    n_steps = pl.num_programs(0) * pairs
    head_a = step * 2
    head_b = head_a + 1
    next_a = ((step + 1) % n_steps) * 2

    def page_copy(bh, j, buf, sem_i):
        t, kk = j // N_SELECT, j % N_SELECT
        jb = jnp.minimum(idx_ref[(bh * SUBLANES + t) * N_SELECT + kk], nb_past - 1)
        page = pt_ref[(bh // NSA_KV_HEADS) * n_pages + jb // per_page]
        return pltpu.make_async_copy(cache_ref.at[page, bh % NSA_KV_HEADS], buf.at[j], sem_ref.at[sem_i])

    def wait_all(bh, buf, sem_i):
        def body(j, c):
            page_copy(bh, j, buf, sem_i).wait()
            return c
        lax.fori_loop(0, n_blk, body, 0)

    def compute(bh, h2, buf):
        n = bh % NSA_KV_HEADS
        slope = _slope_col(slopes_ref, n, 1, SUBLANES)
        gates = _pick_head_gates(misc_ref[:, 0:GATE_W], n)
        tl = tail_ref[:, h2 * KV_SLAB:(h2 + 1) * KV_SLAB]
        wn = wnew_ref[:, h2 * KV_SLAB:(h2 + 1) * KV_SLAB]
        win_k = wcache_ref[h2, 0:HEAD_DIM, :].astype(BF16)
        win_v = wcache_ref[h2, HEAD_DIM:KV_SLAB, :].astype(BF16)
        rows_out = []
        for t in range(t_valid):
            picked = [buf.at[t * N_SELECT + kk] for kk in range(N_SELECT)]
            jbs = [idx_ref[(bh * SUBLANES + t) * N_SELECT + kk] for kk in range(N_SELECT)]
            rows_out.append(_nsa_sample_token(t, q_ref[t:t + 1, h2 * Q_SLAB:(h2 + 1) * Q_SLAB] * SCALE, slope, picked,
                                              jbs, tl, wn, win_k, win_v, oc_ref[h2, t:t + 1, :], gates[t:t + 1, :],
                                              past_len))
        rows_out.append(jnp.zeros((SUBLANES - t_valid, Q_SLAB), F32))
        o_ref[:, h2 * Q_SLAB:(h2 + 1) * Q_SLAB] = jnp.concatenate(rows_out, axis=0)

    @pl.when(step == 0)
    def _():
        for j in range(n_blk):
            page_copy(head_a, j, pages_a, 0).start()

    for j in range(n_blk):
        page_copy(head_b, j, pages_b, 1).start()
    wait_all(head_a, pages_a, 0)
    compute(head_a, 0, pages_a)
    for j in range(n_blk):
        page_copy(next_a, j, pages_a, 0).start()
    wait_all(head_b, pages_b, 1)
    compute(head_b, 1, pages_b)

    @pl.when(step == n_steps - 1)
    def _():
        wait_all(next_a, pages_a, 0)


def _nsa_sample_token(t, qrow, slope, picked, jbs, tl, wn, win_k, win_v, oc_row, gates, past_len):
    nb_past = past_len // SEL_BLOCK
    g8 = SUBLANES
    q = jnp.concatenate([qrow[:, g * HEAD_DIM:(g + 1) * HEAD_DIM] for g in range(NSA_GROUP)]
                        + [jnp.zeros((g8 - NSA_GROUP, HEAD_DIM), F32)], axis=0).astype(BF16)
    qpos = past_len + t

    per_page = PAGE_SIZE // SEL_BLOCK
    k_all = jnp.concatenate([r[0:HEAD_DIM, :] for r in picked], axis=1).astype(BF16)
    v_all = jnp.concatenate([r[HEAD_DIM:KV_SLAB, :] for r in picked], axis=1).astype(BF16)
    jb_row = jnp.concatenate([jnp.full((1, PAGE_SIZE), jb, jnp.int32) for jb in jbs], axis=1)
    tail_count = jnp.zeros((), jnp.int32)
    for jb in jbs:
        tail_count = tail_count + (jb >= nb_past).astype(jnp.int32)
    lane = lax.broadcasted_iota(jnp.int32, (1, N_SELECT * PAGE_SIZE), 1) % PAGE_SIZE
    kpos = (jb_row // per_page) * PAGE_SIZE + lane
    dist = (qpos - kpos).astype(F32)
    ok = (dist >= 0) & (kpos // SEL_BLOCK == jb_row) & (jb_row < nb_past)
    sc = _dot(q, k_all) - slope * dist
    parts = [(jnp.where(ok, sc, NEG_INF), ok, v_all, True)]
    lane8 = lax.broadcasted_iota(jnp.int32, (1, SUBLANES), 1)
    dist = (t - lane8).astype(F32)
    ok = (dist >= 0) & (jnp.full((1, SUBLANES), tail_count, jnp.int32) > 0)
    sc = _dot_nt(q, tl[:, :HEAD_DIM].astype(BF16)) - slope * dist
    parts.append((jnp.where(ok, sc, NEG_INF), ok, tl[:, HEAD_DIM:].astype(BF16), False))

    def softmax_av(parts):
        m = parts[0][0].max(axis=-1, keepdims=True)
        for sc, _, _, _ in parts[1:]:
            m = jnp.maximum(m, sc.max(axis=-1, keepdims=True))
        l = jnp.zeros((g8, 1), F32)
        acc = jnp.zeros((g8, HEAD_DIM), F32)
        for sc, ok, v, v_transposed in parts:
            pr = jnp.where(ok, jnp.exp(sc - m), 0.0)
            l = l + pr.sum(axis=-1, keepdims=True)
            acc = acc + (_dot_nt(pr.astype(BF16), v) if v_transposed else _dot(pr.astype(BF16), v))
        return acc / l

    o_s = softmax_av(parts)

    buf_len = win_k.shape[1]
    lane_w = lax.broadcasted_iota(jnp.int32, (1, buf_len), 1)
    dist_c = (qpos - (past_len - buf_len + lane_w)).astype(F32)
    ok_c = (dist_c >= 0) & (dist_c < WINDOW)
    sc_c = jnp.where(ok_c, _dot(q, win_k) - slope * dist_c, NEG_INF)
    dist_n = (t - lane8).astype(F32)
    ok_n = (dist_n >= 0) & (dist_n < WINDOW)
    sc_n = jnp.where(ok_n, _dot_nt(q, wn[:, :HEAD_DIM].astype(BF16)) - slope * dist_n, NEG_INF)
    o_w = softmax_av([(sc_c, ok_c, win_v, True), (sc_n, ok_n, wn[:, HEAD_DIM:].astype(BF16), False)])

    outs = []
    for g in range(NSA_GROUP):
        outs.append(gates[:, 3 * g:3 * g + 1] * oc_row[:, g * HEAD_DIM:(g + 1) * HEAD_DIM]
                    + gates[:, 3 * g + 1:3 * g + 2] * o_s[g:g + 1]
                    + gates[:, 3 * g + 2:3 * g + 3] * o_w[g:g + 1])
    return jnp.concatenate(outs, axis=1)


def _nsa_sample(proj8, cache_s, cache_w, o_c, idx_flat, page_table_flat, slopes, b, t_valid, past_len):
    n_pages = past_len // PAGE_SIZE
    buf_len = cache_w.shape[-1]
    pairs = NSA_KV_HEADS // 2
    n_blk = t_valid * N_SELECT
    pages = pltpu.VMEM((n_blk, KV_SLAB, PAGE_SIZE), F32)
    grid_spec = pltpu.PrefetchScalarGridSpec(
        num_scalar_prefetch=3,
        grid=(b, pairs),
        in_specs=[pl.BlockSpec(memory_space=pl.ANY),
                  pl.BlockSpec((SUBLANES, 2 * Q_SLAB), lambda i, p, *_: (i, COL_Q // (2 * Q_SLAB) + p)),
                  pl.BlockSpec((SUBLANES, 2 * KV_SLAB), lambda i, p, *_: (i, COL_KVS // (2 * KV_SLAB) + p)),
                  pl.BlockSpec((SUBLANES, 2 * KV_SLAB), lambda i, p, *_: (i, COL_KVW // (2 * KV_SLAB) + p)),
                  pl.BlockSpec((None, 2, KV_SLAB, buf_len), lambda i, p, *_: (i, p, 0, 0)),
                  pl.BlockSpec((None, 2, SUBLANES, Q_SLAB), lambda i, p, *_: (i, p, 0, 0)),
                  pl.BlockSpec((SUBLANES, LANES), lambda i, p, *_: (i, COL_MISC // LANES))],
        out_specs=pl.BlockSpec((SUBLANES, 2 * Q_SLAB), lambda i, p, *_: (i, p)),
        scratch_shapes=[pages, pages, pltpu.SemaphoreType.DMA((2,))],
    )
    return pl.pallas_call(
        functools.partial(_nsa_sample_kernel, past_len=past_len, t_valid=t_valid, n_pages=n_pages),
        grid_spec=grid_spec,
        out_shape=jax.ShapeDtypeStruct((b * SUBLANES, NSA_WIDTH), F32),
        compiler_params=_cparams(("arbitrary", "arbitrary")),
        name="nsa_sample",
    )(idx_flat, page_table_flat, slopes, cache_s, proj8, proj8, proj8, cache_w, o_c, proj8)


def _alibi_slopes():
    h = np.arange(1, NSA_HEADS + 1, dtype=np.float32)
    return jnp.asarray(2.0 ** (-8.0 * h / NSA_HEADS), dtype=F32)


def _permute_w_in(w_in):
    offs = np.cumsum([0, NSA_WIDTH, KV_WIDTH, KV_WIDTH, KV_WIDTH, GATE_W, GLA_HEADS * GLA_DK, GLA_HEADS * GLA_DK,
                      GLA_WIDTH, GLA_WIDTH, GLA_RANK])
    w_bf = w_in.astype(BF16)
    piece = [w_bf[:, offs[k]:offs[k + 1]] for k in range(10)]
    q, kvc, kvs, kvw, gl, gq, gk, gv, gr, ga = piece
    pad = jnp.zeros((w_in.shape[0], LANES - GATE_W - GLA_RANK), BF16)
    return jnp.concatenate([q, gv, gr, kvc, kvs, kvw, gq, gk, gl, ga, pad], axis=1)


def _row_tile(m, pref):
    return pref if m % pref == 0 else m


def _dense_tail(h_in, o_nsa, o_gla, p_emb, wts, conv_p1, conv_p2, seq_rows, last_layer, tm):
    (g_nsa, w_out, g_ffn, w_up, conv_w, conv_b, w_down, w_ple_proj, g_ple, w_ple_gate, g_final) = wts
    h = _out_proj(o_nsa, o_gla, h_in, g_nsa, w_out, tm, D_MODEL // 2)
    h, tails = _ffn(h, g_ffn, w_up, conv_w, conv_b, w_down, conv_p1, conv_p2, tm, FFN_COLS, seq_rows)
    y = _ple(h, p_emb, w_ple_gate, w_ple_proj, g_ple, g_final, tm, last_layer)
    return y, tails


def kernel(x_prompt, x_sample, p_prompt, p_sample, cache_cmp_kv, cache_sel_kv, cache_win_kv, state_gla, state_ffn_conv, page_table, g_attn, w_in, w_cmp_k1, w_cmp_k2, pe_cmp_k, w_cmp_v1, w_cmp_v2, pe_cmp_v, w_gla_a2, b_gla_a, g_nsa_out, g_gla_out, w_out, g_ffn, w_up, conv_w, conv_b, w_down, w_ple_proj, g_ple, w_ple_gate, g_final):
    depth = w_in.shape[0]
    bp, tp, _ = x_prompt.shape
    bs, ts, _ = x_sample.shape
    n_pages = page_table.shape[1]
    past_len = n_pages * PAGE_SIZE
    n_pool = cache_cmp_kv.shape[1]
    assert tp % TQ == 0 and tp >= WINDOW and ts <= SUBLANES and ts <= SEL_BLOCK and ts >= CONV_W - 1
    assert n_pages % CMP_PAGES == 0 and tp // SEL_BLOCK >= N_SELECT
    slopes = _alibi_slopes()
    pt_flat = page_table.reshape(-1).astype(jnp.int32)
    kv_shape = (NSA_KV_HEADS, 2, HEAD_DIM)

    hp = x_prompt.reshape(bp * tp, D_MODEL)
    hs = x_sample.reshape(bs * ts, D_MODEL)
    new_p = [[] for _ in range(5)]
    new_s = [[] for _ in range(5)]
    tm_p = _row_tile(bp * tp, DENSE_ROWS)
    tm_s = bs * ts
    for i in range(depth):
        last = i == depth - 1
        w_in_p = _permute_w_in(w_in[i])
        g_a = g_attn[i].reshape(1, D_MODEL)
        w1c, w2c, pec = _cmp_weights(w_cmp_k1[i], w_cmp_k2[i], pe_cmp_k[i], w_cmp_v1[i], w_cmp_v2[i], pe_cmp_v[i])
        b_a = b_gla_a[i].reshape(1, -1)
        g_go = g_gla_out[i].reshape(1, GLA_DV)
        wts = (g_nsa_out[i].reshape(1, -1), w_out[i].astype(BF16), g_ffn[i].reshape(1, -1), w_up[i].astype(BF16),
               conv_w[i], conv_b[i].reshape(1, -1), w_down[i].astype(BF16), w_ple_proj[i].astype(BF16),
               g_ple[i].reshape(1, -1), w_ple_gate[i].astype(BF16), g_final.reshape(1, -1))

        proj = _norm_matmul(hp, g_a, w_in_p, _row_tile(bp * tp, IN_PROJ_ROWS), IN_PROJ_COLS)
        kcv = _cmp_prompt(proj, bp, tp, w1c, w2c, pec)
        o_nsa = _nsa_prompt(proj, kcv, slopes, bp, tp)
        s0 = jnp.zeros((bp, GLA_HEADS, GLA_DK, GLA_DV), F32)
        o_gla, s_new = _gla(proj, w_gla_a2[i], b_a, g_go, s0, bp, tp, 64, 16, 64)
        zbuf = jnp.zeros((bp, CONV_W - 1, D_FF), F32)
        hp, tails = _dense_tail(hp, o_nsa, o_gla, p_prompt[i].reshape(bp * tp, PLE_DIM), wts, zbuf, zbuf, tp, last, tm_p)
        proj3 = proj.reshape(bp, tp, D_IN_PAD)
        new_p[0].append(proj3[:, :, COL_KVC:COL_KVC + KV_WIDTH].reshape((bp, tp) + kv_shape))
        new_p[1].append(proj3[:, :, COL_KVS:COL_KVS + KV_WIDTH].reshape((bp, tp) + kv_shape))
        new_p[2].append(proj3[:, tp - WINDOW:, COL_KVW:COL_KVW + KV_WIDTH].reshape((bp, WINDOW) + kv_shape))
        new_p[3].append(s_new)
        tiles_per_seq = tp // tm_p
        new_p[4].append(tails.reshape(bp, tiles_per_seq, SUBLANES, D_FF)[:, -1, SUBLANES - (CONV_W - 1):, :])

        proj_s = _norm_matmul(hs, g_a, w_in_p, tm_s, IN_PROJ_COLS)
        proj8 = jnp.pad(proj_s.reshape(bs, ts, D_IN_PAD), ((0, 0), (0, SUBLANES - ts), (0, 0))).reshape(bs * SUBLANES, D_IN_PAD)
        cache_c = jnp.transpose(cache_cmp_kv[i], (0, 2, 3, 4, 1)).reshape(n_pool, KV_WIDTH, PAGE_SIZE)
        cache_s = jnp.transpose(cache_sel_kv[i], (0, 2, 3, 4, 1)).reshape(n_pool, NSA_KV_HEADS, KV_SLAB, PAGE_SIZE)
        cache_w = jnp.transpose(cache_win_kv[i], (0, 2, 3, 4, 1)).reshape(bs, NSA_KV_HEADS, KV_SLAB, -1)
        kcv_s = _cmp_sample(cache_c, pt_flat, w1c, w2c, pec, bs, n_pages)
        o_c, idx = _sel_sample(proj8, kcv_s, slopes, bs, past_len, ts)
        o_nsa_s = _nsa_sample(proj8, cache_s, cache_w, o_c, idx.reshape(-1), pt_flat, slopes, bs, ts, past_len)
        o_nsa_s = o_nsa_s.reshape(bs, SUBLANES, NSA_WIDTH)[:, :ts].reshape(bs * ts, NSA_WIDTH)
        o_gla_s, s_new_s = _gla(proj8, w_gla_a2[i], b_a, g_go, state_gla[i].astype(F32), bs, SUBLANES, SUBLANES, SUBLANES, ts)
        o_gla_s = o_gla_s.reshape(bs, SUBLANES, GLA_WIDTH)[:, :ts].reshape(bs * ts, GLA_WIDTH)
        buf = state_ffn_conv[i]
        zrow = jnp.zeros((bs, ts - 1, D_FF), F32)
        p1 = jnp.concatenate([buf[:, 1:2], zrow], axis=1).reshape(bs * ts, D_FF)
        p2 = jnp.concatenate([buf, jnp.zeros((bs, ts - 2, D_FF), F32)], axis=1).reshape(bs * ts, D_FF)
        hs, tails_s = _dense_tail(hs, o_nsa_s, o_gla_s, p_sample[i].reshape(bs * ts, PLE_DIM), wts, p1, p2, ts, last, tm_s)
        ps3 = proj_s.reshape(bs, ts, D_IN_PAD)
        new_s[0].append(ps3[:, :, COL_KVC:COL_KVC + KV_WIDTH].reshape((bs, ts) + kv_shape))
        new_s[1].append(ps3[:, :, COL_KVS:COL_KVS + KV_WIDTH].reshape((bs, ts) + kv_shape))
        new_s[2].append(ps3[:, :, COL_KVW:COL_KVW + KV_WIDTH].reshape((bs, ts) + kv_shape))
        new_s[3].append(s_new_s)
        new_s[4].append(tails_s.reshape(bs, ts, D_FF)[:, ts - (CONV_W - 1):, :])

    y_prompt = hp.reshape(bp, tp, D_MODEL)
    y_sample = hs.reshape(bs, ts, D_MODEL)
    cmp_p, sel_p, win_p, gla_p, conv_p = [jnp.stack(l) for l in new_p]
    cmp_s, sel_s, win_s, gla_s, conv_s = [jnp.stack(l) for l in new_s]
    return (y_prompt, y_sample, cmp_p, sel_p, win_p, gla_p, conv_p, cmp_s, sel_s, win_s, gla_s, conv_s)
```

```python
import functools

import numpy as np
import jax
import jax.numpy as jnp
from jax import lax
from jax.experimental import pallas as pl
from jax.experimental.pallas import tpu as pltpu

F32 = jnp.float32
BF16 = jnp.bfloat16
HIGHEST = lax.Precision.HIGHEST

D_MODEL = 2048
PAGE_SIZE = 128
NSA_HEADS = 16
NSA_KV_HEADS = 4
NSA_GROUP = NSA_HEADS // NSA_KV_HEADS
HEAD_DIM = 64
CMP_BLOCK = 32
CMP_STRIDE = 16
CMP_HIDDEN = 2 * HEAD_DIM
SEL_BLOCK = 64
N_SELECT = 16
WINDOW = 512
TQ = 512
GLA_HEADS = 4
GLA_DK = 128
GLA_DV = 256
GLA_RANK = 16
GLA_TAU = 16.0
D_FF = 5632
CONV_W = 3
PLE_DIM = 256
EPS = 1e-6
NEG_INF = -1e30
FORCE_SCORE = 1e9
REMOVED_SCORE = -3e38
SCALE = HEAD_DIM ** -0.5
LOG2E = np.float32(1.4426950408889634)

NSA_WIDTH = NSA_HEADS * HEAD_DIM
GLA_WIDTH = GLA_HEADS * GLA_DV
KV_WIDTH = 2 * NSA_KV_HEADS * HEAD_DIM
KV_SLAB = 2 * HEAD_DIM
Q_SLAB = NSA_GROUP * HEAD_DIM
GATE_W = 3 * NSA_HEADS

LANES = 128
SUBLANES = 8
VMEM_LIMIT = 56 * 1024 * 1024

COL_Q = 0
COL_GV = COL_Q + NSA_WIDTH
COL_GR = COL_GV + GLA_WIDTH
COL_KVC = COL_GR + GLA_WIDTH
COL_KVS = COL_KVC + KV_WIDTH
COL_KVW = COL_KVS + KV_WIDTH
COL_GQ = COL_KVW + KV_WIDTH
COL_GK = COL_GQ + GLA_HEADS * GLA_DK
COL_MISC = COL_GK + GLA_HEADS * GLA_DK
D_IN_PAD = COL_MISC + LANES
MISC_GA = GATE_W

DENSE_ROWS = 512
IN_PROJ_ROWS = 512
IN_PROJ_COLS = D_IN_PAD // 5
FFN_COLS = 512


def _cparams(sem):
    return pltpu.CompilerParams(dimension_semantics=sem, vmem_limit_bytes=VMEM_LIMIT)


def _rms(x, g):
    return x * lax.rsqrt(jnp.mean(x * x, axis=-1, keepdims=True) + EPS) * g


def _gelu_tanh(x):
    return 0.5 * x * (1.0 + jnp.tanh(np.float32(np.sqrt(2.0 / np.pi)) * (x + 0.044715 * (x * x * x))))


def _sigmoid(x):
    return 1.0 / (1.0 + jnp.exp(-x))


def _dot(a, b):
    return jnp.dot(a, b, preferred_element_type=F32)


def _dot_nt(a, b):
    return lax.dot_general(a, b, (((1,), (1,)), ((), ())), preferred_element_type=F32)


def _dot_tn(a, b):
    return lax.dot_general(a, b, (((0,), (0,)), ((), ())), preferred_element_type=F32)


def _norm_matmul_kernel(x_ref, g_ref, w_ref, o_ref, xn_ref):
    @pl.when(pl.program_id(1) == 0)
    def _():
        xn_ref[...] = _rms(x_ref[...], g_ref[...]).astype(BF16)

    o_ref[...] = _dot(xn_ref[...], w_ref[...])


def _norm_matmul(x, g, w, tm, tn):
    m, d = x.shape
    n = w.shape[1]
    return pl.pallas_call(
        _norm_matmul_kernel,
        grid=(m // tm, n // tn),
        in_specs=[pl.BlockSpec((tm, d), lambda i, j: (i, 0)),
                  pl.BlockSpec((1, d), lambda i, j: (0, 0)),
                  pl.BlockSpec((d, tn), lambda i, j: (0, j))],
        out_specs=pl.BlockSpec((tm, tn), lambda i, j: (i, j)),
        out_shape=jax.ShapeDtypeStruct((m, n), F32),
        scratch_shapes=[pltpu.VMEM((tm, d), BF16)],
        compiler_params=_cparams(("parallel", "arbitrary")),
        name="in_proj",
    )(x, g, w)


def _cmp_pe_hidden(pe_ref, w1_ref):
    span = CMP_BLOCK // CMP_STRIDE
    acc = jnp.zeros((SUBLANES, 2 * CMP_HIDDEN), F32)
    for j in range(span):
        for s in range(CMP_STRIDE):
            row = jnp.broadcast_to(pe_ref[pl.ds(j * CMP_STRIDE + s, 1), :], (SUBLANES, KV_SLAB)).astype(BF16)
            acc = acc + _dot(row, w1_ref[s][:, j * 2 * CMP_HIDDEN:(j + 1) * 2 * CMP_HIDDEN])
    return acc[0:1, :]


def _cmp_prompt_kernel(x_ref, w1_ref, w2_ref, pe_ref, o_ref, *, n16):
    acc = jnp.zeros((n16, 4 * CMP_HIDDEN), F32)
    for s in range(CMP_STRIDE):
        xs = x_ref[pl.ds(s, n16, stride=CMP_STRIDE), :].astype(BF16)
        acc = acc + _dot(xs, w1_ref[s])
    a = acc[:, :2 * CMP_HIDDEN]
    b_next = pltpu.roll(acc[:, 2 * CMP_HIDDEN:], n16 - 1, axis=0)
    hid = a + b_next + _cmp_pe_hidden(pe_ref, w1_ref)
    o_ref[...] = _dot(_gelu_tanh(hid).astype(BF16), w2_ref[...])


def _cmp_weights(w_k1, w_k2, pe_k, w_v1, w_v2, pe_v):
    z = jnp.zeros((CMP_BLOCK, HEAD_DIM, CMP_HIDDEN), F32)
    wl = jnp.concatenate([jnp.concatenate([w_k1, z], axis=2), jnp.concatenate([z, w_v1], axis=2)], axis=1)
    w1 = jnp.concatenate([wl[:CMP_STRIDE], wl[CMP_STRIDE:]], axis=2).astype(BF16)
    z2 = jnp.zeros((CMP_HIDDEN, HEAD_DIM), F32)
    w2 = jnp.concatenate([jnp.concatenate([w_k2, z2], axis=1), jnp.concatenate([z2, w_v2], axis=1)], axis=0)
    pe = jnp.concatenate([pe_k, pe_v], axis=1)
    return w1, w2.astype(BF16), pe


def _cmp_prompt(proj, b, t, w1, w2, pe):
    n16 = t // CMP_STRIDE
    col0 = COL_KVC // KV_SLAB
    return pl.pallas_call(
        functools.partial(_cmp_prompt_kernel, n16=n16),
        grid=(b, NSA_KV_HEADS),
        in_specs=[pl.BlockSpec((t, KV_SLAB), lambda i, n: (i, col0 + n)),
                  pl.BlockSpec(w1.shape, lambda i, n: (0, 0, 0)),
                  pl.BlockSpec(w2.shape, lambda i, n: (0, 0)),
                  pl.BlockSpec(pe.shape, lambda i, n: (0, 0))],
        out_specs=pl.BlockSpec((None, None, n16, KV_SLAB), lambda i, n: (i, n, 0, 0)),
        out_shape=jax.ShapeDtypeStruct((b, NSA_KV_HEADS, n16, KV_SLAB), F32),
        compiler_params=_cparams(("parallel", "parallel")),
        name="cmp_prompt",
    )(proj, w1, w2, pe)


def _overlap_counts(c, j):
    per_sel = SEL_BLOCK // CMP_STRIDE
    ov = sum(((c + k >= per_sel * j) & (c + k < per_sel * (j + 1))) for k in range(CMP_BLOCK // CMP_STRIDE))
    return np.where(c >= 0, ov, 0).astype(np.float32)


def _top_blocks(imp, n_pick):
    rows, nsel = imp.shape
    j = lax.broadcasted_iota(jnp.int32, (rows, nsel), 1).astype(F32)
    mask = jnp.zeros((rows, nsel), F32)
    picks = []
    for _ in range(n_pick):
        m = jnp.max(imp, axis=-1, keepdims=True)
        jmin = jnp.min(jnp.where(imp == m, j, float(nsel)), axis=-1, keepdims=True)
        hit = j == jmin
        mask = jnp.where(hit, 1.0, mask)
        imp = jnp.where(hit, REMOVED_SCORE, imp)
        picks.append(jmin.astype(jnp.int32))
    return mask, picks


def _slope_col(slopes_ref, n, rows_per_head, rows):
    g = lax.broadcasted_iota(jnp.int32, (rows, 1), 0) // rows_per_head
    col = jnp.zeros((rows, 1), F32)
    for gg in range(NSA_GROUP):
        col = jnp.where(g == gg, slopes_ref[n * NSA_GROUP + gg], col)
    return col


def _pick_head_gates(gl, n):
    out = jnp.zeros((gl.shape[0], 3 * NSA_GROUP), F32)
    for nn in range(NSA_KV_HEADS):
        out = jnp.where(n == nn, gl[:, nn * 3 * NSA_GROUP:(nn + 1) * 3 * NSA_GROUP], out)
    return _sigmoid(out)


MASK_BIAS = -131072.0
M_FLOOR = -65536.0
POS_HI, POS_LO = HEAD_DIM, HEAD_DIM + 3
KEY_TILE = TQ


def _slope_features():
    h = np.arange(1, NSA_HEADS + 1, dtype=np.float32)
    slopes = (2.0 ** (-8.0 * h / NSA_HEADS)).astype(np.float32)
    tab = np.zeros((NSA_HEADS, LANES), np.float32)
    rest = slopes * LOG2E
    for c in range(3):
        piece = rest.astype(BF16).astype(np.float32)
        tab[:, POS_HI + c] = piece
        tab[:, POS_LO + c] = piece
        rest = rest - piece
    assert not rest.any()
    return jnp.asarray(tab)


def _kv_prep_kernel(kvs_ref, kvw_ref, ks_ref, vs_ref, kw_ref, vw_ref, *, t):
    lane = lax.broadcasted_iota(jnp.int32, (t, LANES), 1)
    pos = lax.broadcasted_iota(jnp.int32, (t, LANES), 0)
    hi = ((pos // SEL_BLOCK) * SEL_BLOCK).astype(F32)
    lo = (pos % SEL_BLOCK).astype(F32)
    feat = jnp.where((lane >= POS_HI) & (lane < POS_HI + 3), hi,
                     jnp.where((lane >= POS_LO) & (lane < POS_LO + 3), lo, 0.0))
    ones_col = jnp.where(lane == HEAD_DIM, 1.0, 0.0)
    for src, k_out, v_out in ((kvs_ref, ks_ref, vs_ref), (kvw_ref, kw_ref, vw_ref)):
        x = src[...]
        k_out[:, 0:LANES] = jnp.where(lane < HEAD_DIM, x, feat).astype(BF16)
        v_out[...] = jnp.where(lane < HEAD_DIM, pltpu.roll(x, HEAD_DIM, axis=1), ones_col).astype(BF16)
    ks_ref[:, LANES:2 * LANES] = jnp.where(lane == pos // SEL_BLOCK, 1.0, 0.0).astype(BF16)


def _kv_prep(proj, b, t):
    spec_in = lambda col: pl.BlockSpec((t, KV_SLAB), lambda i, n: (i, col // KV_SLAB + n))
    spec_out = lambda w: pl.BlockSpec((None, None, t, w), lambda i, n: (i, n, 0, 0))
    shape = lambda w: jax.ShapeDtypeStruct((b, NSA_KV_HEADS, t, w), BF16)
    return pl.pallas_call(
        functools.partial(_kv_prep_kernel, t=t),
        grid=(b, NSA_KV_HEADS),
        in_specs=[spec_in(COL_KVS), spec_in(COL_KVW)],
        out_specs=[spec_out(2 * LANES), spec_out(LANES), spec_out(LANES), spec_out(LANES)],
        out_shape=[shape(2 * LANES), shape(LANES), shape(LANES), shape(LANES)],
        compiler_params=_cparams(("parallel", "parallel")),
        name="kv_prep",
    )(proj, proj)


def _nsa_prompt_kernel(slopes_ref, q_ref, kcv_ref, ks_ref, vs_ref, kw_ref, vw_ref, misc_ref, sfeat_ref, ovt_ref, o_ref,
                       *, n16, n_sel):
    n = pl.program_id(1)
    qb = pl.program_id(2)
    rows = NSA_GROUP * TQ
    qf = q_ref[...] * SCALE
    q = jnp.concatenate([qf[:, g * HEAD_DIM:(g + 1) * HEAD_DIM] for g in range(NSA_GROUP)], axis=0).astype(BF16)
    tok = lax.broadcasted_iota(jnp.int32, (rows, 1), 0) % TQ
    qpos = qb * TQ + tok
    slope = _slope_col(slopes_ref, n, TQ, rows)

    kcv = kcv_ref[...]
    kc = kcv[:, :HEAD_DIM].astype(BF16)
    vc = kcv[:, HEAD_DIM:].astype(BF16)
    s = _dot_nt(q, kc)
    end = lax.broadcasted_iota(jnp.int32, (1, n16), 1) * CMP_STRIDE + (CMP_BLOCK - 1)
    dist = (qpos - end).astype(F32)
    valid = (dist >= 0) & (end < n16 * CMP_STRIDE)
    s = jnp.where(valid, s - slope * dist, NEG_INF)
    e = jnp.exp(s - jnp.max(s, axis=-1, keepdims=True))
    p = jnp.where(valid, e / jnp.sum(e, axis=-1, keepdims=True), 0.0)
    o_c = _dot(p.astype(BF16), vc)
    p_grp = p[0:TQ]
    for g in range(1, NSA_GROUP):
        p_grp = p_grp + p[g * TQ:(g + 1) * TQ]
    imp = lax.dot_general(ovt_ref[...], p_grp, (((1,), (1,)), ((), ())),
                          preferred_element_type=F32, precision=HIGHEST)

    jj = lax.broadcasted_iota(jnp.int32, (n_sel, TQ), 0)
    cur = (qb * TQ + lax.broadcasted_iota(jnp.int32, (1, TQ), 1)) // SEL_BLOCK
    forced = (jj == 0) | (jj == cur) | (jj == cur - 1)
    imp = jnp.where(forced, FORCE_SCORE, jnp.where(jj > cur, -FORCE_SCORE, imp))
    beaten = jnp.zeros((n_sel, TQ), F32)
    for jp in range(n_sel):
        other = imp[jp:jp + 1, :]
        beats = (other > imp) | ((other == imp) & (jj > jp))
        beaten = beaten + jnp.where(beats, 1.0, 0.0)
    not_picked = jnp.where(beaten < min(N_SELECT, n_sel), 0.0, 1.0)
    if n_sel < LANES:
        not_picked = jnp.concatenate([not_picked, jnp.zeros((LANES - n_sel, TQ), F32)], axis=0)
    q_bias = jnp.transpose(not_picked) * MASK_BIAS

    qf2 = qf * LOG2E
    q_main = jnp.concatenate(
        [jnp.concatenate([qf2[:, g * HEAD_DIM:(g + 1) * HEAD_DIM],
                          jnp.broadcast_to(sfeat_ref[pl.ds(n * NSA_GROUP + g, 1), HEAD_DIM:LANES],
                                           (TQ, LANES - HEAD_DIM))], axis=1)
         for g in range(NSA_GROUP)], axis=0)
    groups = [slice(g * TQ, (g + 1) * TQ) for g in range(NSA_GROUP)]
    q_win = [q_main[r].astype(BF16) for r in groups]
    q_sel = [jnp.concatenate([q_main[r], q_bias], axis=1).astype(BF16) for r in groups]
    tpos = qb * TQ + lax.broadcasted_iota(jnp.int32, (TQ, 1), 0)

    def flash_step(carry, scores, v):
        out = []
        for (m, acc), sc in zip(carry, scores):
            m_new = jnp.maximum(m, jnp.max(sc, axis=-1, keepdims=True))
            pr = jnp.exp2(sc - m_new).astype(BF16)
            out.append((m_new, jnp.exp2(m - m_new) * acc + _dot(pr, v)))
        return tuple(out)

    def finish(carry):
        return [(acc[:, :HEAD_DIM], acc[:, HEAD_DIM:HEAD_DIM + 1]) for _, acc in carry]

    init = tuple((jnp.full((TQ, 1), M_FLOOR, F32), jnp.zeros((TQ, LANES), F32)) for _ in groups)

    def sel_scores(r, keep=None):
        k = ks_ref[r, :]
        sc = [_dot_nt(qg, k) for qg in q_sel]
        return sc if keep is None else [jnp.where(keep, s_, NEG_INF) for s_ in sc]

    def key_rows(kt):
        return pl.ds(pl.multiple_of(kt * KEY_TILE, KEY_TILE), KEY_TILE)

    def sel_body(kt, carry):
        return flash_step(carry, sel_scores(key_rows(kt)), vs_ref[key_rows(kt), :])

    n_full = (qb * TQ) // KEY_TILE
    carry = lax.fori_loop(0, n_full, sel_body, init)
    sc = sel_scores(key_rows(n_full))
    kpos = n_full * KEY_TILE + lax.broadcasted_iota(jnp.int32, (1, KEY_TILE), 1)
    sc = [jnp.where(kpos <= tpos, s_, NEG_INF) for s_ in sc]
    o_s = finish(flash_step(carry, sc, vs_ref[key_rows(n_full), :]))

    n_win = WINDOW // TQ + 1
    lane_q = lax.broadcasted_iota(jnp.int32, (1, TQ), 1)
    win_sc = [[] for _ in groups]
    win_v = []
    for rel in range(n_win):
        kt = qb - (n_win - 1) + rel
        rows_k = pl.ds(pl.multiple_of(jnp.maximum(kt, 0) * TQ, TQ), TQ)
        k = kw_ref[rows_k, :]
        win_v.append(vw_ref[rows_k, :])
        dist = tpos - (kt * TQ + lane_q)
        keep = jnp.full((1, TQ), kt, jnp.int32) >= 0
        if rel == 0:
            keep = keep & (dist < WINDOW)
        if rel == n_win - 1:
            keep = keep & (dist >= 0)
        for g, qg in enumerate(q_win):
            win_sc[g].append(jnp.where(keep, _dot_nt(qg, k), NEG_INF))
    o_w = finish(flash_step(init, [jnp.concatenate(s_, axis=1) for s_ in win_sc], jnp.concatenate(win_v, axis=0)))

    gates = _pick_head_gates(misc_ref[:, 0:GATE_W], n)
    outs = []
    for g, r in enumerate(groups):
        (u_s, l_s), (u_w, l_w) = o_s[g], o_w[g]
        outs.append(gates[:, 3 * g:3 * g + 1] * o_c[r] + (gates[:, 3 * g + 1:3 * g + 2] / l_s) * u_s
                    + (gates[:, 3 * g + 2:3 * g + 3] / l_w) * u_w)
    o_ref[...] = jnp.concatenate(outs, axis=1)


def _nsa_prompt(proj, kcv, slopes, b, t):
    n16 = t // CMP_STRIDE
    n_sel = t // SEL_BLOCK
    nqb = t // TQ
    assert n_sel <= LANES and t % KEY_TILE == 0
    k_sel, v_sel, k_win, v_win = _kv_prep(proj, b, t)
    seq = lambda w: pl.BlockSpec((None, None, t, w), lambda i, n, qb, sl: (i, n, 0, 0))
    grid_spec = pltpu.PrefetchScalarGridSpec(
        num_scalar_prefetch=1,
        grid=(b, NSA_KV_HEADS, nqb),
        in_specs=[pl.BlockSpec((TQ, Q_SLAB), lambda i, n, qb, sl: (i * nqb + qb, COL_Q // Q_SLAB + n)),
                  pl.BlockSpec((None, None, n16, KV_SLAB), lambda i, n, qb, sl: (i, n, 0, 0)),
                  seq(2 * LANES), seq(LANES), seq(LANES), seq(LANES),
                  pl.BlockSpec((TQ, LANES), lambda i, n, qb, sl: (i * nqb + qb, COL_MISC // LANES)),
                  pl.BlockSpec((NSA_HEADS, LANES), lambda i, n, qb, sl: (0, 0)),
                  pl.BlockSpec((n_sel, n16), lambda i, n, qb, sl: (0, 0))],
        out_specs=pl.BlockSpec((TQ, Q_SLAB), lambda i, n, qb, sl: (i * nqb + qb, n)),
    )
    overlap_t = jnp.asarray(_overlap_counts(np.arange(n16)[None, :], np.arange(n_sel)[:, None]))
    return pl.pallas_call(
        functools.partial(_nsa_prompt_kernel, n16=n16, n_sel=n_sel),
        grid_spec=grid_spec,
        out_shape=jax.ShapeDtypeStruct((b * t, NSA_WIDTH), F32),
        compiler_params=_cparams(("parallel", "parallel", "arbitrary")),
        name="nsa_prompt",
    )(slopes, proj, kcv, k_sel, v_sel, k_win, v_win, proj, _slope_features(), overlap_t)


def _gla_kernel(q_ref, k_ref, v_ref, r_ref, misc_ref, wa_ref, ba_ref, gout_ref, s0_ref, o_ref, sfin_ref, state_ref,
                *, chunk, sub, valid_rows):
    c = pl.program_id(1)

    @pl.when(c == 0)
    def _():
        state_ref[...] = s0_ref[...]

    ga = misc_ref[:, MISC_GA:MISC_GA + GLA_RANK]
    x = jnp.dot(ga, wa_ref[...], preferred_element_type=F32, precision=HIGHEST) + ba_ref[...]
    lg = (jnp.minimum(x, 0.0) - jnp.log1p(jnp.exp(-jnp.abs(x)))) / GLA_TAU
    row = lax.broadcasted_iota(jnp.int32, (chunk, 1), 0)
    if valid_rows < chunk:
        lg = jnp.where(row < valid_rows, lg, 0.0)
    tri = (lax.broadcasted_iota(jnp.int32, (chunk, chunk), 0)
           >= lax.broadcasted_iota(jnp.int32, (chunk, chunk), 1)).astype(F32)
    cum_all = jnp.dot(tri, lg, preferred_element_type=F32, precision=HIGHEST)
    for h in range(GLA_HEADS):
        kcols = slice(h * GLA_DK, (h + 1) * GLA_DK)
        vcols = slice(h * GLA_DV, (h + 1) * GLA_DV)
        o, new_state = _gla_head_chunk(q_ref[:, kcols] * (GLA_DK ** -0.5), k_ref[:, kcols], v_ref[:, vcols],
                                       cum_all[:, kcols], state_ref[h], chunk, sub)
        state_ref[h] = new_state
        rg = r_ref[:, vcols]
        o_ref[:, vcols] = _rms(o, gout_ref[...]) * (rg * _sigmoid(rg))

    @pl.when(c == pl.num_programs(1) - 1)
    def _():
        sfin_ref[...] = state_ref[...]


def _gla_head_chunk(q, k, v, cum, state, chunk, sub):
    vb = v.astype(BF16)
    inter = _dot((q * jnp.exp(cum)).astype(BF16), state.astype(BF16))

    outs = []
    for i in range(chunk // sub):
        r0 = i * sub
        qi, ki, ci, vi = q[r0:r0 + sub], k[r0:r0 + sub], cum[r0:r0 + sub], v[r0:r0 + sub]
        o_i = inter[r0:r0 + sub]
        if i > 0:
            anchor = cum[r0:r0 + 1]
            qd = (qi * jnp.exp(ci - anchor)).astype(BF16)
            kd = (k[0:r0] * jnp.exp(anchor - cum[0:r0])).astype(BF16)
            o_i = o_i + _dot(_dot_nt(qd, kd).astype(BF16), vb[0:r0])
        trow = lax.broadcasted_iota(jnp.int32, (sub, 1), 0)
        for s_ in range(sub):
            w = jnp.sum(qi * (ki[s_:s_ + 1] * jnp.exp(jnp.minimum(ci - ci[s_:s_ + 1], 0.0))), axis=-1, keepdims=True)
            o_i = o_i + jnp.where(trow >= s_, w, 0.0) * vi[s_:s_ + 1]
        outs.append(o_i)
    o = jnp.concatenate(outs, axis=0) if len(outs) > 1 else outs[0]

    last = cum[chunk - 1:chunk]
    kdec = (k * jnp.exp(last - cum)).astype(BF16)
    decay_col = jnp.transpose(jnp.broadcast_to(jnp.exp(last), (SUBLANES, GLA_DK)))[:, 0:1]
    return o, decay_col * state + _dot_tn(kdec, vb)


def _gla(proj, w_a2, b_a, g_out, s0, b, t, chunk, sub, valid_rows):
    nck = t // chunk
    kw = GLA_HEADS * GLA_DK
    state_spec = pl.BlockSpec((None, GLA_HEADS, GLA_DK, GLA_DV), lambda i, c: (i, 0, 0, 0))
    return pl.pallas_call(
        functools.partial(_gla_kernel, chunk=chunk, sub=sub, valid_rows=valid_rows),
        grid=(b, nck),
        in_specs=[pl.BlockSpec((chunk, kw), lambda i, c: (i * nck + c, COL_GQ // kw)),
                  pl.BlockSpec((chunk, kw), lambda i, c: (i * nck + c, COL_GK // kw)),
                  pl.BlockSpec((chunk, GLA_WIDTH), lambda i, c: (i * nck + c, COL_GV // GLA_WIDTH)),
                  pl.BlockSpec((chunk, GLA_WIDTH), lambda i, c: (i * nck + c, COL_GR // GLA_WIDTH)),
                  pl.BlockSpec((chunk, LANES), lambda i, c: (i * nck + c, COL_MISC // LANES)),
                  pl.BlockSpec((GLA_RANK, kw), lambda i, c: (0, 0)),
                  pl.BlockSpec((1, kw), lambda i, c: (0, 0)),
                  pl.BlockSpec((1, GLA_DV), lambda i, c: (0, 0)),
                  state_spec],
        out_specs=[pl.BlockSpec((chunk, GLA_WIDTH), lambda i, c: (i * nck + c, 0)), state_spec],
        out_shape=[jax.ShapeDtypeStruct((b * t, GLA_WIDTH), F32),
                   jax.ShapeDtypeStruct((b, GLA_HEADS, GLA_DK, GLA_DV), F32)],
        scratch_shapes=[pltpu.VMEM((GLA_HEADS, GLA_DK, GLA_DV), F32)],
        compiler_params=_cparams(("parallel", "arbitrary")),
        name="gla",
    )(proj, proj, proj, proj, proj, w_a2, b_a, g_out, s0)


def _out_proj_kernel(on_ref, og_ref, x_ref, g_ref, w_ref, o_ref, a_ref):
    @pl.when(pl.program_id(1) == 0)
    def _():
        a_ref[:, :NSA_WIDTH] = _rms(on_ref[...], g_ref[...]).astype(BF16)
        a_ref[:, NSA_WIDTH:] = og_ref[...].astype(BF16)

    o_ref[...] = x_ref[...] + _dot(a_ref[...], w_ref[...])


def _out_proj(o_nsa, o_gla, x, g_nsa, w_out, tm, tn):
    m = x.shape[0]
    return pl.pallas_call(
        _out_proj_kernel,
        grid=(m // tm, D_MODEL // tn),
        in_specs=[pl.BlockSpec((tm, NSA_WIDTH), lambda i, j: (i, 0)),
                  pl.BlockSpec((tm, GLA_WIDTH), lambda i, j: (i, 0)),
                  pl.BlockSpec((tm, tn), lambda i, j: (i, j)),
                  pl.BlockSpec((1, NSA_WIDTH), lambda i, j: (0, 0)),
                  pl.BlockSpec((NSA_WIDTH + GLA_WIDTH, tn), lambda i, j: (0, j))],
        out_specs=pl.BlockSpec((tm, tn), lambda i, j: (i, j)),
        out_shape=jax.ShapeDtypeStruct((m, D_MODEL), F32),
        scratch_shapes=[pltpu.VMEM((tm, NSA_WIDTH + GLA_WIDTH), BF16)],
        compiler_params=_cparams(("parallel", "arbitrary")),
        name="out_proj",
    )(o_nsa, o_gla, x, g_nsa, w_out)


def _ffn_kernel(h_ref, g_ref, wa_ref, wg_ref, cw_ref, cb_ref, wd_ref, p1_ref, p2_ref, o_ref, tail_ref,
                n2_ref, acc_ref, carry_ref, *, tm, tf, seq_rows):
    i = pl.program_id(0)
    j = pl.program_id(1)

    @pl.when(j == 0)
    def _():
        n2_ref[...] = _rms(h_ref[...], g_ref[...]).astype(BF16)
        acc_ref[...] = jnp.zeros_like(acc_ref)

    n2 = n2_ref[...]
    a = _dot(n2, wa_ref[...])
    gate = _dot(n2, wg_ref[...])
    row = lax.broadcasted_iota(jnp.int32, (tm, 1), 0)
    r1 = pltpu.roll(a, 1, axis=0)
    r2 = pltpu.roll(a, 2, axis=0)
    if seq_rows >= tm:
        cols = pl.ds(pl.multiple_of(j * tf, tf), tf)
        first = (i % (seq_rows // tm)) == 0
        prev = jnp.where(first, p2_ref[...], carry_ref[:, cols])
        a1 = jnp.where(row == 0, prev[1:2], r1)
        a2 = jnp.where(row == 0, prev[0:1], jnp.where(row == 1, prev[1:2], r2))
        carry_ref[:, cols] = a[tm - 2:tm]
    else:
        t = row % seq_rows
        a1 = jnp.where(t == 0, p1_ref[...], r1)
        a2 = jnp.where(t < 2, p2_ref[...], r2)
    cw = cw_ref[...]
    conv = cb_ref[...] + a2 * cw[0:1] + a1 * cw[1:2] + a * cw[2:3]
    y = (_gelu_tanh(conv) * gate).astype(BF16)
    acc_ref[...] += _dot(y, wd_ref[...])
    tail_ref[...] = a[tm - tail_ref.shape[0]:tm]

    @pl.when(j == pl.num_programs(1) - 1)
    def _():
        o_ref[...] = h_ref[...] + acc_ref[...]


def _ffn(h, g_ffn, w_up, conv_w, conv_b, w_down, p1, p2, tm, tf, seq_rows):
    m = h.shape[0]
    nj = D_FF // tf
    if seq_rows >= tm:
        tiles_per_seq = seq_rows // tm
        p1_spec = pl.BlockSpec((None, CONV_W - 1, tf), lambda i, j: (i // tiles_per_seq, 0, j))
        p2_spec = pl.BlockSpec((None, CONV_W - 1, tf), lambda i, j: (i // tiles_per_seq, 0, j))
    else:
        p1_spec = pl.BlockSpec((tm, tf), lambda i, j: (i, j))
        p2_spec = pl.BlockSpec((tm, tf), lambda i, j: (i, j))
    if seq_rows >= tm:
        tail_spec = pl.BlockSpec((None, SUBLANES, tf), lambda i, j: (i, 0, j))
        tail_shape = jax.ShapeDtypeStruct((m // tm, SUBLANES, D_FF), F32)
    else:
        tail_spec = pl.BlockSpec((tm, tf), lambda i, j: (i, j))
        tail_shape = jax.ShapeDtypeStruct((m, D_FF), F32)
    return pl.pallas_call(
        functools.partial(_ffn_kernel, tm=tm, tf=tf, seq_rows=seq_rows),
        grid=(m // tm, nj),
        in_specs=[pl.BlockSpec((tm, D_MODEL), lambda i, j: (i, 0)),
                  pl.BlockSpec((1, D_MODEL), lambda i, j: (0, 0)),
                  pl.BlockSpec((D_MODEL, tf), lambda i, j: (0, j)),
                  pl.BlockSpec((D_MODEL, tf), lambda i, j: (0, nj + j)),
                  pl.BlockSpec((CONV_W, tf), lambda i, j: (0, j)),
                  pl.BlockSpec((1, tf), lambda i, j: (0, j)),
                  pl.BlockSpec((tf, D_MODEL), lambda i, j: (j, 0)),
                  p1_spec, p2_spec],
        out_specs=[pl.BlockSpec((tm, D_MODEL), lambda i, j: (i, 0)),
                   tail_spec],
        out_shape=[jax.ShapeDtypeStruct((m, D_MODEL), F32), tail_shape],
        scratch_shapes=[pltpu.VMEM((tm, D_MODEL), BF16), pltpu.VMEM((tm, D_MODEL), F32),
                        pltpu.VMEM((CONV_W - 1, D_FF), F32)],
        compiler_params=_cparams(("arbitrary", "arbitrary")),
        name="conv_ffn",
    )(h, g_ffn, w_up, w_up, conv_w, conv_b, w_down, p1, p2)


def _ple_kernel(h_ref, p_ref, wg_ref, wp_ref, gp_ref, gf_ref, o_ref, *, final_norm):
    h = h_ref[...]
    gate = _sigmoid(_dot(h.astype(BF16), wg_ref[...]))
    pe = _rms(_dot(p_ref[...].astype(BF16), wp_ref[...]), gp_ref[...])
    h = h + gate * pe
    o_ref[...] = _rms(h, gf_ref[...]) if final_norm else h


def _ple(h, p, w_gate, w_proj, g_ple, g_final, tm, final_norm):
    m = h.shape[0]
    return pl.pallas_call(
        functools.partial(_ple_kernel, final_norm=final_norm),
        grid=(m // tm,),
        in_specs=[pl.BlockSpec((tm, D_MODEL), lambda i: (i, 0)),
                  pl.BlockSpec((tm, PLE_DIM), lambda i: (i, 0)),
                  pl.BlockSpec((D_MODEL, D_MODEL), lambda i: (0, 0)),
                  pl.BlockSpec((PLE_DIM, D_MODEL), lambda i: (0, 0)),
                  pl.BlockSpec((1, D_MODEL), lambda i: (0, 0)),
                  pl.BlockSpec((1, D_MODEL), lambda i: (0, 0))],
        out_specs=pl.BlockSpec((tm, D_MODEL), lambda i: (i, 0)),
        out_shape=jax.ShapeDtypeStruct((m, D_MODEL), F32),
        compiler_params=_cparams(("parallel",)),
        name="ple_norm",
    )(h, p, w_gate, w_proj, g_ple, g_final)


CMP_PAGES = 16
CMP_BATCH = 4
CHUNKS_PER_PAGE = PAGE_SIZE // CMP_STRIDE


def _cmp_sample_kernel(pt_ref, *refs):
    n_in = CMP_BATCH * CMP_PAGES
    perm_ref, w1_ref, w1p_ref, w2_ref, pe_ref, o_ref, carry_ref, x_ref, pe_hid_ref = refs[n_in:]

    @pl.when((pl.program_id(0) == 0) & (pl.program_id(1) == 0))
    def _():
        pe_hid_ref[...] = jnp.broadcast_to(_cmp_pe_hidden(pe_ref, w1_ref), pe_hid_ref.shape)

    @pl.when(pl.program_id(1) == 0)
    def _():
        carry_ref[...] = jnp.zeros_like(carry_ref)

    for e in range(CMP_BATCH):
        _cmp_sample_group(refs[e * CMP_PAGES:(e + 1) * CMP_PAGES], perm_ref, w1p_ref, w2_ref, pe_hid_ref[0:1, :],
                          o_ref.at[e], carry_ref.at[e], x_ref.at[e])


def _cmp_sample_group(page_refs, perm_ref, w1p_ref, w2_ref, pe_hid, o_ref, carry_ref, x_ref):
    nck = CMP_PAGES * CHUNKS_PER_PAGE
    perm = perm_ref[...]
    for k, r in enumerate(page_refs):
        xp = _dot_nt(perm, r[...].astype(BF16))
        for s in range(CMP_STRIDE):
            for n in range(NSA_KV_HEADS):
                x_ref[s, n, k * CHUNKS_PER_PAGE:(k + 1) * CHUNKS_PER_PAGE, :] = (
                    xp[s * CHUNKS_PER_PAGE:(s + 1) * CHUNKS_PER_PAGE, n * KV_SLAB:(n + 1) * KV_SLAB])
    acc = jnp.zeros((NSA_KV_HEADS * nck, 4 * CMP_HIDDEN), F32)
    for sp in range(CMP_STRIDE // 2):
        xs = jnp.concatenate([x_ref[2 * sp].reshape(NSA_KV_HEADS * nck, KV_SLAB),
                              x_ref[2 * sp + 1].reshape(NSA_KV_HEADS * nck, KV_SLAB)], axis=1)
        acc = acc + _dot(xs.astype(BF16), w1p_ref[sp])
    a = acc[:, :2 * CMP_HIDDEN]
    b = acc[:, 2 * CMP_HIDDEN:]
    a_prev = pltpu.roll(a, 1, axis=0)
    row = lax.broadcasted_iota(jnp.int32, (NSA_KV_HEADS * nck, 1), 0)
    for n in range(NSA_KV_HEADS):
        a_prev = jnp.where(row == n * nck, carry_ref[n:n + 1, :], a_prev)
        carry_ref[n:n + 1, :] = a[(n + 1) * nck - 1:(n + 1) * nck]
    hid = a_prev + b + pe_hid
    res = _dot(_gelu_tanh(hid).astype(BF16), w2_ref[...])
    for n in range(NSA_KV_HEADS):
        o_ref[n] = res[n * nck:(n + 1) * nck]


def _cmp_sample(cache_c, page_table_flat, w1, w2, pe, b, n_pages):
    n_grp = n_pages // CMP_PAGES
    nck = CMP_PAGES * CHUNKS_PER_PAGE
    assert b % CMP_BATCH == 0

    def page_spec(e, k):
        return pl.BlockSpec((None, KV_WIDTH, PAGE_SIZE),
                            lambda i, g, pt: (pt[(i * CMP_BATCH + e) * n_pages + g * CMP_PAGES + k], 0, 0))

    r = np.arange(PAGE_SIZE)
    perm_np = np.zeros((PAGE_SIZE, PAGE_SIZE), np.float32)
    perm_np[(r % CMP_STRIDE) * CHUNKS_PER_PAGE + r // CMP_STRIDE, r] = 1.0
    perm = jnp.asarray(perm_np, dtype=BF16)
    w1p = w1.reshape(CMP_STRIDE // 2, 2 * KV_SLAB, 4 * CMP_HIDDEN)

    grid_spec = pltpu.PrefetchScalarGridSpec(
        num_scalar_prefetch=1,
        grid=(b // CMP_BATCH, n_grp),
        in_specs=[page_spec(e, k) for e in range(CMP_BATCH) for k in range(CMP_PAGES)]
        + [pl.BlockSpec(perm.shape, lambda i, g, pt: (0, 0)),
           pl.BlockSpec(w1.shape, lambda i, g, pt: (0, 0, 0)),
           pl.BlockSpec(w1p.shape, lambda i, g, pt: (0, 0, 0)),
           pl.BlockSpec(w2.shape, lambda i, g, pt: (0, 0)),
           pl.BlockSpec(pe.shape, lambda i, g, pt: (0, 0))],
        out_specs=pl.BlockSpec((CMP_BATCH, NSA_KV_HEADS, nck, KV_SLAB), lambda i, g, pt: (i, 0, g, 0)),
        scratch_shapes=[pltpu.VMEM((CMP_BATCH, SUBLANES, 2 * CMP_HIDDEN), F32),
                        pltpu.VMEM((CMP_BATCH, CMP_STRIDE, NSA_KV_HEADS, nck, KV_SLAB), F32),
                        pltpu.VMEM((SUBLANES, 2 * CMP_HIDDEN), F32)],
    )
    return pl.pallas_call(
        _cmp_sample_kernel,
        grid_spec=grid_spec,
        out_shape=jax.ShapeDtypeStruct((b, NSA_KV_HEADS, n_pages * CHUNKS_PER_PAGE, KV_SLAB), F32),
        compiler_params=_cparams(("arbitrary", "arbitrary")),
        name="cmp_sample",
    )(page_table_flat, *([cache_c] * (CMP_BATCH * CMP_PAGES)), perm, w1, w1p, w2, pe)


def _sel_sample_kernel(slopes_ref, q_ref, kcv_ref, ov_ref, oc_ref, idx_ref, *, past_len, n_sel_pad, t_valid):
    tp = SUBLANES
    p_grps = []
    for n in range(NSA_KV_HEADS):
        oc, p_grp = _sel_sample_attend(n, slopes_ref, q_ref[:, n * Q_SLAB:(n + 1) * Q_SLAB] * SCALE, kcv_ref[n],
                                       past_len)
        oc_ref[n] = oc
        p_grps.append(p_grp)
    imp_all = jnp.dot(jnp.concatenate(p_grps, axis=0), ov_ref[...], preferred_element_type=F32, precision=HIGHEST)
    for n in range(NSA_KV_HEADS):
        idx_ref[n] = _sel_sample_pick(imp_all[n * tp:(n + 1) * tp], past_len, n_sel_pad, t_valid)


def _sel_sample_attend(n, slopes_ref, qf, kcv, past_len):
    tp = SUBLANES
    rows = NSA_GROUP * tp
    n_rows_c = past_len // CMP_STRIDE
    q = jnp.concatenate([qf[:, g * HEAD_DIM:(g + 1) * HEAD_DIM] for g in range(NSA_GROUP)], axis=0).astype(BF16)
    tok = lax.broadcasted_iota(jnp.int32, (rows, 1), 0) % tp
    qpos = past_len + tok
    slope = _slope_col(slopes_ref, n, tp, rows)
    kc = kcv[:, :HEAD_DIM].astype(BF16)
    vc = kcv[:, HEAD_DIM:].astype(BF16)
    s = _dot_nt(q, kc)
    cp = lax.broadcasted_iota(jnp.int32, (1, n_rows_c), 1)
    end = (cp - 1) * CMP_STRIDE + (CMP_BLOCK - 1)
    dist = (qpos - end).astype(F32)
    valid = (cp >= 1) & (dist >= 0)
    s = jnp.where(valid, s - slope * dist, NEG_INF)
    e = jnp.exp(s - jnp.max(s, axis=-1, keepdims=True))
    p = jnp.where(valid, e / jnp.sum(e, axis=-1, keepdims=True), 0.0)
    o_c = _dot(p.astype(BF16), vc)
    oc = jnp.concatenate([o_c[g * tp:(g + 1) * tp] for g in range(NSA_GROUP)], axis=1)
    p_grp = p[0:tp]
    for g in range(1, NSA_GROUP):
        p_grp = p_grp + p[g * tp:(g + 1) * tp]
    return oc, p_grp


def _sel_sample_pick(imp, past_len, n_sel_pad, t_valid):
    tp = SUBLANES
    nb_past = past_len // SEL_BLOCK
    n_tail = -(-t_valid // SEL_BLOCK)
    n_sel = nb_past + n_tail
    j = lax.broadcasted_iota(jnp.int32, (tp, n_sel_pad), 1)
    cur = (past_len + lax.broadcasted_iota(jnp.int32, (tp, 1), 0)) // SEL_BLOCK
    forced = (j == 0) | (j == cur) | (j == cur - 1)
    imp = jnp.where(forced, FORCE_SCORE, jnp.where(j > cur, -FORCE_SCORE, imp))
    imp = jnp.where(j < n_sel, imp, REMOVED_SCORE)
    _, picks = _top_blocks(imp, min(N_SELECT, n_sel))
    kcol = lax.broadcasted_iota(jnp.int32, (tp, N_SELECT), 1)
    idx = jnp.zeros((tp, N_SELECT), jnp.int32)
    for kk, pk in enumerate(picks):
        idx = jnp.where(kcol == kk, pk, idx)
    return idx


def _sel_sample(proj8, kcv, slopes, b, past_len, t_valid):
    n_rows_c = past_len // CMP_STRIDE
    nb_past = past_len // SEL_BLOCK
    n_sel_pad = -(-(nb_past + 1) // LANES) * LANES
    overlap = jnp.asarray(_overlap_counts(np.arange(n_rows_c)[:, None] - 1, np.arange(n_sel_pad)[None, :]))
    grid_spec = pltpu.PrefetchScalarGridSpec(
        num_scalar_prefetch=1,
        grid=(b,),
        in_specs=[pl.BlockSpec((SUBLANES, NSA_WIDTH), lambda i, sl: (i, COL_Q // NSA_WIDTH)),
                  pl.BlockSpec((None, NSA_KV_HEADS, n_rows_c, KV_SLAB), lambda i, sl: (i, 0, 0, 0)),
                  pl.BlockSpec(overlap.shape, lambda i, sl: (0, 0))],
        out_specs=[pl.BlockSpec((None, NSA_KV_HEADS, SUBLANES, Q_SLAB), lambda i, sl: (i, 0, 0, 0)),
                   pl.BlockSpec((None, NSA_KV_HEADS, SUBLANES, N_SELECT), lambda i, sl: (i, 0, 0, 0))],
    )
    return pl.pallas_call(
        functools.partial(_sel_sample_kernel, past_len=past_len, n_sel_pad=n_sel_pad, t_valid=t_valid),
        grid_spec=grid_spec,
        out_shape=[jax.ShapeDtypeStruct((b, NSA_KV_HEADS, SUBLANES, Q_SLAB), F32),
                   jax.ShapeDtypeStruct((b, NSA_KV_HEADS, SUBLANES, N_SELECT), jnp.int32)],
        compiler_params=_cparams(("parallel",)),
        name="sel_sample",
    )(slopes, proj8, kcv, overlap)


def _nsa_sample_kernel(idx_ref, pt_ref, slopes_ref, cache_ref, q_ref, tail_ref, wnew_ref, wcache_ref, oc_ref,
                       misc_ref, o_ref, pages_a, pages_b, sem_ref, *, past_len, t_valid, n_pages):
    n_blk = t_valid * N_SELECT
    nb_past = past_len // SEL_BLOCK
    per_page = PAGE_SIZE // SEL_BLOCK
    pairs = NSA_KV_HEADS // 2
    step = pl.program_id(0) * pairs + pl.program_id(1)
    n_steps = pl.num_programs(0) * pairs
    head_a = step * 2
    head_b = head_a + 1
    next_a = ((step + 1) % n_steps) * 2

    def page_copy(bh, j, buf, sem_i):
        t, kk = j // N_SELECT, j % N_SELECT
        jb = jnp.minimum(idx_ref[(bh * SUBLANES + t) * N_SELECT + kk], nb_past - 1)
        page = pt_ref[(bh // NSA_KV_HEADS) * n_pages + jb // per_page]
        return pltpu.make_async_copy(cache_ref.at[page, bh % NSA_KV_HEADS], buf.at[j], sem_ref.at[sem_i])

    def wait_all(bh, buf, sem_i):
        def body(j, c):
            page_copy(bh, j, buf, sem_i).wait()
            return c
        lax.fori_loop(0, n_blk, body, 0)

    def compute(bh, h2, buf):
        n = bh % NSA_KV_HEADS
        slope = _slope_col(slopes_ref, n, 1, SUBLANES)
        gates = _pick_head_gates(misc_ref[:, 0:GATE_W], n)
        tl = tail_ref[:, h2 * KV_SLAB:(h2 + 1) * KV_SLAB]
        wn = wnew_ref[:, h2 * KV_SLAB:(h2 + 1) * KV_SLAB]
        win_k = wcache_ref[h2, 0:HEAD_DIM, :].astype(BF16)
        win_v = wcache_ref[h2, HEAD_DIM:KV_SLAB, :].astype(BF16)
        rows_out = []
        for t in range(t_valid):
            picked = [buf.at[t * N_SELECT + kk] for kk in range(N_SELECT)]
            jbs = [idx_ref[(bh * SUBLANES + t) * N_SELECT + kk] for kk in range(N_SELECT)]
            rows_out.append(_nsa_sample_token(t, q_ref[t:t + 1, h2 * Q_SLAB:(h2 + 1) * Q_SLAB] * SCALE, slope, picked,
                                              jbs, tl, wn, win_k, win_v, oc_ref[h2, t:t + 1, :], gates[t:t + 1, :],
                                              past_len))
        rows_out.append(jnp.zeros((SUBLANES - t_valid, Q_SLAB), F32))
        o_ref[:, h2 * Q_SLAB:(h2 + 1) * Q_SLAB] = jnp.concatenate(rows_out, axis=0)

    @pl.when(step == 0)
    def _():
        for j in range(n_blk):
            page_copy(head_a, j, pages_a, 0).start()

    for j in range(n_blk):
        page_copy(head_b, j, pages_b, 1).start()
    wait_all(head_a, pages_a, 0)
    compute(head_a, 0, pages_a)
    for j in range(n_blk):
        page_copy(next_a, j, pages_a, 0).start()
    wait_all(head_b, pages_b, 1)
    compute(head_b, 1, pages_b)

    @pl.when(step == n_steps - 1)
    def _():
        wait_all(next_a, pages_a, 0)


def _nsa_sample_token(t, qrow, slope, picked, jbs, tl, wn, win_k, win_v, oc_row, gates, past_len):
    nb_past = past_len // SEL_BLOCK
    g8 = SUBLANES
    q = jnp.concatenate([qrow[:, g * HEAD_DIM:(g + 1) * HEAD_DIM] for g in range(NSA_GROUP)]
                        + [jnp.zeros((g8 - NSA_GROUP, HEAD_DIM), F32)], axis=0).astype(BF16)
    qpos = past_len + t

    per_page = PAGE_SIZE // SEL_BLOCK
    k_all = jnp.concatenate([r[0:HEAD_DIM, :] for r in picked], axis=1).astype(BF16)
    v_all = jnp.concatenate([r[HEAD_DIM:KV_SLAB, :] for r in picked], axis=1).astype(BF16)
    jb_row = jnp.concatenate([jnp.full((1, PAGE_SIZE), jb, jnp.int32) for jb in jbs], axis=1)
    tail_count = jnp.zeros((), jnp.int32)
    for jb in jbs:
        tail_count = tail_count + (jb >= nb_past).astype(jnp.int32)
    lane = lax.broadcasted_iota(jnp.int32, (1, N_SELECT * PAGE_SIZE), 1) % PAGE_SIZE
    kpos = (jb_row // per_page) * PAGE_SIZE + lane
    dist = (qpos - kpos).astype(F32)
    ok = (dist >= 0) & (kpos // SEL_BLOCK == jb_row) & (jb_row < nb_past)
    sc = _dot(q, k_all) - slope * dist
    parts = [(jnp.where(ok, sc, NEG_INF), ok, v_all, True)]
    lane8 = lax.broadcasted_iota(jnp.int32, (1, SUBLANES), 1)
    dist = (t - lane8).astype(F32)
    ok = (dist >= 0) & (jnp.full((1, SUBLANES), tail_count, jnp.int32) > 0)
    sc = _dot_nt(q, tl[:, :HEAD_DIM].astype(BF16)) - slope * dist
    parts.append((jnp.where(ok, sc, NEG_INF), ok, tl[:, HEAD_DIM:].astype(BF16), False))

    def softmax_av(parts):
        m = parts[0][0].max(axis=-1, keepdims=True)
        for sc, _, _, _ in parts[1:]:
            m = jnp.maximum(m, sc.max(axis=-1, keepdims=True))
        l = jnp.zeros((g8, 1), F32)
        acc = jnp.zeros((g8, HEAD_DIM), F32)
        for sc, ok, v, v_transposed in parts:
            pr = jnp.where(ok, jnp.exp(sc - m), 0.0)
            l = l + pr.sum(axis=-1, keepdims=True)
            acc = acc + (_dot_nt(pr.astype(BF16), v) if v_transposed else _dot(pr.astype(BF16), v))
        return acc / l

    o_s = softmax_av(parts)

    buf_len = win_k.shape[1]
    lane_w = lax.broadcasted_iota(jnp.int32, (1, buf_len), 1)
    dist_c = (qpos - (past_len - buf_len + lane_w)).astype(F32)
    ok_c = (dist_c >= 0) & (dist_c < WINDOW)
    sc_c = jnp.where(ok_c, _dot(q, win_k) - slope * dist_c, NEG_INF)
    dist_n = (t - lane8).astype(F32)
    ok_n = (dist_n >= 0) & (dist_n < WINDOW)
    sc_n = jnp.where(ok_n, _dot_nt(q, wn[:, :HEAD_DIM].astype(BF16)) - slope * dist_n, NEG_INF)
    o_w = softmax_av([(sc_c, ok_c, win_v, True), (sc_n, ok_n, wn[:, HEAD_DIM:].astype(BF16), False)])

    outs = []
    for g in range(NSA_GROUP):
        outs.append(gates[:, 3 * g:3 * g + 1] * oc_row[:, g * HEAD_DIM:(g + 1) * HEAD_DIM]
                    + gates[:, 3 * g + 1:3 * g + 2] * o_s[g:g + 1]
                    + gates[:, 3 * g + 2:3 * g + 3] * o_w[g:g + 1])
    return jnp.concatenate(outs, axis=1)


def _nsa_sample(proj8, cache_s, cache_w, o_c, idx_flat, page_table_flat, slopes, b, t_valid, past_len):
    n_pages = past_len // PAGE_SIZE
    buf_len = cache_w.shape[-1]
    pairs = NSA_KV_HEADS // 2
    n_blk = t_valid * N_SELECT
    pages = pltpu.VMEM((n_blk, KV_SLAB, PAGE_SIZE), F32)
    grid_spec = pltpu.PrefetchScalarGridSpec(
        num_scalar_prefetch=3,
        grid=(b, pairs),
        in_specs=[pl.BlockSpec(memory_space=pl.ANY),
                  pl.BlockSpec((SUBLANES, 2 * Q_SLAB), lambda i, p, *_: (i, COL_Q // (2 * Q_SLAB) + p)),
                  pl.BlockSpec((SUBLANES, 2 * KV_SLAB), lambda i, p, *_: (i, COL_KVS // (2 * KV_SLAB) + p)),
                  pl.BlockSpec((SUBLANES, 2 * KV_SLAB), lambda i, p, *_: (i, COL_KVW // (2 * KV_SLAB) + p)),
                  pl.BlockSpec((None, 2, KV_SLAB, buf_len), lambda i, p, *_: (i, p, 0, 0)),
                  pl.BlockSpec((None, 2, SUBLANES, Q_SLAB), lambda i, p, *_: (i, p, 0, 0)),
                  pl.BlockSpec((SUBLANES, LANES), lambda i, p, *_: (i, COL_MISC // LANES))],
        out_specs=pl.BlockSpec((SUBLANES, 2 * Q_SLAB), lambda i, p, *_: (i, p)),
        scratch_shapes=[pages, pages, pltpu.SemaphoreType.DMA((2,))],
    )
    return pl.pallas_call(
        functools.partial(_nsa_sample_kernel, past_len=past_len, t_valid=t_valid, n_pages=n_pages),
        grid_spec=grid_spec,
        out_shape=jax.ShapeDtypeStruct((b * SUBLANES, NSA_WIDTH), F32),
        compiler_params=_cparams(("arbitrary", "arbitrary")),
        name="nsa_sample",
    )(idx_flat, page_table_flat, slopes, cache_s, proj8, proj8, proj8, cache_w, o_c, proj8)


def _alibi_slopes():
    h = np.arange(1, NSA_HEADS + 1, dtype=np.float32)
    return jnp.asarray(2.0 ** (-8.0 * h / NSA_HEADS), dtype=F32)


def _permute_w_in(w_in):
    offs = np.cumsum([0, NSA_WIDTH, KV_WIDTH, KV_WIDTH, KV_WIDTH, GATE_W, GLA_HEADS * GLA_DK, GLA_HEADS * GLA_DK,
                      GLA_WIDTH, GLA_WIDTH, GLA_RANK])
    w_bf = w_in.astype(BF16)
    piece = [w_bf[:, offs[k]:offs[k + 1]] for k in range(10)]
    q, kvc, kvs, kvw, gl, gq, gk, gv, gr, ga = piece
    pad = jnp.zeros((w_in.shape[0], LANES - GATE_W - GLA_RANK), BF16)
    return jnp.concatenate([q, gv, gr, kvc, kvs, kvw, gq, gk, gl, ga, pad], axis=1)


def _row_tile(m, pref):
    return pref if m % pref == 0 else m


def _dense_tail(h_in, o_nsa, o_gla, p_emb, wts, conv_p1, conv_p2, seq_rows, last_layer, tm):
    (g_nsa, w_out, g_ffn, w_up, conv_w, conv_b, w_down, w_ple_proj, g_ple, w_ple_gate, g_final) = wts
    h = _out_proj(o_nsa, o_gla, h_in, g_nsa, w_out, tm, D_MODEL // 2)
    h, tails = _ffn(h, g_ffn, w_up, conv_w, conv_b, w_down, conv_p1, conv_p2, tm, FFN_COLS, seq_rows)
    y = _ple(h, p_emb, w_ple_gate, w_ple_proj, g_ple, g_final, tm, last_layer)
    return y, tails


def kernel(x_prompt, x_sample, p_prompt, p_sample, cache_cmp_kv, cache_sel_kv, cache_win_kv, state_gla, state_ffn_conv, page_table, g_attn, w_in, w_cmp_k1, w_cmp_k2, pe_cmp_k, w_cmp_v1, w_cmp_v2, pe_cmp_v, w_gla_a2, b_gla_a, g_nsa_out, g_gla_out, w_out, g_ffn, w_up, conv_w, conv_b, w_down, w_ple_proj, g_ple, w_ple_gate, g_final):
    depth = w_in.shape[0]
    bp, tp, _ = x_prompt.shape
    bs, ts, _ = x_sample.shape
    n_pages = page_table.shape[1]
    past_len = n_pages * PAGE_SIZE
    n_pool = cache_cmp_kv.shape[1]
    assert tp % TQ == 0 and tp >= WINDOW and ts <= SUBLANES and ts <= SEL_BLOCK and ts >= CONV_W - 1
    assert n_pages % CMP_PAGES == 0 and tp // SEL_BLOCK >= N_SELECT
    slopes = _alibi_slopes()
    pt_flat = page_table.reshape(-1).astype(jnp.int32)
    kv_shape = (NSA_KV_HEADS, 2, HEAD_DIM)

    hp = x_prompt.reshape(bp * tp, D_MODEL)
    hs = x_sample.reshape(bs * ts, D_MODEL)
    new_p = [[] for _ in range(5)]
    new_s = [[] for _ in range(5)]
    tm_p = _row_tile(bp * tp, DENSE_ROWS)
    tm_s = bs * ts
    for i in range(depth):
        last = i == depth - 1
        w_in_p = _permute_w_in(w_in[i])
        g_a = g_attn[i].reshape(1, D_MODEL)
        w1c, w2c, pec = _cmp_weights(w_cmp_k1[i], w_cmp_k2[i], pe_cmp_k[i], w_cmp_v1[i], w_cmp_v2[i], pe_cmp_v[i])
        b_a = b_gla_a[i].reshape(1, -1)
        g_go = g_gla_out[i].reshape(1, GLA_DV)
        wts = (g_nsa_out[i].reshape(1, -1), w_out[i].astype(BF16), g_ffn[i].reshape(1, -1), w_up[i].astype(BF16),
               conv_w[i], conv_b[i].reshape(1, -1), w_down[i].astype(BF16), w_ple_proj[i].astype(BF16),
               g_ple[i].reshape(1, -1), w_ple_gate[i].astype(BF16), g_final.reshape(1, -1))

        proj = _norm_matmul(hp, g_a, w_in_p, _row_tile(bp * tp, IN_PROJ_ROWS), IN_PROJ_COLS)
        kcv = _cmp_prompt(proj, bp, tp, w1c, w2c, pec)
        o_nsa = _nsa_prompt(proj, kcv, slopes, bp, tp)
        s0 = jnp.zeros((bp, GLA_HEADS, GLA_DK, GLA_DV), F32)
        o_gla, s_new = _gla(proj, w_gla_a2[i], b_a, g_go, s0, bp, tp, 64, 16, 64)
        zbuf = jnp.zeros((bp, CONV_W - 1, D_FF), F32)
        hp, tails = _dense_tail(hp, o_nsa, o_gla, p_prompt[i].reshape(bp * tp, PLE_DIM), wts, zbuf, zbuf, tp, last, tm_p)
        proj3 = proj.reshape(bp, tp, D_IN_PAD)
        new_p[0].append(proj3[:, :, COL_KVC:COL_KVC + KV_WIDTH].reshape((bp, tp) + kv_shape))
        new_p[1].append(proj3[:, :, COL_KVS:COL_KVS + KV_WIDTH].reshape((bp, tp) + kv_shape))
        new_p[2].append(proj3[:, tp - WINDOW:, COL_KVW:COL_KVW + KV_WIDTH].reshape((bp, WINDOW) + kv_shape))
        new_p[3].append(s_new)
        tiles_per_seq = tp // tm_p
        new_p[4].append(tails.reshape(bp, tiles_per_seq, SUBLANES, D_FF)[:, -1, SUBLANES - (CONV_W - 1):, :])

        proj_s = _norm_matmul(hs, g_a, w_in_p, tm_s, IN_PROJ_COLS)
        proj8 = jnp.pad(proj_s.reshape(bs, ts, D_IN_PAD), ((0, 0), (0, SUBLANES - ts), (0, 0))).reshape(bs * SUBLANES, D_IN_PAD)
        cache_c = jnp.transpose(cache_cmp_kv[i], (0, 2, 3, 4, 1)).reshape(n_pool, KV_WIDTH, PAGE_SIZE)
        cache_s = jnp.transpose(cache_sel_kv[i], (0, 2, 3, 4, 1)).reshape(n_pool, NSA_KV_HEADS, KV_SLAB, PAGE_SIZE)
        cache_w = jnp.transpose(cache_win_kv[i], (0, 2, 3, 4, 1)).reshape(bs, NSA_KV_HEADS, KV_SLAB, -1)
        kcv_s = _cmp_sample(cache_c, pt_flat, w1c, w2c, pec, bs, n_pages)
        o_c, idx = _sel_sample(proj8, kcv_s, slopes, bs, past_len, ts)
        o_nsa_s = _nsa_sample(proj8, cache_s, cache_w, o_c, idx.reshape(-1), pt_flat, slopes, bs, ts, past_len)
        o_nsa_s = o_nsa_s.reshape(bs, SUBLANES, NSA_WIDTH)[:, :ts].reshape(bs * ts, NSA_WIDTH)
        o_gla_s, s_new_s = _gla(proj8, w_gla_a2[i], b_a, g_go, state_gla[i].astype(F32), bs, SUBLANES, SUBLANES, SUBLANES, ts)
        o_gla_s = o_gla_s.reshape(bs, SUBLANES, GLA_WIDTH)[:, :ts].reshape(bs * ts, GLA_WIDTH)
        buf = state_ffn_conv[i]
        zrow = jnp.zeros((bs, ts - 1, D_FF), F32)
        p1 = jnp.concatenate([buf[:, 1:2], zrow], axis=1).reshape(bs * ts, D_FF)
        p2 = jnp.concatenate([buf, jnp.zeros((bs, ts - 2, D_FF), F32)], axis=1).reshape(bs * ts, D_FF)
        hs, tails_s = _dense_tail(hs, o_nsa_s, o_gla_s, p_sample[i].reshape(bs * ts, PLE_DIM), wts, p1, p2, ts, last, tm_s)
        ps3 = proj_s.reshape(bs, ts, D_IN_PAD)
        new_s[0].append(ps3[:, :, COL_KVC:COL_KVC + KV_WIDTH].reshape((bs, ts) + kv_shape))
        new_s[1].append(ps3[:, :, COL_KVS:COL_KVS + KV_WIDTH].reshape((bs, ts) + kv_shape))
        new_s[2].append(ps3[:, :, COL_KVW:COL_KVW + KV_WIDTH].reshape((bs, ts) + kv_shape))
        new_s[3].append(s_new_s)
        new_s[4].append(tails_s.reshape(bs, ts, D_FF)[:, ts - (CONV_W - 1):, :])

    y_prompt = hp.reshape(bp, tp, D_MODEL)
    y_sample = hs.reshape(bs, ts, D_MODEL)
    cmp_p, sel_p, win_p, gla_p, conv_p = [jnp.stack(l) for l in new_p]
    cmp_s, sel_s, win_s, gla_s, conv_s = [jnp.stack(l) for l in new_s]
    return (y_prompt, y_sample, cmp_p, sel_p, win_p, gla_p, conv_p, cmp_s, sel_s, win_s, gla_s, conv_s)
```

```python
import functools

import numpy as np
import jax
import jax.numpy as jnp
from jax import lax
from jax.experimental import pallas as pl
from jax.experimental.pallas import tpu as pltpu

F32 = jnp.float32
BF16 = jnp.bfloat16
HIGHEST = lax.Precision.HIGHEST

D_MODEL = 2048
PAGE_SIZE = 128
NSA_HEADS = 16
NSA_KV_HEADS = 4
NSA_GROUP = NSA_HEADS // NSA_KV_HEADS
HEAD_DIM = 64
CMP_BLOCK = 32
CMP_STRIDE = 16
CMP_HIDDEN = 2 * HEAD_DIM
SEL_BLOCK = 64
N_SELECT = 16
WINDOW = 512
TQ = 512
GLA_HEADS = 4
GLA_DK = 128
GLA_DV = 256
GLA_RANK = 16
GLA_TAU = 16.0
D_FF = 5632
CONV_W = 3
PLE_DIM = 256
EPS = 1e-6
NEG_INF = -1e30
FORCE_SCORE = 1e9
REMOVED_SCORE = -3e38
SCALE = HEAD_DIM ** -0.5
LOG2E = np.float32(1.4426950408889634)

NSA_WIDTH = NSA_HEADS * HEAD_DIM
GLA_WIDTH = GLA_HEADS * GLA_DV
KV_WIDTH = 2 * NSA_KV_HEADS * HEAD_DIM
KV_SLAB = 2 * HEAD_DIM
Q_SLAB = NSA_GROUP * HEAD_DIM
GATE_W = 3 * NSA_HEADS

LANES = 128
SUBLANES = 8
VMEM_LIMIT = 56 * 1024 * 1024

COL_Q = 0
COL_GV = COL_Q + NSA_WIDTH
COL_GR = COL_GV + GLA_WIDTH
COL_KVC = COL_GR + GLA_WIDTH
COL_KVS = COL_KVC + KV_WIDTH
COL_KVW = COL_KVS + KV_WIDTH
COL_GQ = COL_KVW + KV_WIDTH
COL_GK = COL_GQ + GLA_HEADS * GLA_DK
COL_MISC = COL_GK + GLA_HEADS * GLA_DK
D_IN_PAD = COL_MISC + LANES
MISC_GA = GATE_W

DENSE_ROWS = 512
IN_PROJ_ROWS = 512
IN_PROJ_COLS = D_IN_PAD // 5
FFN_COLS = 512


def _cparams(sem):
    return pltpu.CompilerParams(dimension_semantics=sem, vmem_limit_bytes=VMEM_LIMIT)


def _rms(x, g):
    return x * lax.rsqrt(jnp.mean(x * x, axis=-1, keepdims=True) + EPS) * g


def _gelu_tanh(x):
    return 0.5 * x * (1.0 + jnp.tanh(np.float32(np.sqrt(2.0 / np.pi)) * (x + 0.044715 * (x * x * x))))


def _sigmoid(x):
    return 1.0 / (1.0 + jnp.exp(-x))


def _dot(a, b):
    return jnp.dot(a, b, preferred_element_type=F32)


def _dot_nt(a, b):
    return lax.dot_general(a, b, (((1,), (1,)), ((), ())), preferred_element_type=F32)


def _dot_tn(a, b):
    return lax.dot_general(a, b, (((0,), (0,)), ((), ())), preferred_element_type=F32)


def _norm_matmul_kernel(x_ref, g_ref, w_ref, o_ref, xn_ref):
    @pl.when(pl.program_id(1) == 0)
    def _():
        xn_ref[...] = _rms(x_ref[...], g_ref[...]).astype(BF16)

    o_ref[...] = _dot(xn_ref[...], w_ref[...])


def _norm_matmul(x, g, w, tm, tn):
    m, d = x.shape
    n = w.shape[1]
    return pl.pallas_call(
        _norm_matmul_kernel,
        grid=(m // tm, n // tn),
        in_specs=[pl.BlockSpec((tm, d), lambda i, j: (i, 0)),
                  pl.BlockSpec((1, d), lambda i, j: (0, 0)),
                  pl.BlockSpec((d, tn), lambda i, j: (0, j))],
        out_specs=pl.BlockSpec((tm, tn), lambda i, j: (i, j)),
        out_shape=jax.ShapeDtypeStruct((m, n), F32),
        scratch_shapes=[pltpu.VMEM((tm, d), BF16)],
        compiler_params=_cparams(("parallel", "arbitrary")),
        name="in_proj",
    )(x, g, w)


def _cmp_pe_hidden(pe_ref, w1_ref):
    span = CMP_BLOCK // CMP_STRIDE
    acc = jnp.zeros((SUBLANES, 2 * CMP_HIDDEN), F32)
    for j in range(span):
        for s in range(CMP_STRIDE):
            row = jnp.broadcast_to(pe_ref[pl.ds(j * CMP_STRIDE + s, 1), :], (SUBLANES, KV_SLAB)).astype(BF16)
            acc = acc + _dot(row, w1_ref[s][:, j * 2 * CMP_HIDDEN:(j + 1) * 2 * CMP_HIDDEN])
    return acc[0:1, :]


def _cmp_prompt_kernel(x_ref, w1_ref, w2_ref, pe_ref, o_ref, *, n16):
    acc = jnp.zeros((n16, 4 * CMP_HIDDEN), F32)
    for s in range(CMP_STRIDE):
        xs = x_ref[pl.ds(s, n16, stride=CMP_STRIDE), :].astype(BF16)
        acc = acc + _dot(xs, w1_ref[s])
    a = acc[:, :2 * CMP_HIDDEN]
    b_next = pltpu.roll(acc[:, 2 * CMP_HIDDEN:], n16 - 1, axis=0)
    hid = a + b_next + _cmp_pe_hidden(pe_ref, w1_ref)
    o_ref[...] = _dot(_gelu_tanh(hid).astype(BF16), w2_ref[...])


def _cmp_weights(w_k1, w_k2, pe_k, w_v1, w_v2, pe_v):
    z = jnp.zeros((CMP_BLOCK, HEAD_DIM, CMP_HIDDEN), F32)
    wl = jnp.concatenate([jnp.concatenate([w_k1, z], axis=2), jnp.concatenate([z, w_v1], axis=2)], axis=1)
    w1 = jnp.concatenate([wl[:CMP_STRIDE], wl[CMP_STRIDE:]], axis=2).astype(BF16)
    z2 = jnp.zeros((CMP_HIDDEN, HEAD_DIM), F32)
    w2 = jnp.concatenate([jnp.concatenate([w_k2, z2], axis=1), jnp.concatenate([z2, w_v2], axis=1)], axis=0)
    pe = jnp.concatenate([pe_k, pe_v], axis=1)
    return w1, w2.astype(BF16), pe


def _cmp_prompt(proj, b, t, w1, w2, pe):
    n16 = t // CMP_STRIDE
    col0 = COL_KVC // KV_SLAB
    return pl.pallas_call(
        functools.partial(_cmp_prompt_kernel, n16=n16),
        grid=(b, NSA_KV_HEADS),
        in_specs=[pl.BlockSpec((t, KV_SLAB), lambda i, n: (i, col0 + n)),
                  pl.BlockSpec(w1.shape, lambda i, n: (0, 0, 0)),
                  pl.BlockSpec(w2.shape, lambda i, n: (0, 0)),
                  pl.BlockSpec(pe.shape, lambda i, n: (0, 0))],
        out_specs=pl.BlockSpec((None, None, n16, KV_SLAB), lambda i, n: (i, n, 0, 0)),
        out_shape=jax.ShapeDtypeStruct((b, NSA_KV_HEADS, n16, KV_SLAB), F32),
        compiler_params=_cparams(("parallel", "parallel")),
        name="cmp_prompt",
    )(proj, w1, w2, pe)


def _overlap_counts(c, j):
    per_sel = SEL_BLOCK // CMP_STRIDE
    ov = sum(((c + k >= per_sel * j) & (c + k < per_sel * (j + 1))) for k in range(CMP_BLOCK // CMP_STRIDE))
    return np.where(c >= 0, ov, 0).astype(np.float32)


def _top_blocks(imp, n_pick):
    rows, nsel = imp.shape
    j = lax.broadcasted_iota(jnp.int32, (rows, nsel), 1).astype(F32)
    mask = jnp.zeros((rows, nsel), F32)
    picks = []
    for _ in range(n_pick):
        m = jnp.max(imp, axis=-1, keepdims=True)
        jmin = jnp.min(jnp.where(imp == m, j, float(nsel)), axis=-1, keepdims=True)
        hit = j == jmin
        mask = jnp.where(hit, 1.0, mask)
        imp = jnp.where(hit, REMOVED_SCORE, imp)
        picks.append(jmin.astype(jnp.int32))
    return mask, picks


def _slope_col(slopes_ref, n, rows_per_head, rows):
    g = lax.broadcasted_iota(jnp.int32, (rows, 1), 0) // rows_per_head
    col = jnp.zeros((rows, 1), F32)
    for gg in range(NSA_GROUP):
        col = jnp.where(g == gg, slopes_ref[n * NSA_GROUP + gg], col)
    return col


def _pick_head_gates(gl, n):
    out = jnp.zeros((gl.shape[0], 3 * NSA_GROUP), F32)
    for nn in range(NSA_KV_HEADS):
        out = jnp.where(n == nn, gl[:, nn * 3 * NSA_GROUP:(nn + 1) * 3 * NSA_GROUP], out)
    return _sigmoid(out)


MASK_BIAS = -131072.0
M_FLOOR = -65536.0
POS_HI, POS_LO = HEAD_DIM, HEAD_DIM + 3
KEY_TILE = TQ


def _slope_features():
    h = np.arange(1, NSA_HEADS + 1, dtype=np.float32)
    slopes = (2.0 ** (-8.0 * h / NSA_HEADS)).astype(np.float32)
    tab = np.zeros((NSA_HEADS, LANES), np.float32)
    rest = slopes * LOG2E
    for c in range(3):
        piece = rest.astype(BF16).astype(np.float32)
        tab[:, POS_HI + c] = piece
        tab[:, POS_LO + c] = piece
        rest = rest - piece
    assert not rest.any()
    return jnp.asarray(tab)


def _kv_prep_kernel(kvs_ref, kvw_ref, ks_ref, vs_ref, kw_ref, vw_ref, *, t):
    lane = lax.broadcasted_iota(jnp.int32, (t, LANES), 1)
    pos = lax.broadcasted_iota(jnp.int32, (t, LANES), 0)
    hi = ((pos // SEL_BLOCK) * SEL_BLOCK).astype(F32)
    lo = (pos % SEL_BLOCK).astype(F32)
    feat = jnp.where((lane >= POS_HI) & (lane < POS_HI + 3), hi,
                     jnp.where((lane >= POS_LO) & (lane < POS_LO + 3), lo, 0.0))
    ones_col = jnp.where(lane == HEAD_DIM, 1.0, 0.0)
    for src, k_out, v_out in ((kvs_ref, ks_ref, vs_ref), (kvw_ref, kw_ref, vw_ref)):
        x = src[...]
        k_out[:, 0:LANES] = jnp.where(lane < HEAD_DIM, x, feat).astype(BF16)
        v_out[...] = jnp.where(lane < HEAD_DIM, pltpu.roll(x, HEAD_DIM, axis=1), ones_col).astype(BF16)
    ks_ref[:, LANES:2 * LANES] = jnp.where(lane == pos // SEL_BLOCK, 1.0, 0.0).astype(BF16)


def _kv_prep(proj, b, t):
    spec_in = lambda col: pl.BlockSpec((t, KV_SLAB), lambda i, n: (i, col // KV_SLAB + n))
    spec_out = lambda w: pl.BlockSpec((None, None, t, w), lambda i, n: (i, n, 0, 0))
    shape = lambda w: jax.ShapeDtypeStruct((b, NSA_KV_HEADS, t, w), BF16)
    return pl.pallas_call(
        functools.partial(_kv_prep_kernel, t=t),
        grid=(b, NSA_KV_HEADS),
        in_specs=[spec_in(COL_KVS), spec_in(COL_KVW)],
        out_specs=[spec_out(2 * LANES), spec_out(LANES), spec_out(LANES), spec_out(LANES)],
        out_shape=[shape(2 * LANES), shape(LANES), shape(LANES), shape(LANES)],
        compiler_params=_cparams(("parallel", "parallel")),
        name="kv_prep",
    )(proj, proj)


def _nsa_prompt_kernel(slopes_ref, q_ref, kcv_ref, ks_ref, vs_ref, kw_ref, vw_ref, misc_ref, sfeat_ref, ovt_ref, o_ref,
                       *, n16, n_sel):
    n = pl.program_id(1)
    qb = pl.program_id(2)
    rows = NSA_GROUP * TQ
    qf = q_ref[...] * SCALE
    q = jnp.concatenate([qf[:, g * HEAD_DIM:(g + 1) * HEAD_DIM] for g in range(NSA_GROUP)], axis=0).astype(BF16)
    tok = lax.broadcasted_iota(jnp.int32, (rows, 1), 0) % TQ
    qpos = qb * TQ + tok
    slope = _slope_col(slopes_ref, n, TQ, rows)

    kcv = kcv_ref[...]
    kc = kcv[:, :HEAD_DIM].astype(BF16)
    vc = kcv[:, HEAD_DIM:].astype(BF16)
    s = _dot_nt(q, kc)
    end = lax.broadcasted_iota(jnp.int32, (1, n16), 1) * CMP_STRIDE + (CMP_BLOCK - 1)
    dist = (qpos - end).astype(F32)
    valid = (dist >= 0) & (end < n16 * CMP_STRIDE)
    s = jnp.where(valid, s - slope * dist, NEG_INF)
    e = jnp.exp(s - jnp.max(s, axis=-1, keepdims=True))
    p = jnp.where(valid, e / jnp.sum(e, axis=-1, keepdims=True), 0.0)
    o_c = _dot(p.astype(BF16), vc)
    p_grp = p[0:TQ]
    for g in range(1, NSA_GROUP):
        p_grp = p_grp + p[g * TQ:(g + 1) * TQ]
    imp = lax.dot_general(ovt_ref[...], p_grp, (((1,), (1,)), ((), ())),
                          preferred_element_type=F32, precision=HIGHEST)

    jj = lax.broadcasted_iota(jnp.int32, (n_sel, TQ), 0)
    cur = (qb * TQ + lax.broadcasted_iota(jnp.int32, (1, TQ), 1)) // SEL_BLOCK
    forced = (jj == 0) | (jj == cur) | (jj == cur - 1)
    imp = jnp.where(forced, FORCE_SCORE, jnp.where(jj > cur, -FORCE_SCORE, imp))
    beaten = jnp.zeros((n_sel, TQ), F32)
    for jp in range(n_sel):
        other = imp[jp:jp + 1, :]
        beats = (other > imp) | ((other == imp) & (jj > jp))
        beaten = beaten + jnp.where(beats, 1.0, 0.0)
    not_picked = jnp.where(beaten < min(N_SELECT, n_sel), 0.0, 1.0)
    if n_sel < LANES:
        not_picked = jnp.concatenate([not_picked, jnp.zeros((LANES - n_sel, TQ), F32)], axis=0)
    q_bias = jnp.transpose(not_picked) * MASK_BIAS

    qf2 = qf * LOG2E
    q_main = jnp.concatenate(
        [jnp.concatenate([qf2[:, g * HEAD_DIM:(g + 1) * HEAD_DIM],
                          jnp.broadcast_to(sfeat_ref[pl.ds(n * NSA_GROUP + g, 1), HEAD_DIM:LANES],
                                           (TQ, LANES - HEAD_DIM))], axis=1)
         for g in range(NSA_GROUP)], axis=0)
    groups = [slice(g * TQ, (g + 1) * TQ) for g in range(NSA_GROUP)]
    q_win = [q_main[r].astype(BF16) for r in groups]
    q_sel = [jnp.concatenate([q_main[r], q_bias], axis=1).astype(BF16) for r in groups]
    tpos = qb * TQ + lax.broadcasted_iota(jnp.int32, (TQ, 1), 0)

    def flash_step(carry, scores, v):
        out = []
        for (m, acc), sc in zip(carry, scores):
            m_new = jnp.maximum(m, jnp.max(sc, axis=-1, keepdims=True))
            pr = jnp.exp2(sc - m_new).astype(BF16)
            out.append((m_new, jnp.exp2(m - m_new) * acc + _dot(pr, v)))
        return tuple(out)

    def finish(carry):
        return [(acc[:, :HEAD_DIM], acc[:, HEAD_DIM:HEAD_DIM + 1]) for _, acc in carry]

    init = tuple((jnp.full((TQ, 1), M_FLOOR, F32), jnp.zeros((TQ, LANES), F32)) for _ in groups)

    def sel_scores(r, keep=None):
        k = ks_ref[r, :]
        sc = [_dot_nt(qg, k) for qg in q_sel]
        return sc if keep is None else [jnp.where(keep, s_, NEG_INF) for s_ in sc]

    def key_rows(kt):
        return pl.ds(pl.multiple_of(kt * KEY_TILE, KEY_TILE), KEY_TILE)

    def sel_body(kt, carry):
        return flash_step(carry, sel_scores(key_rows(kt)), vs_ref[key_rows(kt), :])

    n_full = (qb * TQ) // KEY_TILE
    carry = lax.fori_loop(0, n_full, sel_body, init)
    sc = sel_scores(key_rows(n_full))
    kpos = n_full * KEY_TILE + lax.broadcasted_iota(jnp.int32, (1, KEY_TILE), 1)
    sc = [jnp.where(kpos <= tpos, s_, NEG_INF) for s_ in sc]
    o_s = finish(flash_step(carry, sc, vs_ref[key_rows(n_full), :]))

    n_win = WINDOW // TQ + 1
    lane_q = lax.broadcasted_iota(jnp.int32, (1, TQ), 1)
    win_sc = [[] for _ in groups]
    win_v = []
    for rel in range(n_win):
        kt = qb - (n_win - 1) + rel
        rows_k = pl.ds(pl.multiple_of(jnp.maximum(kt, 0) * TQ, TQ), TQ)
        k = kw_ref[rows_k, :]
        win_v.append(vw_ref[rows_k, :])
        dist = tpos - (kt * TQ + lane_q)
        keep = jnp.full((1, TQ), kt, jnp.int32) >= 0
        if rel == 0:
            keep = keep & (dist < WINDOW)
        if rel == n_win - 1:
            keep = keep & (dist >= 0)
        for g, qg in enumerate(q_win):
            win_sc[g].append(jnp.where(keep, _dot_nt(qg, k), NEG_INF))
    o_w = finish(flash_step(init, [jnp.concatenate(s_, axis=1) for s_ in win_sc], jnp.concatenate(win_v, axis=0)))

    gates = _pick_head_gates(misc_ref[:, 0:GATE_W], n)
    outs = []
    for g, r in enumerate(groups):
        (u_s, l_s), (u_w, l_w) = o_s[g], o_w[g]
        outs.append(gates[:, 3 * g:3 * g + 1] * o_c[r] + (gates[:, 3 * g + 1:3 * g + 2] / l_s) * u_s
                    + (gates[:, 3 * g + 2:3 * g + 3] / l_w) * u_w)
    o_ref[...] = jnp.concatenate(outs, axis=1)


def _nsa_prompt(proj, kcv, slopes, b, t):
    n16 = t // CMP_STRIDE
    n_sel = t // SEL_BLOCK
    nqb = t // TQ
    assert n_sel <= LANES and t % KEY_TILE == 0
    k_sel, v_sel, k_win, v_win = _kv_prep(proj, b, t)
    seq = lambda w: pl.BlockSpec((None, None, t, w), lambda i, n, qb, sl: (i, n, 0, 0))
    grid_spec = pltpu.PrefetchScalarGridSpec(
        num_scalar_prefetch=1,
        grid=(b, NSA_KV_HEADS, nqb),
        in_specs=[pl.BlockSpec((TQ, Q_SLAB), lambda i, n, qb, sl: (i * nqb + qb, COL_Q // Q_SLAB + n)),
                  pl.BlockSpec((None, None, n16, KV_SLAB), lambda i, n, qb, sl: (i, n, 0, 0)),
                  seq(2 * LANES), seq(LANES), seq(LANES), seq(LANES),
                  pl.BlockSpec((TQ, LANES), lambda i, n, qb, sl: (i * nqb + qb, COL_MISC // LANES)),
                  pl.BlockSpec((NSA_HEADS, LANES), lambda i, n, qb, sl: (0, 0)),
                  pl.BlockSpec((n_sel, n16), lambda i, n, qb, sl: (0, 0))],
        out_specs=pl.BlockSpec((TQ, Q_SLAB), lambda i, n, qb, sl: (i * nqb + qb, n)),
    )
    overlap_t = jnp.asarray(_overlap_counts(np.arange(n16)[None, :], np.arange(n_sel)[:, None]))
    return pl.pallas_call(
        functools.partial(_nsa_prompt_kernel, n16=n16, n_sel=n_sel),
        grid_spec=grid_spec,
        out_shape=jax.ShapeDtypeStruct((b * t, NSA_WIDTH), F32),
        compiler_params=_cparams(("parallel", "parallel", "arbitrary")),
        name="nsa_prompt",
    )(slopes, proj, kcv, k_sel, v_sel, k_win, v_win, proj, _slope_features(), overlap_t)


def _gla_kernel(q_ref, k_ref, v_ref, r_ref, misc_ref, wa_ref, ba_ref, gout_ref, s0_ref, o_ref, sfin_ref, state_ref,
                *, chunk, sub, valid_rows):
    c = pl.program_id(1)

    @pl.when(c == 0)
    def _():
        state_ref[...] = s0_ref[...]

    ga = misc_ref[:, MISC_GA:MISC_GA + GLA_RANK]
    x = jnp.dot(ga, wa_ref[...], preferred_element_type=F32, precision=HIGHEST) + ba_ref[...]
    lg = (jnp.minimum(x, 0.0) - jnp.log1p(jnp.exp(-jnp.abs(x)))) / GLA_TAU
    row = lax.broadcasted_iota(jnp.int32, (chunk, 1), 0)
    if valid_rows < chunk:
        lg = jnp.where(row < valid_rows, lg, 0.0)
    tri = (lax.broadcasted_iota(jnp.int32, (chunk, chunk), 0)
           >= lax.broadcasted_iota(jnp.int32, (chunk, chunk), 1)).astype(F32)
    cum_all = jnp.dot(tri, lg, preferred_element_type=F32, precision=HIGHEST)
    for h in range(GLA_HEADS):
        kcols = slice(h * GLA_DK, (h + 1) * GLA_DK)
        vcols = slice(h * GLA_DV, (h + 1) * GLA_DV)
        o, new_state = _gla_head_chunk(q_ref[:, kcols] * (GLA_DK ** -0.5), k_ref[:, kcols], v_ref[:, vcols],
                                       cum_all[:, kcols], state_ref[h], chunk, sub)
        state_ref[h] = new_state
        rg = r_ref[:, vcols]
        o_ref[:, vcols] = _rms(o, gout_ref[...]) * (rg * _sigmoid(rg))

    @pl.when(c == pl.num_programs(1) - 1)
    def _():
        sfin_ref[...] = state_ref[...]


def _gla_head_chunk(q, k, v, cum, state, chunk, sub):
    vb = v.astype(BF16)
    inter = _dot((q * jnp.exp(cum)).astype(BF16), state.astype(BF16))

    outs = []
    for i in range(chunk // sub):
        r0 = i * sub
        qi, ki, ci, vi = q[r0:r0 + sub], k[r0:r0 + sub], cum[r0:r0 + sub], v[r0:r0 + sub]
        o_i = inter[r0:r0 + sub]
        if i > 0:
            anchor = cum[r0:r0 + 1]
            qd = (qi * jnp.exp(ci - anchor)).astype(BF16)
            kd = (k[0:r0] * jnp.exp(anchor - cum[0:r0])).astype(BF16)
            o_i = o_i + _dot(_dot_nt(qd, kd).astype(BF16), vb[0:r0])
        trow = lax.broadcasted_iota(jnp.int32, (sub, 1), 0)
        for s_ in range(sub):
            w = jnp.sum(qi * (ki[s_:s_ + 1] * jnp.exp(jnp.minimum(ci - ci[s_:s_ + 1], 0.0))), axis=-1, keepdims=True)
            o_i = o_i + jnp.where(trow >= s_, w, 0.0) * vi[s_:s_ + 1]
        outs.append(o_i)
    o = jnp.concatenate(outs, axis=0) if len(outs) > 1 else outs[0]

    last = cum[chunk - 1:chunk]
    kdec = (k * jnp.exp(last - cum)).astype(BF16)
    decay_col = jnp.transpose(jnp.broadcast_to(jnp.exp(last), (SUBLANES, GLA_DK)))[:, 0:1]
    return o, decay_col * state + _dot_tn(kdec, vb)


def _gla(proj, w_a2, b_a, g_out, s0, b, t, chunk, sub, valid_rows):
    nck = t // chunk
    kw = GLA_HEADS * GLA_DK
    state_spec = pl.BlockSpec((None, GLA_HEADS, GLA_DK, GLA_DV), lambda i, c: (i, 0, 0, 0))
    return pl.pallas_call(
        functools.partial(_gla_kernel, chunk=chunk, sub=sub, valid_rows=valid_rows),
        grid=(b, nck),
        in_specs=[pl.BlockSpec((chunk, kw), lambda i, c: (i * nck + c, COL_GQ // kw)),
                  pl.BlockSpec((chunk, kw), lambda i, c: (i * nck + c, COL_GK // kw)),
                  pl.BlockSpec((chunk, GLA_WIDTH), lambda i, c: (i * nck + c, COL_GV // GLA_WIDTH)),
                  pl.BlockSpec((chunk, GLA_WIDTH), lambda i, c: (i * nck + c, COL_GR // GLA_WIDTH)),
                  pl.BlockSpec((chunk, LANES), lambda i, c: (i * nck + c, COL_MISC // LANES)),
                  pl.BlockSpec((GLA_RANK, kw), lambda i, c: (0, 0)),
                  pl.BlockSpec((1, kw), lambda i, c: (0, 0)),
                  pl.BlockSpec((1, GLA_DV), lambda i, c: (0, 0)),
                  state_spec],
        out_specs=[pl.BlockSpec((chunk, GLA_WIDTH), lambda i, c: (i * nck + c, 0)), state_spec],
        out_shape=[jax.ShapeDtypeStruct((b * t, GLA_WIDTH), F32),
                   jax.ShapeDtypeStruct((b, GLA_HEADS, GLA_DK, GLA_DV), F32)],
        scratch_shapes=[pltpu.VMEM((GLA_HEADS, GLA_DK, GLA_DV), F32)],
        compiler_params=_cparams(("parallel", "arbitrary")),
        name="gla",
    )(proj, proj, proj, proj, proj, w_a2, b_a, g_out, s0)


def _out_proj_kernel(on_ref, og_ref, x_ref, g_ref, w_ref, o_ref, a_ref):
    @pl.when(pl.program_id(1) == 0)
    def _():
        a_ref[:, :NSA_WIDTH] = _rms(on_ref[...], g_ref[...]).astype(BF16)
        a_ref[:, NSA_WIDTH:] = og_ref[...].astype(BF16)

    o_ref[...] = x_ref[...] + _dot(a_ref[...], w_ref[...])


def _out_proj(o_nsa, o_gla, x, g_nsa, w_out, tm, tn):
    m = x.shape[0]
    return pl.pallas_call(
        _out_proj_kernel,
        grid=(m // tm, D_MODEL // tn),
        in_specs=[pl.BlockSpec((tm, NSA_WIDTH), lambda i, j: (i, 0)),
                  pl.BlockSpec((tm, GLA_WIDTH), lambda i, j: (i, 0)),
                  pl.BlockSpec((tm, tn), lambda i, j: (i, j)),
                  pl.BlockSpec((1, NSA_WIDTH), lambda i, j: (0, 0)),
                  pl.BlockSpec((NSA_WIDTH + GLA_WIDTH, tn), lambda i, j: (0, j))],
        out_specs=pl.BlockSpec((tm, tn), lambda i, j: (i, j)),
        out_shape=jax.ShapeDtypeStruct((m, D_MODEL), F32),
        scratch_shapes=[pltpu.VMEM((tm, NSA_WIDTH + GLA_WIDTH), BF16)],
        compiler_params=_cparams(("parallel", "arbitrary")),
        name="out_proj",
    )(o_nsa, o_gla, x, g_nsa, w_out)


def _ffn_kernel(h_ref, g_ref, wa_ref, wg_ref, cw_ref, cb_ref, wd_ref, p1_ref, p2_ref, o_ref, tail_ref,
                n2_ref, acc_ref, carry_ref, *, tm, tf, seq_rows):
    i = pl.program_id(0)
    j = pl.program_id(1)

    @pl.when(j == 0)
    def _():
        n2_ref[...] = _rms(h_ref[...], g_ref[...]).astype(BF16)
        acc_ref[...] = jnp.zeros_like(acc_ref)

    n2 = n2_ref[...]
    a = _dot(n2, wa_ref[...])
    gate = _dot(n2, wg_ref[...])
    row = lax.broadcasted_iota(jnp.int32, (tm, 1), 0)
    r1 = pltpu.roll(a, 1, axis=0)
    r2 = pltpu.roll(a, 2, axis=0)
    if seq_rows >= tm:
        cols = pl.ds(pl.multiple_of(j * tf, tf), tf)
        first = (i % (seq_rows // tm)) == 0
        prev = jnp.where(first, p2_ref[...], carry_ref[:, cols])
        a1 = jnp.where(row == 0, prev[1:2], r1)
        a2 = jnp.where(row == 0, prev[0:1], jnp.where(row == 1, prev[1:2], r2))
        carry_ref[:, cols] = a[tm - 2:tm]
    else:
        t = row % seq_rows
        a1 = jnp.where(t == 0, p1_ref[...], r1)
        a2 = jnp.where(t < 2, p2_ref[...], r2)
    cw = cw_ref[...]
    conv = cb_ref[...] + a2 * cw[0:1] + a1 * cw[1:2] + a * cw[2:3]
    y = (_gelu_tanh(conv) * gate).astype(BF16)
    acc_ref[...] += _dot(y, wd_ref[...])
    tail_ref[...] = a[tm - tail_ref.shape[0]:tm]

    @pl.when(j == pl.num_programs(1) - 1)
    def _():
        o_ref[...] = h_ref[...] + acc_ref[...]


def _ffn(h, g_ffn, w_up, conv_w, conv_b, w_down, p1, p2, tm, tf, seq_rows):
    m = h.shape[0]
    nj = D_FF // tf
    if seq_rows >= tm:
        tiles_per_seq = seq_rows // tm
        p1_spec = pl.BlockSpec((None, CONV_W - 1, tf), lambda i, j: (i // tiles_per_seq, 0, j))
        p2_spec = pl.BlockSpec((None, CONV_W - 1, tf), lambda i, j: (i // tiles_per_seq, 0, j))
    else:
        p1_spec = pl.BlockSpec((tm, tf), lambda i, j: (i, j))
        p2_spec = pl.BlockSpec((tm, tf), lambda i, j: (i, j))
    if seq_rows >= tm:
        tail_spec = pl.BlockSpec((None, SUBLANES, tf), lambda i, j: (i, 0, j))
        tail_shape = jax.ShapeDtypeStruct((m // tm, SUBLANES, D_FF), F32)
    else:
        tail_spec = pl.BlockSpec((tm, tf), lambda i, j: (i, j))
        tail_shape = jax.ShapeDtypeStruct((m, D_FF), F32)
    return pl.pallas_call(
        functools.partial(_ffn_kernel, tm=tm, tf=tf, seq_rows=seq_rows),
        grid=(m // tm, nj),
        in_specs=[pl.BlockSpec((tm, D_MODEL), lambda i, j: (i, 0)),
                  pl.BlockSpec((1, D_MODEL), lambda i, j: (0, 0)),
                  pl.BlockSpec((D_MODEL, tf), lambda i, j: (0, j)),
                  pl.BlockSpec((D_MODEL, tf), lambda i, j: (0, nj + j)),
                  pl.BlockSpec((CONV_W, tf), lambda i, j: (0, j)),
                  pl.BlockSpec((1, tf), lambda i, j: (0, j)),
                  pl.BlockSpec((tf, D_MODEL), lambda i, j: (j, 0)),
                  p1_spec, p2_spec],
        out_specs=[pl.BlockSpec((tm, D_MODEL), lambda i, j: (i, 0)),
                   tail_spec],
        out_shape=[jax.ShapeDtypeStruct((m, D_MODEL), F32), tail_shape],
        scratch_shapes=[pltpu.VMEM((tm, D_MODEL), BF16), pltpu.VMEM((tm, D_MODEL), F32),
                        pltpu.VMEM((CONV_W - 1, D_FF), F32)],
        compiler_params=_cparams(("arbitrary", "arbitrary")),
        name="conv_ffn",
    )(h, g_ffn, w_up, w_up, conv_w, conv_b, w_down, p1, p2)


def _ple_kernel(h_ref, p_ref, wg_ref, wp_ref, gp_ref, gf_ref, o_ref, *, final_norm):
    h = h_ref[...]
    gate = _sigmoid(_dot(h.astype(BF16), wg_ref[...]))
    pe = _rms(_dot(p_ref[...].astype(BF16), wp_ref[...]), gp_ref[...])
    h = h + gate * pe
    o_ref[...] = _rms(h, gf_ref[...]) if final_norm else h


def _ple(h, p, w_gate, w_proj, g_ple, g_final, tm, final_norm):
    m = h.shape[0]
    return pl.pallas_call(
        functools.partial(_ple_kernel, final_norm=final_norm),
        grid=(m // tm,),
        in_specs=[pl.BlockSpec((tm, D_MODEL), lambda i: (i, 0)),
                  pl.BlockSpec((tm, PLE_DIM), lambda i: (i, 0)),
                  pl.BlockSpec((D_MODEL, D_MODEL), lambda i: (0, 0)),
                  pl.BlockSpec((PLE_DIM, D_MODEL), lambda i: (0, 0)),
                  pl.BlockSpec((1, D_MODEL), lambda i: (0, 0)),
                  pl.BlockSpec((1, D_MODEL), lambda i: (0, 0))],
        out_specs=pl.BlockSpec((tm, D_MODEL), lambda i: (i, 0)),
        out_shape=jax.ShapeDtypeStruct((m, D_MODEL), F32),
        compiler_params=_cparams(("parallel",)),
        name="ple_norm",
    )(h, p, w_gate, w_proj, g_ple, g_final)


CMP_PAGES = 16
CMP_BATCH = 4
CHUNKS_PER_PAGE = PAGE_SIZE // CMP_STRIDE


def _cmp_sample_kernel(pt_ref, *refs):
    n_in = CMP_BATCH * CMP_PAGES
    perm_ref, w1_ref, w1p_ref, w2_ref, pe_ref, o_ref, carry_ref, x_ref, pe_hid_ref = refs[n_in:]

    @pl.when((pl.program_id(0) == 0) & (pl.program_id(1) == 0))
    def _():
        pe_hid_ref[...] = jnp.broadcast_to(_cmp_pe_hidden(pe_ref, w1_ref), pe_hid_ref.shape)

    @pl.when(pl.program_id(1) == 0)
    def _():
        carry_ref[...] = jnp.zeros_like(carry_ref)

    for e in range(CMP_BATCH):
        _cmp_sample_group(refs[e * CMP_PAGES:(e + 1) * CMP_PAGES], perm_ref, w1p_ref, w2_ref, pe_hid_ref[0:1, :],
                          o_ref.at[e], carry_ref.at[e], x_ref.at[e])


def _cmp_sample_group(page_refs, perm_ref, w1p_ref, w2_ref, pe_hid, o_ref, carry_ref, x_ref):
    nck = CMP_PAGES * CHUNKS_PER_PAGE
    perm = perm_ref[...]
    for k, r in enumerate(page_refs):
        xp = _dot_nt(perm, r[...].astype(BF16))
        for s in range(CMP_STRIDE):
            for n in range(NSA_KV_HEADS):
                x_ref[s, n, k * CHUNKS_PER_PAGE:(k + 1) * CHUNKS_PER_PAGE, :] = (
                    xp[s * CHUNKS_PER_PAGE:(s + 1) * CHUNKS_PER_PAGE, n * KV_SLAB:(n + 1) * KV_SLAB])
    acc = jnp.zeros((NSA_KV_HEADS * nck, 4 * CMP_HIDDEN), F32)
    for sp in range(CMP_STRIDE // 2):
        xs = jnp.concatenate([x_ref[2 * sp].reshape(NSA_KV_HEADS * nck, KV_SLAB),
                              x_ref[2 * sp + 1].reshape(NSA_KV_HEADS * nck, KV_SLAB)], axis=1)
        acc = acc + _dot(xs.astype(BF16), w1p_ref[sp])
    a = acc[:, :2 * CMP_HIDDEN]
    b = acc[:, 2 * CMP_HIDDEN:]
    a_prev = pltpu.roll(a, 1, axis=0)
    row = lax.broadcasted_iota(jnp.int32, (NSA_KV_HEADS * nck, 1), 0)
    for n in range(NSA_KV_HEADS):
        a_prev = jnp.where(row == n * nck, carry_ref[n:n + 1, :], a_prev)
        carry_ref[n:n + 1, :] = a[(n + 1) * nck - 1:(n + 1) * nck]
    hid = a_prev + b + pe_hid
    res = _dot(_gelu_tanh(hid).astype(BF16), w2_ref[...])
    for n in range(NSA_KV_HEADS):
        o_ref[n] = res[n * nck:(n + 1) * nck]


def _cmp_sample(cache_c, page_table_flat, w1, w2, pe, b, n_pages):
    n_grp = n_pages // CMP_PAGES
    nck = CMP_PAGES * CHUNKS_PER_PAGE
    assert b % CMP_BATCH == 0

    def page_spec(e, k):
        return pl.BlockSpec((None, KV_WIDTH, PAGE_SIZE),
                            lambda i, g, pt: (pt[(i * CMP_BATCH + e) * n_pages + g * CMP_PAGES + k], 0, 0))

    r = np.arange(PAGE_SIZE)
    perm_np = np.zeros((PAGE_SIZE, PAGE_SIZE), np.float32)
    perm_np[(r % CMP_STRIDE) * CHUNKS_PER_PAGE + r // CMP_STRIDE, r] = 1.0
    perm = jnp.asarray(perm_np, dtype=BF16)
    w1p = w1.reshape(CMP_STRIDE // 2, 2 * KV_SLAB, 4 * CMP_HIDDEN)

    grid_spec = pltpu.PrefetchScalarGridSpec(
        num_scalar_prefetch=1,
        grid=(b // CMP_BATCH, n_grp),
        in_specs=[page_spec(e, k) for e in range(CMP_BATCH) for k in range(CMP_PAGES)]
        + [pl.BlockSpec(perm.shape, lambda i, g, pt: (0, 0)),
           pl.BlockSpec(w1.shape, lambda i, g, pt: (0, 0, 0)),
           pl.BlockSpec(w1p.shape, lambda i, g, pt: (0, 0, 0)),
           pl.BlockSpec(w2.shape, lambda i, g, pt: (0, 0)),
           pl.BlockSpec(pe.shape, lambda i, g, pt: (0, 0))],
        out_specs=pl.BlockSpec((CMP_BATCH, NSA_KV_HEADS, nck, KV_SLAB), lambda i, g, pt: (i, 0, g, 0)),
        scratch_shapes=[pltpu.VMEM((CMP_BATCH, SUBLANES, 2 * CMP_HIDDEN), F32),
                        pltpu.VMEM((CMP_BATCH, CMP_STRIDE, NSA_KV_HEADS, nck, KV_SLAB), F32),
                        pltpu.VMEM((SUBLANES, 2 * CMP_HIDDEN), F32)],
    )
    return pl.pallas_call(
        _cmp_sample_kernel,
        grid_spec=grid_spec,
        out_shape=jax.ShapeDtypeStruct((b, NSA_KV_HEADS, n_pages * CHUNKS_PER_PAGE, KV_SLAB), F32),
        compiler_params=_cparams(("arbitrary", "arbitrary")),
        name="cmp_sample",
    )(page_table_flat, *([cache_c] * (CMP_BATCH * CMP_PAGES)), perm, w1, w1p, w2, pe)


def _sel_sample_kernel(slopes_ref, q_ref, kcv_ref, ov_ref, oc_ref, idx_ref, *, past_len, n_sel_pad, t_valid):
    tp = SUBLANES
    p_grps = []
    for n in range(NSA_KV_HEADS):
        oc, p_grp = _sel_sample_attend(n, slopes_ref, q_ref[:, n * Q_SLAB:(n + 1) * Q_SLAB] * SCALE, kcv_ref[n],
                                       past_len)
        oc_ref[n] = oc
        p_grps.append(p_grp)
    imp_all = jnp.dot(jnp.concatenate(p_grps, axis=0), ov_ref[...], preferred_element_type=F32, precision=HIGHEST)
    for n in range(NSA_KV_HEADS):
        idx_ref[n] = _sel_sample_pick(imp_all[n * tp:(n + 1) * tp], past_len, n_sel_pad, t_valid)


def _sel_sample_attend(n, slopes_ref, qf, kcv, past_len):
    tp = SUBLANES
    rows = NSA_GROUP * tp
    n_rows_c = past_len // CMP_STRIDE
    q = jnp.concatenate([qf[:, g * HEAD_DIM:(g + 1) * HEAD_DIM] for g in range(NSA_GROUP)], axis=0).astype(BF16)
    tok = lax.broadcasted_iota(jnp.int32, (rows, 1), 0) % tp
    qpos = past_len + tok
    slope = _slope_col(slopes_ref, n, tp, rows)
    kc = kcv[:, :HEAD_DIM].astype(BF16)
    vc = kcv[:, HEAD_DIM:].astype(BF16)
    s = _dot_nt(q, kc)
    cp = lax.broadcasted_iota(jnp.int32, (1, n_rows_c), 1)
    end = (cp - 1) * CMP_STRIDE + (CMP_BLOCK - 1)
    dist = (qpos - end).astype(F32)
    valid = (cp >= 1) & (dist >= 0)
    s = jnp.where(valid, s - slope * dist, NEG_INF)
    e = jnp.exp(s - jnp.max(s, axis=-1, keepdims=True))
    p = jnp.where(valid, e / jnp.sum(e, axis=-1, keepdims=True), 0.0)
    o_c = _dot(p.astype(BF16), vc)
    oc = jnp.concatenate([o_c[g * tp:(g + 1) * tp] for g in range(NSA_GROUP)], axis=1)
    p_grp = p[0:tp]
    for g in range(1, NSA_GROUP):
        p_grp = p_grp + p[g * tp:(g + 1) * tp]
    return oc, p_grp


def _sel_sample_pick(imp, past_len, n_sel_pad, t_valid):
    tp = SUBLANES
    nb_past = past_len // SEL_BLOCK
    n_tail = -(-t_valid // SEL_BLOCK)
    n_sel = nb_past + n_tail
    j = lax.broadcasted_iota(jnp.int32, (tp, n_sel_pad), 1)
    cur = (past_len + lax.broadcasted_iota(jnp.int32, (tp, 1), 0)) // SEL_BLOCK
    forced = (j == 0) | (j == cur) | (j == cur - 1)
    imp = jnp.where(forced, FORCE_SCORE, jnp.where(j > cur, -FORCE_SCORE, imp))
    imp = jnp.where(j < n_sel, imp, REMOVED_SCORE)
    _, picks = _top_blocks(imp, min(N_SELECT, n_sel))
    kcol = lax.broadcasted_iota(jnp.int32, (tp, N_SELECT), 1)
    idx = jnp.zeros((tp, N_SELECT), jnp.int32)
    for kk, pk in enumerate(picks):
        idx = jnp.where(kcol == kk, pk, idx)
    return idx


def _sel_sample(proj8, kcv, slopes, b, past_len, t_valid):
    n_rows_c = past_len // CMP_STRIDE
    nb_past = past_len // SEL_BLOCK
    n_sel_pad = -(-(nb_past + 1) // LANES) * LANES
    overlap = jnp.asarray(_overlap_counts(np.arange(n_rows_c)[:, None] - 1, np.arange(n_sel_pad)[None, :]))
    grid_spec = pltpu.PrefetchScalarGridSpec(
        num_scalar_prefetch=1,
        grid=(b,),
        in_specs=[pl.BlockSpec((SUBLANES, NSA_WIDTH), lambda i, sl: (i, COL_Q // NSA_WIDTH)),
                  pl.BlockSpec((None, NSA_KV_HEADS, n_rows_c, KV_SLAB), lambda i, sl: (i, 0, 0, 0)),
                  pl.BlockSpec(overlap.shape, lambda i, sl: (0, 0))],
        out_specs=[pl.BlockSpec((None, NSA_KV_HEADS, SUBLANES, Q_SLAB), lambda i, sl: (i, 0, 0, 0)),
                   pl.BlockSpec((None, NSA_KV_HEADS, SUBLANES, N_SELECT), lambda i, sl: (i, 0, 0, 0))],
    )
    return pl.pallas_call(
        functools.partial(_sel_sample_kernel, past_len=past_len, n_sel_pad=n_sel_pad, t_valid=t_valid),
        grid_spec=grid_spec,
        out_shape=[jax.ShapeDtypeStruct((b, NSA_KV_HEADS, SUBLANES, Q_SLAB), F32),
                   jax.ShapeDtypeStruct((b, NSA_KV_HEADS, SUBLANES, N_SELECT), jnp.int32)],
        compiler_params=_cparams(("parallel",)),
        name="sel_sample",
    )(slopes, proj8, kcv, overlap)


def _nsa_sample_kernel(idx_ref, pt_ref, slopes_ref, cache_ref, q_ref, tail_ref, wnew_ref, wcache_ref, oc_ref,
                       misc_ref, o_ref, pages_a, pages_b, sem_ref, *, past_len, t_valid, n_pages):
    n_blk = t_valid * N_SELECT
    nb_past = past_len // SEL_BLOCK
    per_page = PAGE_SIZE // SEL_BLOCK
    pairs = NSA_KV_HEADS // 2
    step = pl.program_id(0) * pairs + pl.program_id(1)
    n_steps = pl.num_programs(0) * pairs
    head_a = step * 2
    head_b = head_a + 1
    next_a = ((step + 1) % n_steps) * 2

    def page_copy(bh, j, buf, sem_i):
        t, kk = j // N_SELECT, j % N_SELECT
        jb = jnp.minimum(idx_ref[(bh * SUBLANES + t) * N_SELECT + kk], nb_past - 1)
        page = pt_ref[(bh // NSA_KV_HEADS) * n_pages + jb // per_page]
        return pltpu.make_async_copy(cache_ref.at[page, bh % NSA_KV_HEADS], buf.at[j], sem_ref.at[sem_i])

    def wait_all(bh, buf, sem_i):
        def body(j, c):
            page_copy(bh, j, buf, sem_i).wait()
            return c
        lax.fori_loop(0, n_blk, body, 0)

    def compute(bh, h2, buf):
        n = bh % NSA_KV_HEADS
        slope = _slope_col(slopes_ref, n, 1, SUBLANES)
        gates = _pick_head_gates(misc_ref[:, 0:GATE_W], n)
        tl = tail_ref[:, h2 * KV_SLAB:(h2 + 1) * KV_SLAB]
        wn = wnew_ref[:, h2 * KV_SLAB:(h2 + 1) * KV_SLAB]
        win_k = wcache_ref[h2, 0:HEAD_DIM, :].astype(BF16)
        win_v = wcache_ref[h2, HEAD_DIM:KV_SLAB, :].astype(BF16)
        rows_out = []
        for t in range(t_valid):
            picked = [buf.at[t * N_SELECT + kk] for kk in range(N_SELECT)]
            jbs = [idx_ref[(bh * SUBLANES + t) * N_SELECT + kk] for kk in range(N_SELECT)]
            rows_out.append(_nsa_sample_token(t, q_ref[t:t + 1, h2 * Q_SLAB:(h2 + 1) * Q_SLAB] * SCALE, slope, picked,
                                              jbs, tl, wn, win_k, win_v, oc_ref[h2, t:t + 1, :], gates[t:t + 1, :],
                                              past_len))
        rows_out.append(jnp.zeros((SUBLANES - t_valid, Q_SLAB), F32))
        o_ref[:, h2 * Q_SLAB:(h2 + 1) * Q_SLAB] = jnp.concatenate(rows_out, axis=0)

    @pl.when(step == 0)
    def _():
        for j in range(n_blk):
            page_copy(head_a, j, pages_a, 0).start(priority=j % 2)

    for j in range(n_blk):
        page_copy(head_b, j, pages_b, 1).start(priority=j % 2)
    wait_all(head_a, pages_a, 0)
    compute(head_a, 0, pages_a)
    for j in range(n_blk):
        page_copy(next_a, j, pages_a, 0).start(priority=j % 2)
    wait_all(head_b, pages_b, 1)
    compute(head_b, 1, pages_b)

    @pl.when(step == n_steps - 1)
    def _():
        wait_all(next_a, pages_a, 0)


def _nsa_sample_token(t, qrow, slope, picked, jbs, tl, wn, win_k, win_v, oc_row, gates, past_len):
    nb_past = past_len // SEL_BLOCK
    g8 = SUBLANES
    q = jnp.concatenate([qrow[:, g * HEAD_DIM:(g + 1) * HEAD_DIM] for g in range(NSA_GROUP)]
                        + [jnp.zeros((g8 - NSA_GROUP, HEAD_DIM), F32)], axis=0).astype(BF16)
    qpos = past_len + t

    per_page = PAGE_SIZE // SEL_BLOCK
    k_all = jnp.concatenate([r[0:HEAD_DIM, :] for r in picked], axis=1).astype(BF16)
    v_all = jnp.concatenate([r[HEAD_DIM:KV_SLAB, :] for r in picked], axis=1).astype(BF16)
    jb_row = jnp.concatenate([jnp.full((1, PAGE_SIZE), jb, jnp.int32) for jb in jbs], axis=1)
    tail_count = jnp.zeros((), jnp.int32)
    for jb in jbs:
        tail_count = tail_count + (jb >= nb_past).astype(jnp.int32)
    lane = lax.broadcasted_iota(jnp.int32, (1, N_SELECT * PAGE_SIZE), 1) % PAGE_SIZE
    kpos = (jb_row // per_page) * PAGE_SIZE + lane
    dist = (qpos - kpos).astype(F32)
    ok = (dist >= 0) & (kpos // SEL_BLOCK == jb_row) & (jb_row < nb_past)
    sc = _dot(q, k_all) - slope * dist
    parts = [(jnp.where(ok, sc, NEG_INF), ok, v_all, True)]
    lane8 = lax.broadcasted_iota(jnp.int32, (1, SUBLANES), 1)
    dist = (t - lane8).astype(F32)
    ok = (dist >= 0) & (jnp.full((1, SUBLANES), tail_count, jnp.int32) > 0)
    sc = _dot_nt(q, tl[:, :HEAD_DIM].astype(BF16)) - slope * dist
    parts.append((jnp.where(ok, sc, NEG_INF), ok, tl[:, HEAD_DIM:].astype(BF16), False))

    def softmax_av(parts):
        m = parts[0][0].max(axis=-1, keepdims=True)
        for sc, _, _, _ in parts[1:]:
            m = jnp.maximum(m, sc.max(axis=-1, keepdims=True))
        l = jnp.zeros((g8, 1), F32)
        acc = jnp.zeros((g8, HEAD_DIM), F32)
        for sc, ok, v, v_transposed in parts:
            pr = jnp.where(ok, jnp.exp(sc - m), 0.0)
            l = l + pr.sum(axis=-1, keepdims=True)
            acc = acc + (_dot_nt(pr.astype(BF16), v) if v_transposed else _dot(pr.astype(BF16), v))
        return acc / l

    o_s = softmax_av(parts)

    buf_len = win_k.shape[1]
    lane_w = lax.broadcasted_iota(jnp.int32, (1, buf_len), 1)
    dist_c = (qpos - (past_len - buf_len + lane_w)).astype(F32)
    ok_c = (dist_c >= 0) & (dist_c < WINDOW)
    sc_c = jnp.where(ok_c, _dot(q, win_k) - slope * dist_c, NEG_INF)
    dist_n = (t - lane8).astype(F32)
    ok_n = (dist_n >= 0) & (dist_n < WINDOW)
    sc_n = jnp.where(ok_n, _dot_nt(q, wn[:, :HEAD_DIM].astype(BF16)) - slope * dist_n, NEG_INF)
    o_w = softmax_av([(sc_c, ok_c, win_v, True), (sc_n, ok_n, wn[:, HEAD_DIM:].astype(BF16), False)])

    outs = []
    for g in range(NSA_GROUP):
        outs.append(gates[:, 3 * g:3 * g + 1] * oc_row[:, g * HEAD_DIM:(g + 1) * HEAD_DIM]
                    + gates[:, 3 * g + 1:3 * g + 2] * o_s[g:g + 1]
                    + gates[:, 3 * g + 2:3 * g + 3] * o_w[g:g + 1])
    return jnp.concatenate(outs, axis=1)


def _nsa_sample(proj8, cache_s, cache_w, o_c, idx_flat, page_table_flat, slopes, b, t_valid, past_len):
    n_pages = past_len // PAGE_SIZE
    buf_len = cache_w.shape[-1]
    pairs = NSA_KV_HEADS // 2
    n_blk = t_valid * N_SELECT
    pages = pltpu.VMEM((n_blk, KV_SLAB, PAGE_SIZE), F32)
    grid_spec = pltpu.PrefetchScalarGridSpec(
        num_scalar_prefetch=3,
        grid=(b, pairs),
        in_specs=[pl.BlockSpec(memory_space=pl.ANY),
                  pl.BlockSpec((SUBLANES, 2 * Q_SLAB), lambda i, p, *_: (i, COL_Q // (2 * Q_SLAB) + p)),
                  pl.BlockSpec((SUBLANES, 2 * KV_SLAB), lambda i, p, *_: (i, COL_KVS // (2 * KV_SLAB) + p)),
                  pl.BlockSpec((SUBLANES, 2 * KV_SLAB), lambda i, p, *_: (i, COL_KVW // (2 * KV_SLAB) + p)),
                  pl.BlockSpec((None, 2, KV_SLAB, buf_len), lambda i, p, *_: (i, p, 0, 0)),
                  pl.BlockSpec((None, 2, SUBLANES, Q_SLAB), lambda i, p, *_: (i, p, 0, 0)),
                  pl.BlockSpec((SUBLANES, LANES), lambda i, p, *_: (i, COL_MISC // LANES))],
        out_specs=pl.BlockSpec((SUBLANES, 2 * Q_SLAB), lambda i, p, *_: (i, p)),
        scratch_shapes=[pages, pages, pltpu.SemaphoreType.DMA((2,))],
    )
    return pl.pallas_call(
        functools.partial(_nsa_sample_kernel, past_len=past_len, t_valid=t_valid, n_pages=n_pages),
        grid_spec=grid_spec,
        out_shape=jax.ShapeDtypeStruct((b * SUBLANES, NSA_WIDTH), F32),
        compiler_params=_cparams(("arbitrary", "arbitrary")),
        name="nsa_sample",
    )(idx_flat, page_table_flat, slopes, cache_s, proj8, proj8, proj8, cache_w, o_c, proj8)


def _alibi_slopes():
    h = np.arange(1, NSA_HEADS + 1, dtype=np.float32)
    return jnp.asarray(2.0 ** (-8.0 * h / NSA_HEADS), dtype=F32)


def _permute_w_in(w_in):
    offs = np.cumsum([0, NSA_WIDTH, KV_WIDTH, KV_WIDTH, KV_WIDTH, GATE_W, GLA_HEADS * GLA_DK, GLA_HEADS * GLA_DK,
                      GLA_WIDTH, GLA_WIDTH, GLA_RANK])
    w_bf = w_in.astype(BF16)
    piece = [w_bf[:, offs[k]:offs[k + 1]] for k in range(10)]
    q, kvc, kvs, kvw, gl, gq, gk, gv, gr, ga = piece
    pad = jnp.zeros((w_in.shape[0], LANES - GATE_W - GLA_RANK), BF16)
    return jnp.concatenate([q, gv, gr, kvc, kvs, kvw, gq, gk, gl, ga, pad], axis=1)


def _row_tile(m, pref):
    return pref if m % pref == 0 else m


def _dense_tail(h_in, o_nsa, o_gla, p_emb, wts, conv_p1, conv_p2, seq_rows, last_layer, tm):
    (g_nsa, w_out, g_ffn, w_up, conv_w, conv_b, w_down, w_ple_proj, g_ple, w_ple_gate, g_final) = wts
    h = _out_proj(o_nsa, o_gla, h_in, g_nsa, w_out, tm, D_MODEL // 2)
    h, tails = _ffn(h, g_ffn, w_up, conv_w, conv_b, w_down, conv_p1, conv_p2, tm, FFN_COLS, seq_rows)
    y = _ple(h, p_emb, w_ple_gate, w_ple_proj, g_ple, g_final, tm, last_layer)
    return y, tails


def kernel(x_prompt, x_sample, p_prompt, p_sample, cache_cmp_kv, cache_sel_kv, cache_win_kv, state_gla, state_ffn_conv, page_table, g_attn, w_in, w_cmp_k1, w_cmp_k2, pe_cmp_k, w_cmp_v1, w_cmp_v2, pe_cmp_v, w_gla_a2, b_gla_a, g_nsa_out, g_gla_out, w_out, g_ffn, w_up, conv_w, conv_b, w_down, w_ple_proj, g_ple, w_ple_gate, g_final):
    depth = w_in.shape[0]
    bp, tp, _ = x_prompt.shape
    bs, ts, _ = x_sample.shape
    n_pages = page_table.shape[1]
    past_len = n_pages * PAGE_SIZE
    n_pool = cache_cmp_kv.shape[1]
    assert tp % TQ == 0 and tp >= WINDOW and ts <= SUBLANES and ts <= SEL_BLOCK and ts >= CONV_W - 1
    assert n_pages % CMP_PAGES == 0 and tp // SEL_BLOCK >= N_SELECT
    slopes = _alibi_slopes()
    pt_flat = page_table.reshape(-1).astype(jnp.int32)
    kv_shape = (NSA_KV_HEADS, 2, HEAD_DIM)

    hp = x_prompt.reshape(bp * tp, D_MODEL)
    hs = x_sample.reshape(bs * ts, D_MODEL)
    new_p = [[] for _ in range(5)]
    new_s = [[] for _ in range(5)]
    tm_p = _row_tile(bp * tp, DENSE_ROWS)
    tm_s = bs * ts
    for i in range(depth):
        last = i == depth - 1
        w_in_p = _permute_w_in(w_in[i])
        g_a = g_attn[i].reshape(1, D_MODEL)
        w1c, w2c, pec = _cmp_weights(w_cmp_k1[i], w_cmp_k2[i], pe_cmp_k[i], w_cmp_v1[i], w_cmp_v2[i], pe_cmp_v[i])
        b_a = b_gla_a[i].reshape(1, -1)
        g_go = g_gla_out[i].reshape(1, GLA_DV)
        wts = (g_nsa_out[i].reshape(1, -1), w_out[i].astype(BF16), g_ffn[i].reshape(1, -1), w_up[i].astype(BF16),
               conv_w[i], conv_b[i].reshape(1, -1), w_down[i].astype(BF16), w_ple_proj[i].astype(BF16),
               g_ple[i].reshape(1, -1), w_ple_gate[i].astype(BF16), g_final.reshape(1, -1))

        proj = _norm_matmul(hp, g_a, w_in_p, _row_tile(bp * tp, IN_PROJ_ROWS), IN_PROJ_COLS)
        kcv = _cmp_prompt(proj, bp, tp, w1c, w2c, pec)
        o_nsa = _nsa_prompt(proj, kcv, slopes, bp, tp)
        s0 = jnp.zeros((bp, GLA_HEADS, GLA_DK, GLA_DV), F32)
        o_gla, s_new = _gla(proj, w_gla_a2[i], b_a, g_go, s0, bp, tp, 64, 16, 64)
        zbuf = jnp.zeros((bp, CONV_W - 1, D_FF), F32)
        hp, tails = _dense_tail(hp, o_nsa, o_gla, p_prompt[i].reshape(bp * tp, PLE_DIM), wts, zbuf, zbuf, tp, last, tm_p)
        proj3 = proj.reshape(bp, tp, D_IN_PAD)
        new_p[0].append(proj3[:, :, COL_KVC:COL_KVC + KV_WIDTH].reshape((bp, tp) + kv_shape))
        new_p[1].append(proj3[:, :, COL_KVS:COL_KVS + KV_WIDTH].reshape((bp, tp) + kv_shape))
        new_p[2].append(proj3[:, tp - WINDOW:, COL_KVW:COL_KVW + KV_WIDTH].reshape((bp, WINDOW) + kv_shape))
        new_p[3].append(s_new)
        tiles_per_seq = tp // tm_p
        new_p[4].append(tails.reshape(bp, tiles_per_seq, SUBLANES, D_FF)[:, -1, SUBLANES - (CONV_W - 1):, :])

        proj_s = _norm_matmul(hs, g_a, w_in_p, tm_s, IN_PROJ_COLS)
        proj8 = jnp.pad(proj_s.reshape(bs, ts, D_IN_PAD), ((0, 0), (0, SUBLANES - ts), (0, 0))).reshape(bs * SUBLANES, D_IN_PAD)
        cache_c = jnp.transpose(cache_cmp_kv[i], (0, 2, 3, 4, 1)).reshape(n_pool, KV_WIDTH, PAGE_SIZE)
        cache_s = jnp.transpose(cache_sel_kv[i], (0, 2, 3, 4, 1)).reshape(n_pool, NSA_KV_HEADS, KV_SLAB, PAGE_SIZE)
        cache_w = jnp.transpose(cache_win_kv[i], (0, 2, 3, 4, 1)).reshape(bs, NSA_KV_HEADS, KV_SLAB, -1)
        kcv_s = _cmp_sample(cache_c, pt_flat, w1c, w2c, pec, bs, n_pages)
        o_c, idx = _sel_sample(proj8, kcv_s, slopes, bs, past_len, ts)
        o_nsa_s = _nsa_sample(proj8, cache_s, cache_w, o_c, idx.reshape(-1), pt_flat, slopes, bs, ts, past_len)
        o_nsa_s = o_nsa_s.reshape(bs, SUBLANES, NSA_WIDTH)[:, :ts].reshape(bs * ts, NSA_WIDTH)
        o_gla_s, s_new_s = _gla(proj8, w_gla_a2[i], b_a, g_go, state_gla[i].astype(F32), bs, SUBLANES, SUBLANES, SUBLANES, ts)
        o_gla_s = o_gla_s.reshape(bs, SUBLANES, GLA_WIDTH)[:, :ts].reshape(bs * ts, GLA_WIDTH)
        buf = state_ffn_conv[i]
        zrow = jnp.zeros((bs, ts - 1, D_FF), F32)
        p1 = jnp.concatenate([buf[:, 1:2], zrow], axis=1).reshape(bs * ts, D_FF)
        p2 = jnp.concatenate([buf, jnp.zeros((bs, ts - 2, D_FF), F32)], axis=1).reshape(bs * ts, D_FF)
        hs, tails_s = _dense_tail(hs, o_nsa_s, o_gla_s, p_sample[i].reshape(bs * ts, PLE_DIM), wts, p1, p2, ts, last, tm_s)
        ps3 = proj_s.reshape(bs, ts, D_IN_PAD)
        new_s[0].append(ps3[:, :, COL_KVC:COL_KVC + KV_WIDTH].reshape((bs, ts) + kv_shape))
        new_s[1].append(ps3[:, :, COL_KVS:COL_KVS + KV_WIDTH].reshape((bs, ts) + kv_shape))
        new_s[2].append(ps3[:, :, COL_KVW:COL_KVW + KV_WIDTH].reshape((bs, ts) + kv_shape))
        new_s[3].append(s_new_s)
        new_s[4].append(tails_s.reshape(bs, ts, D_FF)[:, ts - (CONV_W - 1):, :])

    y_prompt = hp.reshape(bp, tp, D_MODEL)
    y_sample = hs.reshape(bs, ts, D_MODEL)
    cmp_p, sel_p, win_p, gla_p, conv_p = [jnp.stack(l) for l in new_p]
    cmp_s, sel_s, win_s, gla_s, conv_s = [jnp.stack(l) for l in new_s]
    return (y_prompt, y_sample, cmp_p, sel_p, win_p, gla_p, conv_p, cmp_s, sel_s, win_s, gla_s, conv_s)
```
